```python
import jax, jax.numpy as jnp
from jax import lax
import numpy as np

D_MODEL = 2048
BATCH = 2
SEQ = 8192
DEPTH = 1
DEC_BATCH = 16
DEC_SEQ = 32
PAST_LEN = 1024

CHUNK = 64
EPS = 1e-6
VA = 128
HA = (D_MODEL // 2) // VA
NOPE = 128
ROPE = 64
QKA = NOPE + ROPE
Q_LORA = 512
KV_LORA = 256
ROPE_BASE = 10000.0
SCALE_A = QKA ** -0.5
DHB = 128
HB = (D_MODEL // 2) // DHB
BAND_CHUNKS = 8
BAND_PAST = BAND_CHUNKS * CHUNK
BAND_KEYS = (BAND_CHUNKS + 1) * CHUNK
MAX_REL = 128
N_REL = 2 * MAX_REL + 1
SCALE_B = DHB ** -0.5
MIX_WIDTH = HA * VA + HB * DHB
IN_COLS = Q_LORA + KV_LORA + ROPE + 3 * HB * DHB
SPLITS = [Q_LORA, Q_LORA + KV_LORA, Q_LORA + KV_LORA + ROPE,
          Q_LORA + KV_LORA + ROPE + HB * DHB, Q_LORA + KV_LORA + ROPE + 2 * HB * DHB]
D_FF = 4 * D_MODEL
Q_BLOCK = 128
NEG = -1e30

kernel_name = 'hybrid_mla_chunkband_stream_step'


def rms_norm(x, g):
    xf = x.astype(jnp.float32)
    y = xf * lax.rsqrt(jnp.mean(xf * xf, axis=-1, keepdims=True) + EPS)
    return (y * g.astype(jnp.float32)).astype(x.dtype)


def rope_part(x, pos):
    inv = 1.0 / (ROPE_BASE ** (jnp.arange(0, ROPE, 2, dtype=jnp.float32) / ROPE))
    ang = pos.astype(jnp.float32)[:, None] * inv[None, :]
    c = jnp.cos(ang)[:, None, :].astype(x.dtype)
    s = jnp.sin(ang)[:, None, :].astype(x.dtype)
    x1, x2 = jnp.split(x[..., NOPE:], 2, axis=-1)
    return jnp.concatenate([x[..., :NOPE], x1 * c - x2 * s, x1 * s + x2 * c], axis=-1)


def mixer_inputs(hn, pos, w_in, g_cq, w_uq, g_ckv, g_qa, g_qb, g_kb):
    B, T, _ = hn.shape
    c_q, c_kv, kpe, qb, kb, vb = jnp.split(hn @ w_in, SPLITS, axis=-1)
    qa = (rms_norm(c_q, g_cq) @ w_uq).reshape(B, T, HA, QKA)
    qa = rope_part(rms_norm(qa, g_qa), pos)
    ckv = rms_norm(c_kv, g_ckv)
    qb = rms_norm(qb.reshape(B, T, HB, DHB), g_qb)
    kb = rms_norm(kb.reshape(B, T, HB, DHB), g_kb)
    vb = vb.reshape(B, T, HB, DHB)
    return qa, ckv, kpe, qb, kb, vb


def mla_keys_values(ckv, kpe, pos, w_uk, w_uv, g_ka):
    B, T, _ = ckv.shape
    k_nope = (ckv @ w_uk).reshape(B, T, HA, NOPE)
    k = jnp.concatenate([k_nope, jnp.broadcast_to(kpe[:, :, None, :], (B, T, HA, ROPE))], axis=-1)
    k = rope_part(rms_norm(k, g_ka), pos)
    v = (ckv @ w_uv).reshape(B, T, HA, VA)
    return k, v


def dense_attention(q, k, v, scale, bias):
    B, Q = q.shape[:2]
    s = jnp.einsum('bqhd,bkhd->bhqk', q, k, preferred_element_type=jnp.float32) * scale
    if bias is not None:
        s = s + bias.astype(jnp.float32)[None]
    p = jax.nn.softmax(s, axis=-1).astype(v.dtype)
    return jnp.einsum('bhqk,bkhd->bqhd', p, v).reshape(B, Q, -1)


def mla_prompt(q, k, v):
    B, S = q.shape[:2]
    nblk = S // Q_BLOCK
    qblocks = q.reshape(B, nblk, Q_BLOCK, HA, QKA).transpose(1, 0, 2, 3, 4)
    key_chunk = jnp.arange(S) // CHUNK

    def one_block(args):
        qblk, i = args
        q_chunk = (i * Q_BLOCK + jnp.arange(Q_BLOCK)) // CHUNK
        s = jnp.einsum('bqhd,bkhd->bhqk', qblk, k, preferred_element_type=jnp.float32) * SCALE_A
        s = jnp.where((key_chunk[None, :] <= q_chunk[:, None])[None, None], s, NEG)
        p = jax.nn.softmax(s, axis=-1).astype(v.dtype)
        return jnp.einsum('bhqk,bkhd->bqhd', p, v)

    o = lax.map(one_block, (qblocks, jnp.arange(nblk)))
    return o.transpose(1, 0, 2, 3, 4).reshape(B, S, HA * VA)


def band_prompt(q, k, v, rel_bias):
    B, S = q.shape[:2]
    NC = S // CHUNK
    qc = q.reshape(B, NC, CHUNK, HB, DHB)
    pad = ((0, 0), (BAND_CHUNKS, 0), (0, 0), (0, 0), (0, 0))
    kc = jnp.pad(k.reshape(B, NC, CHUNK, HB, DHB), pad)
    vc = jnp.pad(v.reshape(B, NC, CHUNK, HB, DHB), pad)
    kband = jnp.concatenate([kc[:, j:j + NC] for j in range(BAND_CHUNKS + 1)], axis=2)
    vband = jnp.concatenate([vc[:, j:j + NC] for j in range(BAND_CHUNKS + 1)], axis=2)
    src_chunk = jnp.arange(NC)[:, None] - BAND_CHUNKS + jnp.arange(BAND_CHUNKS + 1)[None, :]
    valid = jnp.repeat(src_chunk >= 0, CHUNK, axis=1)
    rel = jnp.arange(CHUNK)[:, None] - (jnp.arange(BAND_KEYS) - BAND_PAST)[None, :]
    bias = rel_bias[:, jnp.clip(rel, -MAX_REL, MAX_REL) + MAX_REL].astype(jnp.float32)
    s = jnp.einsum('bcqhd,bckhd->bchqk', qc, kband, preferred_element_type=jnp.float32) * SCALE_B
    s = jnp.where(valid[None, :, None, None, :], s + bias[None, None], NEG)
    p = jax.nn.softmax(s, axis=-1).astype(v.dtype)
    o = jnp.einsum('bchqk,bckhd->bcqhd', p, vband)
    return o.reshape(B, S, HB * DHB)


def merge_and_ffn(h, oa, ob, w_o, norm_ffn, w_up, w_down):
    h = h + jnp.concatenate([oa, ob], axis=-1) @ w_o
    u = rms_norm(h, norm_ffn) @ w_up
    return h + jnp.square(jax.nn.relu(u)) @ w_down


def setup_inputs(seed: int = 0) -> dict:
    key = jax.random.key(seed)
    ks = jax.random.split(key, 24)
    f32 = jnp.float32
    L_band = min(BAND_PAST, PAST_LEN)

    def w(k, shape, fan_in):
        return jax.random.normal(k, (DEPTH,) + shape, f32) * (fan_in ** -0.5)

    def gain(k, n):
        return 1.0 + 0.01 * jax.random.normal(k, (DEPTH, n), f32)

    return {
        'x_prompt': jax.random.normal(ks[0], (BATCH, SEQ, D_MODEL), f32),
        'x_sample': jax.random.normal(ks[1], (DEC_BATCH, DEC_SEQ, D_MODEL), f32),
        'cache_mla_ckv': jax.random.normal(ks[2], (DEPTH, DEC_BATCH, PAST_LEN, KV_LORA), f32),
        'cache_mla_kpe': jax.random.normal(ks[3], (DEPTH, DEC_BATCH, PAST_LEN, ROPE), f32),
        'cache_band_k': jax.random.normal(ks[4], (DEPTH, DEC_BATCH, L_band, HB, DHB), f32),
        'cache_band_v': jax.random.normal(ks[5], (DEPTH, DEC_BATCH, L_band, HB, DHB), f32),
        'norm_mix': gain(ks[6], D_MODEL),
        'w_in': w(ks[7], (D_MODEL, IN_COLS), D_MODEL),
        'g_cq': gain(ks[8], Q_LORA),
        'w_uq': w(ks[9], (Q_LORA, HA * QKA), Q_LORA),
        'g_ckv': gain(ks[10], KV_LORA),
        'w_uk': w(ks[11], (KV_LORA, HA * NOPE), KV_LORA),
        'w_uv': w(ks[12], (KV_LORA, HA * VA), KV_LORA),
        'g_qa': gain(ks[13], QKA),
        'g_ka': gain(ks[14], QKA),
        'g_qb': gain(ks[15], DHB),
        'g_kb': gain(ks[16], DHB),
        'rel_bias': 0.5 * jax.random.normal(ks[17], (DEPTH, HB, N_REL), f32),
        'w_o': w(ks[18], (MIX_WIDTH, D_MODEL), MIX_WIDTH),
        'norm_ffn': gain(ks[19], D_MODEL),
        'w_up': w(ks[20], (D_MODEL, D_FF), D_MODEL),
        'w_down': w(ks[21], (D_FF, D_MODEL), D_FF),
    }


def reference(x_prompt, x_sample, cache_mla_ckv, cache_mla_kpe, cache_band_k, cache_band_v,
              norm_mix, w_in, g_cq, w_uq, g_ckv, w_uk, w_uv, g_qa, g_ka, g_qb, g_kb, rel_bias,
              w_o, norm_ffn, w_up, w_down):
    S = x_prompt.shape[1]
    T = x_sample.shape[1]
    P = cache_mla_ckv.shape[2]
    Lb = cache_band_k.shape[2]
    keep_p = min(BAND_PAST, S)
    pos_p = jnp.arange(S, dtype=jnp.int32)
    pos_s = P + jnp.arange(T, dtype=jnp.int32)
    pos_hist = jnp.arange(P + T, dtype=jnp.int32)
    band_kpos = jnp.concatenate([jnp.arange(P - Lb, P, dtype=jnp.int32), pos_s])
    band_rel = jnp.clip(pos_s[:, None] - band_kpos[None, :], -MAX_REL, MAX_REL) + MAX_REL

    hp, hs = x_prompt, x_sample
    ckv_p_l, kpe_p_l, bk_p_l, bv_p_l = [], [], [], []
    ckv_s_l, kpe_s_l, bk_s_l, bv_s_l = [], [], [], []
    for l in range(DEPTH):
        qa, ckv, kpe, qb, kb, vb = mixer_inputs(rms_norm(hp, norm_mix[l]), pos_p, w_in[l], g_cq[l],
                                                w_uq[l], g_ckv[l], g_qa[l], g_qb[l], g_kb[l])
        ka, va = mla_keys_values(ckv, kpe, pos_p, w_uk[l], w_uv[l], g_ka[l])
        oa = mla_prompt(qa, ka, va)
        ob = band_prompt(qb, kb, vb, rel_bias[l])
        hp = merge_and_ffn(hp, oa, ob, w_o[l], norm_ffn[l], w_up[l], w_down[l])
        ckv_p_l.append(ckv)
        kpe_p_l.append(kpe)
        bk_p_l.append(kb[:, S - keep_p:])
        bv_p_l.append(vb[:, S - keep_p:])

        qa_s, ckv_s, kpe_s, qb_s, kb_s, vb_s = mixer_inputs(rms_norm(hs, norm_mix[l]), pos_s, w_in[l], g_cq[l],
                                                            w_uq[l], g_ckv[l], g_qa[l], g_qb[l], g_kb[l])
        ckv_all = jnp.concatenate([cache_mla_ckv[l].astype(ckv_s.dtype), ckv_s], axis=1)
        kpe_all = jnp.concatenate([cache_mla_kpe[l].astype(kpe_s.dtype), kpe_s], axis=1)
        ka_s, va_s = mla_keys_values(ckv_all, kpe_all, pos_hist, w_uk[l], w_uv[l], g_ka[l])
        oa_s = dense_attention(qa_s, ka_s, va_s, SCALE_A, None)
        kb_all = jnp.concatenate([cache_band_k[l].astype(kb_s.dtype), kb_s], axis=1)
        vb_all = jnp.concatenate([cache_band_v[l].astype(vb_s.dtype), vb_s], axis=1)
        ob_s = dense_attention(qb_s, kb_all, vb_all, SCALE_B, rel_bias[l][:, band_rel])
        hs = merge_and_ffn(hs, oa_s, ob_s, w_o[l], norm_ffn[l], w_up[l], w_down[l])
        ckv_s_l.append(ckv_s)
        kpe_s_l.append(kpe_s)
        bk_s_l.append(kb_s)
        bv_s_l.append(vb_s)

    ckv_prompt = jnp.stack(ckv_p_l)
    kpe_prompt = jnp.stack(kpe_p_l)
    bandk_prompt = jnp.stack(bk_p_l)
    bandv_prompt = jnp.stack(bv_p_l)
    ckv_sample = jnp.stack(ckv_s_l)
    kpe_sample = jnp.stack(kpe_s_l)
    bandk_sample = jnp.stack(bk_s_l)
    bandv_sample = jnp.stack(bv_s_l)
    return (hp, hs, ckv_prompt, kpe_prompt, bandk_prompt, bandv_prompt,
            ckv_sample, kpe_sample, bandk_sample, bandv_sample)
```

```python
import functools

import jax
import jax.numpy as jnp
from jax import lax
from jax.experimental import pallas as pl
from jax.experimental.pallas import tpu as pltpu

CHUNK = 64
EPS = 1e-6
NOPE = 128
ROPE = 64
HALF_ROPE = ROPE // 2
QK_MLA = NOPE + ROPE
V_MLA = 128
D_BAND = 128
BAND_CHUNKS = 8
BAND_PAST = BAND_CHUNKS * CHUNK
MAX_REL = 128
ROPE_BASE = 10000.0
Q_LORA = 512
KV_LORA = 256
NEG = -1e30

LANES = 128
QK_PAD = 2 * LANES
VMEM_LIMIT = 56 * 1024 * 1024

BF16 = jnp.bfloat16
F32 = jnp.float32

NT_DIMS = (((1,), (1,)), ((), ()))


def _params(*sem):
    return pltpu.CompilerParams(dimension_semantics=sem, vmem_limit_bytes=VMEM_LIMIT)


def _rms(x, g):
    ms = jnp.mean(x * x, axis=-1, keepdims=True)
    return x * lax.rsqrt(ms + EPS) * g


def _dot(a, b):
    return jnp.dot(a, b, preferred_element_type=F32)


def _dot_nt(a, b):
    return lax.dot_general(a, b, NT_DIMS, preferred_element_type=F32)


def _proj_kernel(x_ref, nm_ref, wcq_ref, wckv_ref, wkpe_ref, wqb_ref, wkb_ref, wvb_ref,
                 gcq_ref, gckv_ref, gqb_ref, gkb_ref,
                 cqn_ref, ckv_ref, kpe_ref, qb_ref, kb_ref, vb_ref, kbt_ref, vbt_ref,
                 *, n_heads, tail_start):
    t = pl.program_id(1)
    xn = _rms(x_ref[0], nm_ref[...]).astype(BF16)
    cqn_ref[0] = _rms(_dot(xn, wcq_ref[...]), gcq_ref[...]).astype(BF16)
    ckv_ref[0] = _rms(_dot(xn, wckv_ref[...]), gckv_ref[...])
    kpe_ref[0] = _dot(xn, wkpe_ref[...])
    qb = _dot(xn, wqb_ref[...])
    kb = _dot(xn, wkb_ref[...])
    vb = _dot(xn, wvb_ref[...])
    vb_ref[0] = vb.astype(BF16)
    kbn = []
    for h in range(n_heads):
        sl = slice(h * D_BAND, (h + 1) * D_BAND)
        qb_ref[0, :, sl] = _rms(qb[:, sl], gqb_ref[...]).astype(BF16)
        kbn.append(_rms(kb[:, sl], gkb_ref[...]))
    kbn = jnp.concatenate(kbn, axis=-1)
    kb_ref[0] = kbn.astype(BF16)

    @pl.when(t >= tail_start)
    def _():
        kbt_ref[0] = kbn
        vbt_ref[0] = vb


def _proj(x, norm_mix, w, g_cq, g_ckv, g_qb, g_kb, *, keep, tm):
    B, S, D = x.shape
    n_heads = w["qb"].shape[1] // D_BAND
    hw = n_heads * D_BAND
    nt = S // tm
    tail_start = nt - keep // tm
    row = lambda b, t: (b, t, 0)
    const = lambda b, t: (0, 0)
    tail = lambda b, t: (b, jnp.maximum(t - tail_start, 0), 0)
    wspec = lambda a: pl.BlockSpec(a.shape, const)
    out_shapes = (
        jax.ShapeDtypeStruct((B, S, Q_LORA), BF16),
        jax.ShapeDtypeStruct((B, S, KV_LORA), F32),
        jax.ShapeDtypeStruct((B, S, LANES), F32),
        jax.ShapeDtypeStruct((B, S, hw), BF16),
        jax.ShapeDtypeStruct((B, S, hw), BF16),
        jax.ShapeDtypeStruct((B, S, hw), BF16),
        jax.ShapeDtypeStruct((B, keep, hw), F32),
        jax.ShapeDtypeStruct((B, keep, hw), F32),
    )
    out_specs = (
        pl.BlockSpec((1, tm, Q_LORA), row),
        pl.BlockSpec((1, tm, KV_LORA), row),
        pl.BlockSpec((1, tm, LANES), row),
        pl.BlockSpec((1, tm, hw), row),
        pl.BlockSpec((1, tm, hw), row),
        pl.BlockSpec((1, tm, hw), row),
        pl.BlockSpec((1, tm, hw), tail),
        pl.BlockSpec((1, tm, hw), tail),
    )
    weights = (w["cq"], w["ckv"], w["kpe"], w["qb"], w["kb"], w["vb"])
    gains = (g_cq, g_ckv, g_qb, g_kb)
    return pl.pallas_call(
        functools.partial(_proj_kernel, n_heads=n_heads, tail_start=tail_start),
        grid=(B, nt),
        in_specs=[pl.BlockSpec((1, tm, D), row), wspec(norm_mix)]
        + [wspec(a) for a in weights] + [wspec(a) for a in gains],
        out_specs=out_specs,
        out_shape=out_shapes,
        compiler_params=_params("arbitrary", "arbitrary"),
        name="proj",
    )(x, norm_mix, *weights, *gains)


def _q_up_kernel(cqn_ref, w_ref, gn_ref, gr_ref, grs_ref, cos_ref, sin_ref, q_ref, *, n_heads):
    hw = n_heads * LANES
    y = _dot(cqn_ref[0], w_ref[...])
    cos = cos_ref[...]
    sin = sin_ref[...]
    for h in range(n_heads):
        sl = slice(h * LANES, (h + 1) * LANES)
        n = y[:, sl]
        r = y[:, hw + h * LANES: hw + (h + 1) * LANES]
        rs = y[:, 2 * hw + h * LANES: 2 * hw + (h + 1) * LANES]
        ss = jnp.sum(n * n, axis=-1, keepdims=True) + jnp.sum(r * r, axis=-1, keepdims=True)
        rinv = lax.rsqrt(ss * (1.0 / QK_MLA) + EPS)
        q_ref[0, h, :, 0:LANES] = (n * rinv * gn_ref[...]).astype(BF16)
        rot = (r * rinv * gr_ref[...]) * cos + (rs * rinv * grs_ref[...]) * sin
        q_ref[0, h, :, LANES:QK_PAD] = rot.astype(BF16)


def _q_up(cqn, w, g_n, g_r, g_rs, cos, sin, *, tm):
    B, S, _ = cqn.shape
    n_heads = w.shape[1] // (3 * LANES)
    row = lambda b, t: (b, t, 0)
    const = lambda b, t: (0, 0)
    pos = lambda b, t: (t, 0)
    return pl.pallas_call(
        functools.partial(_q_up_kernel, n_heads=n_heads),
        grid=(B, S // tm),
        in_specs=[pl.BlockSpec((1, tm, Q_LORA), row), pl.BlockSpec(w.shape, const),
                  pl.BlockSpec(g_n.shape, const), pl.BlockSpec(g_r.shape, const),
                  pl.BlockSpec(g_rs.shape, const),
                  pl.BlockSpec((tm, LANES), pos), pl.BlockSpec((tm, LANES), pos)],
        out_specs=pl.BlockSpec((1, n_heads, tm, QK_PAD), lambda b, t: (b, 0, t, 0)),
        out_shape=jax.ShapeDtypeStruct((B, n_heads, S, QK_PAD), BF16),
        compiler_params=_params("arbitrary", "arbitrary"),
        name="q_up",
    )(cqn, w, g_n, g_r, g_rs, cos, sin)


def _kv_up_kernel(ckv_ref, kpe_ref, wuk_ref, wuvt_ref, gn_ref, gr_ref, cos_ref, sin_ref,
                  k_ref, vt_ref, *, n_heads):
    c = ckv_ref[0].astype(BF16)
    kn = _dot(c, wuk_ref[...])
    vt = _dot_nt(wuvt_ref[...], c)
    a = kpe_ref[0] * gr_ref[...]
    sspe = jnp.sum(kpe_ref[0] * kpe_ref[0], axis=-1, keepdims=True)
    a_sw = pltpu.roll(a, HALF_ROPE, 1) + pltpu.roll(a, LANES - HALF_ROPE, 1)
    rot = a * cos_ref[...] + a_sw * sin_ref[...]
    for h in range(n_heads):
        n = kn[:, h * NOPE:(h + 1) * NOPE]
        ss = jnp.sum(n * n, axis=-1, keepdims=True) + sspe
        rinv = lax.rsqrt(ss * (1.0 / QK_MLA) + EPS)
        k_ref[0, h, :, 0:LANES] = (n * rinv * gn_ref[...]).astype(BF16)
        k_ref[0, h, :, LANES:QK_PAD] = (rot * rinv).astype(BF16)
        vt_ref[0, h] = vt[h * V_MLA:(h + 1) * V_MLA, :].astype(BF16)


def _kv_up(ckv, kpe_pad, w_uk, w_uvt, g_n, g_r, cos, sin, *, tm):
    B, S, _ = ckv.shape
    n_heads = w_uk.shape[1] // NOPE
    row = lambda b, t: (b, t, 0)
    const = lambda b, t: (0, 0)
    pos = lambda b, t: (t, 0)
    return pl.pallas_call(
        functools.partial(_kv_up_kernel, n_heads=n_heads),
        grid=(B, S // tm),
        in_specs=[pl.BlockSpec((1, tm, KV_LORA), row), pl.BlockSpec((1, tm, LANES), row),
                  pl.BlockSpec(w_uk.shape, const), pl.BlockSpec(w_uvt.shape, const),
                  pl.BlockSpec(g_n.shape, const), pl.BlockSpec(g_r.shape, const),
                  pl.BlockSpec((tm, LANES), pos), pl.BlockSpec((tm, LANES), pos)],
        out_specs=(pl.BlockSpec((1, n_heads, tm, QK_PAD), lambda b, t: (b, 0, t, 0)),
                   pl.BlockSpec((1, n_heads, V_MLA, tm), lambda b, t: (b, 0, 0, t))),
        out_shape=(jax.ShapeDtypeStruct((B, n_heads, S, QK_PAD), BF16),
                   jax.ShapeDtypeStruct((B, n_heads, V_MLA, S), BF16)),
        compiler_params=_params("arbitrary", "arbitrary"),
        name="kv_up",
    )(ckv, kpe_pad, w_uk, w_uvt, g_n, g_r, cos, sin)


def _mla_flash_kernel(q_ref, k_ref, vt_ref, o_ref, m_sc, l_sc, acc_sc, *, tq, tk, scale):
    i = pl.program_id(2)
    q = q_ref[0, 0]
    m_sc[...] = jnp.full(m_sc.shape, -jnp.inf, F32)
    l_sc[...] = jnp.zeros(l_sc.shape, F32)
    acc_sc[...] = jnp.zeros(acc_sc.shape, F32)

    def step(j, mask):
        k0 = pl.multiple_of(j * tk, tk)
        st = _dot_nt(k_ref[0, 0, pl.ds(k0, tk), :], q) * scale
        if mask is not None:
            st = jnp.where(mask, st, NEG)
        m_prev = m_sc[...]
        m_new = jnp.maximum(m_prev, jnp.max(st, axis=0, keepdims=True))
        alpha = jnp.exp(m_prev - m_new)
        p = jnp.exp(st - m_new)
        l_sc[...] = alpha * l_sc[...] + jnp.sum(p, axis=0, keepdims=True)
        pv = _dot(vt_ref[0, 0, :, pl.ds(k0, tk)], p.astype(BF16))
        acc_sc[...] = acc_sc[...] * alpha + pv
        m_sc[...] = m_new

    n_diag = tq // tk
    n_full = i * n_diag

    def body(j, carry):
        step(j, None)
        return carry

    lax.fori_loop(0, n_full, body, 0)
    q_chunk = lax.broadcasted_iota(jnp.int32, (tk, tq), 1) // CHUNK
    k_chunk = lax.broadcasted_iota(jnp.int32, (tk, tq), 0) // CHUNK
    for d in range(n_diag):
        step(n_full + d, k_chunk + d * (tk // CHUNK) <= q_chunk)
    o_ref[0] = (acc_sc[...] / l_sc[...]).T.astype(o_ref.dtype)


def _mla_flash(q, k, vt, *, tq, tk):
    B, H, S, _ = q.shape
    return pl.pallas_call(
        functools.partial(_mla_flash_kernel, tq=tq, tk=tk, scale=QK_MLA ** -0.5),
        grid=(B, H, S // tq),
        in_specs=[pl.BlockSpec((1, 1, tq, QK_PAD), lambda b, h, i: (b, h, i, 0)),
                  pl.BlockSpec((1, 1, S, QK_PAD), lambda b, h, i: (b, h, 0, 0)),
                  pl.BlockSpec((1, 1, V_MLA, S), lambda b, h, i: (b, h, 0, 0))],
        out_specs=pl.BlockSpec((1, tq, V_MLA), lambda b, h, i: (b, i, h)),
        out_shape=jax.ShapeDtypeStruct((B, S, H * V_MLA), BF16),
        scratch_shapes=[pltpu.VMEM((1, tq), F32), pltpu.VMEM((1, tq), F32),
                        pltpu.VMEM((V_MLA, tq), F32)],
        compiler_params=_params("arbitrary", "arbitrary", "arbitrary"),
        name="mla_flash",
    )(q, k, vt)


def _softmax_pv(parts, o_ref):
    m = None
    for s, _ in parts:
        mx = jnp.max(s, axis=-1, keepdims=True)
        m = mx if m is None else jnp.maximum(m, mx)
    l = None
    acc = None
    for s, v in parts:
        p = jnp.exp(s - m)
        ps = jnp.sum(p, axis=-1, keepdims=True)
        pv = _dot(p.astype(BF16), v)
        l = ps if l is None else l + ps
        acc = pv if acc is None else acc + pv
    o_ref[0] = (acc / l).astype(o_ref.dtype)


def _band_kernel(q_ref, kc_ref, kp_ref, vc_ref, vp_ref, bc_ref, bp_ref, o_ref, *, scale):
    g = pl.program_id(2)
    q = q_ref[0]
    s_cur = _dot_nt(q, kc_ref[0]) * scale + bc_ref[0]

    @pl.when(g == 0)
    def _():
        _softmax_pv([(s_cur, vc_ref[0])], o_ref)

    @pl.when(g > 0)
    def _():
        s_prev = _dot_nt(q, kp_ref[0]) * scale + bp_ref[0]
        _softmax_pv([(s_prev, vp_ref[0]), (s_cur, vc_ref[0])], o_ref)


def _band(qb, kb, vb, bias_cur, bias_prev):
    B, S, hw = qb.shape
    H = hw // D_BAND
    tg = BAND_PAST
    cur = lambda b, h, g: (b, g, h)
    prev = lambda b, h, g: (b, jnp.maximum(g - 1, 0), h)
    tab = lambda b, h, g: (h, 0, 0)
    blk = (1, tg, D_BAND)
    return pl.pallas_call(
        functools.partial(_band_kernel, scale=D_BAND ** -0.5),
        grid=(B, H, S // tg),
        in_specs=[pl.BlockSpec(blk, cur), pl.BlockSpec(blk, cur), pl.BlockSpec(blk, prev),
                  pl.BlockSpec(blk, cur), pl.BlockSpec(blk, prev),
                  pl.BlockSpec((1, tg, tg), tab), pl.BlockSpec((1, tg, tg), tab)],
        out_specs=pl.BlockSpec(blk, cur),
        out_shape=jax.ShapeDtypeStruct((B, S, hw), BF16),
        compiler_params=_params("arbitrary", "arbitrary", "arbitrary"),
        name="band",
    )(qb, kb, kb, vb, vb, bias_cur, bias_prev)


def _dense_kernel(q_ref, k_ref, v_ref, b_ref, o_ref, *, scale, v_transposed):
    q, k, v = (r[(0,) * (len(r.shape) - 2)] for r in (q_ref, k_ref, v_ref))
    s = _dot_nt(q, k) * scale + b_ref[0]
    m = jnp.max(s, axis=-1, keepdims=True)
    p = jnp.exp(s - m)
    l = jnp.sum(p, axis=-1, keepdims=True)
    pb = p.astype(BF16)
    pv = _dot_nt(pb, v) if v_transposed else _dot(pb, v)
    o_ref[0] = (pv / l).astype(o_ref.dtype)


def _dense_attn(q, k, v, bias, *, q_spec, k_spec, v_spec, n_batch, n_heads, t, dv, scale,
                v_transposed, name):
    kp = bias.shape[-1]
    b_map = (lambda b, h: (h, 0, 0)) if bias.shape[0] > 1 else (lambda b, h: (0, 0, 0))
    return pl.pallas_call(
        functools.partial(_dense_kernel, scale=scale, v_transposed=v_transposed),
        grid=(n_batch, n_heads),
        in_specs=[q_spec, k_spec, v_spec, pl.BlockSpec((1, t, kp), b_map)],
        out_specs=pl.BlockSpec((1, t, dv), lambda b, h: (b, 0, h)),
        out_shape=jax.ShapeDtypeStruct((n_batch, t, n_heads * dv), BF16),
        compiler_params=_params("arbitrary", "arbitrary"),
        name=name,
    )(q, k, v, bias)


def _merge_kernel(x_ref, oa_ref, ob_ref, woa_ref, wob_ref, h_ref):
    h_ref[0] = x_ref[0] + _dot(oa_ref[0], woa_ref[...]) + _dot(ob_ref[0], wob_ref[...])


def _merge(x, oa, ob, w_oa, w_ob, *, tm):
    B, S, D = x.shape
    row = lambda b, t: (b, t, 0)
    const = lambda b, t: (0, 0)
    return pl.pallas_call(
        _merge_kernel,
        grid=(B, S // tm),
        in_specs=[pl.BlockSpec((1, tm, D), row), pl.BlockSpec((1, tm, oa.shape[-1]), row),
                  pl.BlockSpec((1, tm, ob.shape[-1]), row),
                  pl.BlockSpec(w_oa.shape, const), pl.BlockSpec(w_ob.shape, const)],
        out_specs=pl.BlockSpec((1, tm, D), row),
        out_shape=jax.ShapeDtypeStruct((B, S, D), F32),
        compiler_params=_params("arbitrary", "arbitrary"),
        name="merge",
    )(x, oa, ob, w_oa, w_ob)


def _ffn_kernel(h_ref, g_ref, wup_ref, wdn_ref, y_ref, hn_sc, acc_sc):
    j = pl.program_id(2)

    @pl.when(j == 0)
    def _():
        h = h_ref[0]
        hn_sc[...] = _rms(h, g_ref[...]).astype(BF16)
        acc_sc[...] = h

    u = jnp.maximum(_dot(hn_sc[...], wup_ref[...]), 0.0)
    acc_sc[...] += _dot((u * u).astype(BF16), wdn_ref[...])

    @pl.when(j == pl.num_programs(2) - 1)
    def _():
        y_ref[0] = acc_sc[...]


def _ffn(h, g, w_up, w_down, *, tm, tf):
    B, S, D = h.shape
    F = w_up.shape[1]
    row = lambda b, t, j: (b, t, 0)
    return pl.pallas_call(
        _ffn_kernel,
        grid=(B, S // tm, F // tf),
        in_specs=[pl.BlockSpec((1, tm, D), row), pl.BlockSpec(g.shape, lambda b, t, j: (0, 0)),
                  pl.BlockSpec((D, tf), lambda b, t, j: (0, j)),
                  pl.BlockSpec((tf, D), lambda b, t, j: (j, 0))],
        out_specs=pl.BlockSpec((1, tm, D), row),
        out_shape=jax.ShapeDtypeStruct((B, S, D), F32),
        scratch_shapes=[pltpu.VMEM((tm, D), BF16), pltpu.VMEM((tm, D), F32)],
        compiler_params=_params("arbitrary", "arbitrary", "arbitrary"),
        name="ffn",
    )(h, g, w_up, w_down)


def _rope_tables(pos):
    inv = 1.0 / (ROPE_BASE ** (jnp.arange(0, ROPE, 2, dtype=F32) / ROPE))
    ang = pos.astype(F32)[:, None] * inv[None, :]
    c, s = jnp.cos(ang), jnp.sin(ang)
    z = jnp.zeros((pos.shape[0], LANES - ROPE), F32)
    return jnp.concatenate([c, c, z], axis=1), jnp.concatenate([-s, s, z], axis=1)


def _pad_lanes(a, width=LANES):
    return jnp.pad(a, [(0, 0)] * (a.ndim - 1) + [(0, width - a.shape[-1])])


def _swap_halves(a):
    return jnp.concatenate([a[..., HALF_ROPE:], a[..., :HALF_ROPE]], axis=-1)


def _prep_weights(w_in, w_uq, w_uk, w_uv, w_o, w_up, w_down, g_qa, g_ka):
    hb3 = (w_in.shape[1] - Q_LORA - KV_LORA - ROPE) // 3
    o = [0, Q_LORA, Q_LORA + KV_LORA, Q_LORA + KV_LORA + ROPE]
    o += [o[3] + hb3, o[3] + 2 * hb3, o[3] + 3 * hb3]
    wb = w_in.astype(BF16)
    w = {"cq": wb[:, o[0]:o[1]], "ckv": wb[:, o[1]:o[2]], "kpe": _pad_lanes(wb[:, o[2]:o[3]]),
         "qb": wb[:, o[3]:o[4]], "kb": wb[:, o[4]:o[5]], "vb": wb[:, o[5]:o[6]]}
    ha = w_uq.shape[1] // QK_MLA
    uq = w_uq.astype(BF16).reshape(Q_LORA, ha, QK_MLA)
    nope = uq[:, :, :NOPE].reshape(Q_LORA, ha * NOPE)
    rope = uq[:, :, NOPE:]
    w["uq"] = jnp.concatenate(
        [nope, _pad_lanes(rope).reshape(Q_LORA, ha * LANES),
         _pad_lanes(_swap_halves(rope)).reshape(Q_LORA, ha * LANES)], axis=1)
    w["uk"] = w_uk.astype(BF16)
    w["uvt"] = w_uv.astype(BF16).T
    half = ha * V_MLA
    ob = w_o.astype(BF16)
    w["oa"], w["ob"] = ob[:half], ob[half:]
    w["up"], w["down"] = w_up.astype(BF16), w_down.astype(BF16)
    g = {"qa_n": g_qa[None, :NOPE], "qa_r": _pad_lanes(g_qa[None, NOPE:]),
         "qa_rs": _pad_lanes(_swap_halves(g_qa[None, NOPE:])),
         "ka_n": g_ka[None, :NOPE], "ka_r": _pad_lanes(g_ka[None, NOPE:])}
    return w, g


def _band_tables(rel_bias):
    t = BAND_PAST
    qi = jnp.arange(t)[:, None]
    kj = jnp.arange(t)[None, :]
    gather = lambda rel: rel_bias[:, jnp.clip(rel, -MAX_REL, MAX_REL) + MAX_REL]
    cur = jnp.where((kj // CHUNK <= qi // CHUNK)[None], gather(qi - kj), NEG)
    prev = jnp.where((kj // CHUNK >= qi // CHUNK)[None], gather(qi - kj + t), NEG)
    return cur.astype(F32), prev.astype(F32)


def _row_tile(n, pref):
    return pref if n % pref == 0 else n


def _layer_common(x, w, g, norm_mix, g_cq, g_ckv, g_qb, g_kb, keep):
    S = x.shape[1]
    return _proj(x, norm_mix, w, g_cq, g_ckv, g_qb, g_kb, keep=keep, tm=_row_tile(S, 256))


def kernel(x_prompt, x_sample, cache_mla_ckv, cache_mla_kpe, cache_band_k, cache_band_v,
           norm_mix, w_in, g_cq, w_uq, g_ckv, w_uk, w_uv, g_qa, g_ka, g_qb, g_kb, rel_bias,
           w_o, norm_ffn, w_up, w_down):
    depth = w_in.shape[0]
    assert depth == 1, "single-layer step"
    B, S, D = x_prompt.shape
    Bd, T, _ = x_sample.shape
    P = cache_mla_ckv.shape[2]
    Lb = cache_band_k.shape[2]
    keep_p = min(BAND_PAST, S)
    assert S % BAND_PAST == 0 and T <= CHUNK

    w, g = _prep_weights(w_in[0], w_uq[0], w_uk[0], w_uv[0], w_o[0], w_up[0], w_down[0],
                         g_qa[0], g_ka[0])
    ha = w["uk"].shape[1] // NOPE
    hb = rel_bias.shape[1]
    hw = hb * D_BAND
    gains = (g_cq, g_ckv, g_qb, g_kb)
    nm, nf = norm_mix, norm_ffn

    cqn, ckv, kpe_pad, qb, kb, vb, kb_tail, vb_tail = _layer_common(x_prompt, w, g, nm, *gains, keep_p)
    cos_p, sin_p = _rope_tables(jnp.arange(S, dtype=jnp.int32))
    tmq = _row_tile(S, 512)
    q = _q_up(cqn, w["uq"], g["qa_n"], g["qa_r"], g["qa_rs"], cos_p, sin_p, tm=tmq)
    k, vt = _kv_up(ckv, kpe_pad, w["uk"], w["uvt"], g["ka_n"], g["ka_r"], cos_p, sin_p, tm=tmq)
    oa = _mla_flash(q, k, vt, tq=tmq, tk=tmq)
    bias_cur, bias_prev = _band_tables(rel_bias[0])
    ob = _band(qb, kb, vb, bias_cur, bias_prev)
    h = _merge(x_prompt, oa, ob, w["oa"], w["ob"], tm=_row_tile(S, 512))
    y_prompt = _ffn(h, nf, w["up"], w["down"], tm=_row_tile(S, 512), tf=1024)

    n_s = Bd * T
    xs = x_sample.reshape(1, n_s, D)
    cqn_s, ckv_s, kpe_pad_s, qb_s, kb_s, vb_s, kb_s32, vb_s32 = _layer_common(xs, w, g, nm, *gains, n_s)
    pos_s = P + jnp.arange(T, dtype=jnp.int32)
    cos_s, sin_s = _rope_tables(pos_s)
    q_s = _q_up(cqn_s.reshape(Bd, T, Q_LORA), w["uq"], g["qa_n"], g["qa_r"], g["qa_rs"],
                cos_s, sin_s, tm=T)
    n_hist = P + T
    n_hist_pad = -(-n_hist // LANES) * LANES
    padr = n_hist_pad - n_hist
    ckv_all = jnp.concatenate([cache_mla_ckv[0], ckv_s.reshape(Bd, T, KV_LORA),
                               jnp.zeros((Bd, padr, KV_LORA), F32)], axis=1)
    kpe_all = jnp.concatenate([_pad_lanes(cache_mla_kpe[0]), kpe_pad_s.reshape(Bd, T, LANES),
                               jnp.zeros((Bd, padr, LANES), F32)], axis=1)
    cos_h, sin_h = _rope_tables(jnp.arange(n_hist_pad, dtype=jnp.int32))
    k_s, vt_s = _kv_up(ckv_all, kpe_all, w["uk"], w["uvt"], g["ka_n"], g["ka_r"], cos_h, sin_h,
                       tm=n_hist_pad)
    hist_mask = jnp.where(jnp.arange(n_hist_pad) < n_hist, 0.0, NEG).astype(F32)
    hist_mask = jnp.broadcast_to(hist_mask[None, None, :], (1, T, n_hist_pad))
    oa_s = _dense_attn(
        q_s, k_s, vt_s, hist_mask,
        q_spec=pl.BlockSpec((1, 1, T, QK_PAD), lambda b, h: (b, h, 0, 0)),
        k_spec=pl.BlockSpec((1, 1, n_hist_pad, QK_PAD), lambda b, h: (b, h, 0, 0)),
        v_spec=pl.BlockSpec((1, 1, V_MLA, n_hist_pad), lambda b, h: (b, h, 0, 0)),
        n_batch=Bd, n_heads=ha, t=T, dv=V_MLA, scale=QK_MLA ** -0.5, v_transposed=True,
        name="mla_sample")
    n_band = Lb + T
    n_band_pad = -(-n_band // LANES) * LANES
    zb = jnp.zeros((Bd, n_band_pad - n_band, hw), BF16)
    kb_all = jnp.concatenate([cache_band_k[0].reshape(Bd, Lb, hw).astype(BF16),
                              kb_s.reshape(Bd, T, hw), zb], axis=1)
    vb_all = jnp.concatenate([cache_band_v[0].reshape(Bd, Lb, hw).astype(BF16),
                              vb_s.reshape(Bd, T, hw), zb], axis=1)
    band_kpos = jnp.concatenate([jnp.arange(P - Lb, P, dtype=jnp.int32), pos_s])
    band_rel = jnp.clip(pos_s[:, None] - band_kpos[None, :], -MAX_REL, MAX_REL) + MAX_REL
    bias_s = jnp.concatenate([rel_bias[0][:, band_rel].astype(F32),
                              jnp.full((hb, T, n_band_pad - n_band), NEG, F32)], axis=-1)
    ob_s = _dense_attn(
        qb_s.reshape(Bd, T, hw), kb_all, vb_all, bias_s,
        q_spec=pl.BlockSpec((1, T, D_BAND), lambda b, h: (b, 0, h)),
        k_spec=pl.BlockSpec((1, n_band_pad, D_BAND), lambda b, h: (b, 0, h)),
        v_spec=pl.BlockSpec((1, n_band_pad, D_BAND), lambda b, h: (b, 0, h)),
        n_batch=Bd, n_heads=hb, t=T, dv=D_BAND, scale=D_BAND ** -0.5, v_transposed=False,
        name="band_sample")
    h_s = _merge(xs, oa_s.reshape(1, n_s, ha * V_MLA), ob_s.reshape(1, n_s, hw), w["oa"], w["ob"],
                 tm=n_s)
    y_sample = _ffn(h_s, nf, w["up"], w["down"], tm=n_s, tf=1024).reshape(Bd, T, D)

    return (y_prompt, y_sample,
            ckv[None], kpe_pad[None, ..., :ROPE],
            kb_tail.reshape(B, keep_p, hb, D_BAND)[None], vb_tail.reshape(B, keep_p, hb, D_BAND)[None],
            ckv_s.reshape(Bd, T, KV_LORA)[None], kpe_pad_s.reshape(Bd, T, LANES)[None, ..., :ROPE],
            kb_s32.reshape(Bd, T, hb, D_BAND)[None], vb_s32.reshape(Bd, T, hb, D_BAND)[None])
```

```python
import functools

import jax
import jax.numpy as jnp
import numpy as np
from jax import lax
from jax.experimental import pallas as pl
from jax.experimental.pallas import tpu as pltpu

CHUNK = 64
EPS = 1e-6
NOPE = 128
ROPE = 64
HALF_ROPE = ROPE // 2
QK_MLA = NOPE + ROPE
V_MLA = 128
D_BAND = 128
BAND_CHUNKS = 8
BAND_PAST = BAND_CHUNKS * CHUNK
MAX_REL = 128
ROPE_BASE = 10000.0
Q_LORA = 512
KV_LORA = 256
NEG = -1e30
LOG2E = 1.4426950408889634

LANES = 128
QK_PAD = 2 * LANES
VMEM_LIMIT = 56 * 1024 * 1024

BF16 = jnp.bfloat16
F32 = jnp.float32

NT_DIMS = (((1,), (1,)), ((), ()))


def _params(*sem):
    return pltpu.CompilerParams(dimension_semantics=sem, vmem_limit_bytes=VMEM_LIMIT)


def _rms(x, g):
    ms = jnp.mean(x * x, axis=-1, keepdims=True)
    return x * lax.rsqrt(ms + EPS) * g


def _dot(a, b):
    return jnp.dot(a, b, preferred_element_type=F32)


def _dot_nt(a, b):
    return lax.dot_general(a, b, NT_DIMS, preferred_element_type=F32)


def _proj_kernel(x_ref, nm_ref, wcq_ref, wckv_ref, wkpe_ref, wqb_ref, wkb_ref, wvb_ref,
                 gcq_ref, gckv_ref, gqb_ref, gkb_ref,
                 cqn_ref, ckv_ref, kpe_ref, qb_ref, kb_ref, vb_ref, kbt_ref, vbt_ref,
                 *, n_heads, tail_start):
    t = pl.program_id(1)
    xn = _rms(x_ref[0], nm_ref[...]).astype(BF16)
    cqn_ref[0] = _rms(_dot(xn, wcq_ref[...]), gcq_ref[...]).astype(BF16)
    ckv_ref[0] = _rms(_dot(xn, wckv_ref[...]), gckv_ref[...])
    kpe_ref[0] = _dot(xn, wkpe_ref[...])
    qb = _dot(xn, wqb_ref[...])
    kb = _dot(xn, wkb_ref[...])
    vb = _dot(xn, wvb_ref[...])
    vb_ref[0] = vb.astype(BF16)
    kbn = []
    for h in range(n_heads):
        sl = slice(h * D_BAND, (h + 1) * D_BAND)
        qb_ref[0, :, sl] = _rms(qb[:, sl], gqb_ref[...]).astype(BF16)
        kbn.append(_rms(kb[:, sl], gkb_ref[...]))
    kbn = jnp.concatenate(kbn, axis=-1)
    kb_ref[0] = kbn.astype(BF16)

    @pl.when(t >= tail_start)
    def _():
        kbt_ref[0] = kbn
        vbt_ref[0] = vb


def _proj(x, norm_mix, w, g_cq, g_ckv, g_qb, g_kb, *, keep, tm):
    B, S, D = x.shape
    n_heads = w["qb"].shape[1] // D_BAND
    hw = n_heads * D_BAND
    nt = S // tm
    tail_start = nt - keep // tm
    row = lambda b, t: (b, t, 0)
    const = lambda b, t: (0, 0)
    tail = lambda b, t: (b, jnp.maximum(t - tail_start, 0), 0)
    wspec = lambda a: pl.BlockSpec(a.shape, const)
    out_shapes = (
        jax.ShapeDtypeStruct((B, S, Q_LORA), BF16),
        jax.ShapeDtypeStruct((B, S, KV_LORA), F32),
        jax.ShapeDtypeStruct((B, S, LANES), F32),
        jax.ShapeDtypeStruct((B, S, hw), BF16),
        jax.ShapeDtypeStruct((B, S, hw), BF16),
        jax.ShapeDtypeStruct((B, S, hw), BF16),
        jax.ShapeDtypeStruct((B, keep, hw), F32),
        jax.ShapeDtypeStruct((B, keep, hw), F32),
    )
    out_specs = (
        pl.BlockSpec((1, tm, Q_LORA), row),
        pl.BlockSpec((1, tm, KV_LORA), row),
        pl.BlockSpec((1, tm, LANES), row),
        pl.BlockSpec((1, tm, hw), row),
        pl.BlockSpec((1, tm, hw), row),
        pl.BlockSpec((1, tm, hw), row),
        pl.BlockSpec((1, tm, hw), tail),
        pl.BlockSpec((1, tm, hw), tail),
    )
    weights = (w["cq"], w["ckv"], w["kpe"], w["qb"], w["kb"], w["vb"])
    gains = (g_cq, g_ckv, g_qb, g_kb)
    return pl.pallas_call(
        functools.partial(_proj_kernel, n_heads=n_heads, tail_start=tail_start),
        grid=(B, nt),
        in_specs=[pl.BlockSpec((1, tm, D), row), wspec(norm_mix)]
        + [wspec(a) for a in weights] + [wspec(a) for a in gains],
        out_specs=out_specs,
        out_shape=out_shapes,
        compiler_params=_params("arbitrary", "arbitrary"),
        name="proj",
    )(x, norm_mix, *weights, *gains)


def _q_up_kernel(cqn_ref, w_ref, gn_ref, gr_ref, grs_ref, cos_ref, sin_ref, q_ref, *, n_heads):
    hw = n_heads * LANES
    y = _dot(cqn_ref[0], w_ref[...])
    cos = cos_ref[...]
    sin = sin_ref[...]
    for h in range(n_heads):
        sl = slice(h * LANES, (h + 1) * LANES)
        n = y[:, sl]
        r = y[:, hw + h * LANES: hw + (h + 1) * LANES]
        rs = y[:, 2 * hw + h * LANES: 2 * hw + (h + 1) * LANES]
        ss = jnp.sum(n * n, axis=-1, keepdims=True) + jnp.sum(r * r, axis=-1, keepdims=True)
        rinv = lax.rsqrt(ss * (1.0 / QK_MLA) + EPS)
        q_ref[0, h, :, 0:LANES] = (n * rinv * gn_ref[...]).astype(BF16)
        rot = (r * rinv * gr_ref[...]) * cos + (rs * rinv * grs_ref[...]) * sin
        q_ref[0, h, :, LANES:QK_PAD] = rot.astype(BF16)


def _q_up(cqn, w, g_n, g_r, g_rs, cos, sin, *, tm):
    B, S, _ = cqn.shape
    n_heads = w.shape[1] // (3 * LANES)
    row = lambda b, t: (b, t, 0)
    const = lambda b, t: (0, 0)
    pos = lambda b, t: (t, 0)
    return pl.pallas_call(
        functools.partial(_q_up_kernel, n_heads=n_heads),
        grid=(B, S // tm),
        in_specs=[pl.BlockSpec((1, tm, Q_LORA), row), pl.BlockSpec(w.shape, const),
                  pl.BlockSpec(g_n.shape, const), pl.BlockSpec(g_r.shape, const),
                  pl.BlockSpec(g_rs.shape, const),
                  pl.BlockSpec((tm, LANES), pos), pl.BlockSpec((tm, LANES), pos)],
        out_specs=pl.BlockSpec((1, n_heads, tm, QK_PAD), lambda b, t: (b, 0, t, 0)),
        out_shape=jax.ShapeDtypeStruct((B, n_heads, S, QK_PAD), BF16),
        compiler_params=_params("arbitrary", "arbitrary"),
        name="q_up",
    )(cqn, w, g_n, g_r, g_rs, cos, sin)


def _kv_up_kernel(ckv_ref, kpe_ref, wuk_ref, wuvt_ref, gn_ref, gr_ref, cos_ref, sin_ref,
                  k_ref, vt_ref, *, n_heads):
    c = ckv_ref[0].astype(BF16)
    kn = _dot(c, wuk_ref[...])
    vt = _dot_nt(wuvt_ref[...], c)
    a = kpe_ref[0] * gr_ref[...]
    sspe = jnp.sum(kpe_ref[0] * kpe_ref[0], axis=-1, keepdims=True)
    a_sw = pltpu.roll(a, HALF_ROPE, 1) + pltpu.roll(a, LANES - HALF_ROPE, 1)
    rot = a * cos_ref[...] + a_sw * sin_ref[...]
    for h in range(n_heads):
        n = kn[:, h * NOPE:(h + 1) * NOPE]
        ss = jnp.sum(n * n, axis=-1, keepdims=True) + sspe
        rinv = lax.rsqrt(ss * (1.0 / QK_MLA) + EPS)
        k_ref[0, h, :, 0:LANES] = (n * rinv * gn_ref[...]).astype(BF16)
        k_ref[0, h, :, LANES:QK_PAD] = (rot * rinv).astype(BF16)
        vt_ref[0, h] = vt[h * V_MLA:(h + 1) * V_MLA, :].astype(BF16)


def _kv_up(ckv, kpe_pad, w_uk, w_uvt, g_n, g_r, cos, sin, *, tm):
    B, S, _ = ckv.shape
    n_heads = w_uk.shape[1] // NOPE
    row = lambda b, t: (b, t, 0)
    const = lambda b, t: (0, 0)
    pos = lambda b, t: (t, 0)
    return pl.pallas_call(
        functools.partial(_kv_up_kernel, n_heads=n_heads),
        grid=(B, S // tm),
        in_specs=[pl.BlockSpec((1, tm, KV_LORA), row), pl.BlockSpec((1, tm, LANES), row),
                  pl.BlockSpec(w_uk.shape, const), pl.BlockSpec(w_uvt.shape, const),
                  pl.BlockSpec(g_n.shape, const), pl.BlockSpec(g_r.shape, const),
                  pl.BlockSpec((tm, LANES), pos), pl.BlockSpec((tm, LANES), pos)],
        out_specs=(pl.BlockSpec((1, n_heads, tm, QK_PAD), lambda b, t: (b, 0, t, 0)),
                   pl.BlockSpec((1, n_heads, V_MLA, tm), lambda b, t: (b, 0, 0, t))),
        out_shape=(jax.ShapeDtypeStruct((B, n_heads, S, QK_PAD), BF16),
                   jax.ShapeDtypeStruct((B, n_heads, V_MLA, S), BF16)),
        compiler_params=_params("arbitrary", "arbitrary"),
        name="kv_up",
    )(ckv, kpe_pad, w_uk, w_uvt, g_n, g_r, cos, sin)


def _mla_flash_kernel(q_ref, k_ref, vt_ref, o_ref, m_sc, l_sc, acc_sc, *, tq, tk, hg, scale):
    i = pl.program_id(2)
    c = scale * LOG2E
    m_sc[...] = jnp.full(m_sc.shape, -jnp.inf, F32)
    l_sc[...] = jnp.zeros(l_sc.shape, F32)
    acc_sc[...] = jnp.zeros(acc_sc.shape, F32)

    def step(j, mask):
        k0 = pl.multiple_of(j * tk, tk)
        sts = [_dot_nt(k_ref[0, hh, pl.ds(k0, tk), :], q_ref[0, hh]) for hh in range(hg)]
        for hh, st in enumerate(sts):
            if mask is not None:
                st = jnp.where(mask, st, NEG)
            m_prev = m_sc[hh]
            m_new = jnp.maximum(m_prev, jnp.max(st, axis=0, keepdims=True))
            alpha = jnp.exp2((m_prev - m_new) * c)
            p = jnp.exp2((st - m_new) * c)
            l_sc[hh] = alpha * l_sc[hh] + jnp.sum(p, axis=0, keepdims=True)
            pv = _dot(vt_ref[0, hh, :, pl.ds(k0, tk)], p.astype(BF16))
            acc_sc[hh] = acc_sc[hh] * alpha + pv
            m_sc[hh] = m_new

    n_diag = tq // tk
    n_full = i * n_diag

    def body(j, carry):
        step(j, None)
        return carry

    lax.fori_loop(0, n_full, body, 0)
    q_chunk = lax.broadcasted_iota(jnp.int32, (tk, tq), 1) // CHUNK
    k_chunk = lax.broadcasted_iota(jnp.int32, (tk, tq), 0) // CHUNK
    for d in range(n_diag):
        step(n_full + d, k_chunk + d * (tk // CHUNK) <= q_chunk)
    for hh in range(hg):
        o_ref[0, :, hh * V_MLA:(hh + 1) * V_MLA] = (acc_sc[hh] / l_sc[hh]).T.astype(o_ref.dtype)


def _mla_flash(q, k, vt, *, tq, tk, hg):
    B, H, S, _ = q.shape
    return pl.pallas_call(
        functools.partial(_mla_flash_kernel, tq=tq, tk=tk, hg=hg, scale=QK_MLA ** -0.5),
        grid=(B, H // hg, S // tq),
        in_specs=[pl.BlockSpec((1, hg, tq, QK_PAD), lambda b, h, i: (b, h, i, 0)),
                  pl.BlockSpec((1, hg, S, QK_PAD), lambda b, h, i: (b, h, 0, 0)),
                  pl.BlockSpec((1, hg, V_MLA, S), lambda b, h, i: (b, h, 0, 0))],
        out_specs=pl.BlockSpec((1, tq, hg * V_MLA), lambda b, h, i: (b, i, h)),
        out_shape=jax.ShapeDtypeStruct((B, S, H * V_MLA), BF16),
        scratch_shapes=[pltpu.VMEM((hg, 1, tq), F32), pltpu.VMEM((hg, 1, tq), F32),
                        pltpu.VMEM((hg, V_MLA, tq), F32)],
        compiler_params=_params("arbitrary", "arbitrary", "arbitrary"),
        name="mla_flash",
    )(q, k, vt)


def _softmax_pv(parts, o_ref):
    m = None
    for s, _ in parts:
        mx = jnp.max(s, axis=-1, keepdims=True)
        m = mx if m is None else jnp.maximum(m, mx)
    l = None
    acc = None
    for s, v in parts:
        p = jnp.exp(s - m)
        ps = jnp.sum(p, axis=-1, keepdims=True)
        pv = _dot(p.astype(BF16), v)
        l = ps if l is None else l + ps
        acc = pv if acc is None else acc + pv
    o_ref[0] = (acc / l).astype(o_ref.dtype)


def _band_kernel(q_ref, kc_ref, kp_ref, vc_ref, vp_ref, u_ref, o_ref, bc_sc, bp_sc, *, scale):
    g = pl.program_id(2)
    tg = bc_sc.shape[0]

    @pl.when(g == 0)
    def _():
        qc = lax.broadcasted_iota(jnp.int32, (tg, tg), 0) // CHUNK
        kc = lax.broadcasted_iota(jnp.int32, (tg, tg), 1) // CHUNK
        for row, sc, vis in ((0, bc_sc, kc <= qc), (1, bp_sc, kc >= qc)):
            u = jnp.broadcast_to(u_ref[0, row:row + 1, :], (tg, u_ref.shape[-1]))
            t = pltpu.roll(u, 0, 1, stride=1, stride_axis=0)
            sc[...] = jnp.where(vis, t[:, :tg], NEG)

    q = q_ref[0]
    s_cur = _dot_nt(q, kc_ref[0]) * scale + bc_sc[...]

    @pl.when(g == 0)
    def _():
        _softmax_pv([(s_cur, vc_ref[0])], o_ref)

    @pl.when(g > 0)
    def _():
        s_prev = _dot_nt(q, kp_ref[0]) * scale + bp_sc[...]
        _softmax_pv([(s_prev, vp_ref[0]), (s_cur, vc_ref[0])], o_ref)


def _band(qb, kb, vb, u_tab):
    B, S, hw = qb.shape
    H = hw // D_BAND
    tg = BAND_PAST
    cur = lambda b, h, g: (b, g, h)
    prev = lambda b, h, g: (b, jnp.maximum(g - 1, 0), h)
    blk = (1, tg, D_BAND)
    return pl.pallas_call(
        functools.partial(_band_kernel, scale=D_BAND ** -0.5),
        grid=(B, H, S // tg),
        in_specs=[pl.BlockSpec(blk, cur), pl.BlockSpec(blk, cur), pl.BlockSpec(blk, prev),
                  pl.BlockSpec(blk, cur), pl.BlockSpec(blk, prev),
                  pl.BlockSpec((1,) + u_tab.shape[1:], lambda b, h, g: (h, 0, 0))],
        out_specs=pl.BlockSpec(blk, cur),
        out_shape=jax.ShapeDtypeStruct((B, S, hw), BF16),
        scratch_shapes=[pltpu.VMEM((tg, tg), F32), pltpu.VMEM((tg, tg), F32)],
        compiler_params=_params("arbitrary", "arbitrary", "arbitrary"),
        name="band",
    )(qb, kb, kb, vb, vb, u_tab)


def _dense_kernel(q_ref, k_ref, v_ref, b_ref, o_ref, *, scale, v_transposed):
    q, k, v = (r[(0,) * (len(r.shape) - 2)] for r in (q_ref, k_ref, v_ref))
    s = _dot_nt(q, k) * scale + b_ref[0]
    m = jnp.max(s, axis=-1, keepdims=True)
    p = jnp.exp(s - m)
    l = jnp.sum(p, axis=-1, keepdims=True)
    pb = p.astype(BF16)
    pv = _dot_nt(pb, v) if v_transposed else _dot(pb, v)
    o_ref[0] = (pv / l).astype(o_ref.dtype)


def _dense_attn(q, k, v, bias, *, q_spec, k_spec, v_spec, n_batch, n_heads, t, dv, scale,
                v_transposed, name):
    kp = bias.shape[-1]
    b_map = (lambda b, h: (h, 0, 0)) if bias.shape[0] > 1 else (lambda b, h: (0, 0, 0))
    return pl.pallas_call(
        functools.partial(_dense_kernel, scale=scale, v_transposed=v_transposed),
        grid=(n_batch, n_heads),
        in_specs=[q_spec, k_spec, v_spec, pl.BlockSpec((1, t, kp), b_map)],
        out_specs=pl.BlockSpec((1, t, dv), lambda b, h: (b, 0, h)),
        out_shape=jax.ShapeDtypeStruct((n_batch, t, n_heads * dv), BF16),
        compiler_params=_params("arbitrary", "arbitrary"),
        name=name,
    )(q, k, v, bias)


def _merge_kernel(x_ref, oa_ref, ob_ref, woa_ref, wob_ref, h_ref):
    h_ref[0] = x_ref[0] + _dot(oa_ref[0], woa_ref[...]) + _dot(ob_ref[0], wob_ref[...])


def _merge(x, oa, ob, w_oa, w_ob, *, tm):
    B, S, D = x.shape
    row = lambda b, t: (b, t, 0)
    const = lambda b, t: (0, 0)
    return pl.pallas_call(
        _merge_kernel,
        grid=(B, S // tm),
        in_specs=[pl.BlockSpec((1, tm, D), row), pl.BlockSpec((1, tm, oa.shape[-1]), row),
                  pl.BlockSpec((1, tm, ob.shape[-1]), row),
                  pl.BlockSpec(w_oa.shape, const), pl.BlockSpec(w_ob.shape, const)],
        out_specs=pl.BlockSpec((1, tm, D), row),
        out_shape=jax.ShapeDtypeStruct((B, S, D), F32),
        compiler_params=_params("arbitrary", "arbitrary"),
        name="merge",
    )(x, oa, ob, w_oa, w_ob)


def _ffn_kernel(h_ref, g_ref, wup_ref, wdn_ref, y_ref, hn_sc, acc_sc):
    j = pl.program_id(2)

    @pl.when(j == 0)
    def _():
        h = h_ref[0]
        hn_sc[...] = _rms(h, g_ref[...]).astype(BF16)
        acc_sc[...] = h

    u = jnp.maximum(_dot(hn_sc[...], wup_ref[...]), 0.0)
    acc_sc[...] += _dot((u * u).astype(BF16), wdn_ref[...])

    @pl.when(j == pl.num_programs(2) - 1)
    def _():
        y_ref[0] = acc_sc[...]


def _ffn(h, g, w_up, w_down, *, tm, tf):
    B, S, D = h.shape
    F = w_up.shape[1]
    row = lambda b, t, j: (b, t, 0)
    return pl.pallas_call(
        _ffn_kernel,
        grid=(B, S // tm, F // tf),
        in_specs=[pl.BlockSpec((1, tm, D), row), pl.BlockSpec(g.shape, lambda b, t, j: (0, 0)),
                  pl.BlockSpec((D, tf), lambda b, t, j: (0, j)),
                  pl.BlockSpec((tf, D), lambda b, t, j: (j, 0))],
        out_specs=pl.BlockSpec((1, tm, D), row),
        out_shape=jax.ShapeDtypeStruct((B, S, D), F32),
        scratch_shapes=[pltpu.VMEM((tm, D), BF16), pltpu.VMEM((tm, D), F32)],
        compiler_params=_params("arbitrary", "arbitrary", "arbitrary"),
        name="ffn",
    )(h, g, w_up, w_down)


def _rope_tables(pos):
    inv = 1.0 / (ROPE_BASE ** (jnp.arange(0, ROPE, 2, dtype=F32) / ROPE))
    ang = pos.astype(F32)[:, None] * inv[None, :]
    c, s = jnp.cos(ang), jnp.sin(ang)
    z = jnp.zeros((pos.shape[0], LANES - ROPE), F32)
    return jnp.concatenate([c, c, z], axis=1), jnp.concatenate([-s, s, z], axis=1)


def _pad_lanes(a, width=LANES):
    return jnp.pad(a, [(0, 0)] * (a.ndim - 1) + [(0, width - a.shape[-1])])


def _swap_halves(a):
    return jnp.concatenate([a[..., HALF_ROPE:], a[..., :HALF_ROPE]], axis=-1)


def _prep_weights(w_in, w_uq, w_uk, w_uv, w_o, w_up, w_down, g_qa, g_ka):
    hb3 = (w_in.shape[1] - Q_LORA - KV_LORA - ROPE) // 3
    o = [0, Q_LORA, Q_LORA + KV_LORA, Q_LORA + KV_LORA + ROPE]
    o += [o[3] + hb3, o[3] + 2 * hb3, o[3] + 3 * hb3]
    wb = w_in.astype(BF16)
    w = {"cq": wb[:, o[0]:o[1]], "ckv": wb[:, o[1]:o[2]], "kpe": _pad_lanes(wb[:, o[2]:o[3]]),
         "qb": wb[:, o[3]:o[4]], "kb": wb[:, o[4]:o[5]], "vb": wb[:, o[5]:o[6]]}
    ha = w_uq.shape[1] // QK_MLA
    uq = w_uq.astype(BF16).reshape(Q_LORA, ha, QK_MLA)
    nope = uq[:, :, :NOPE].reshape(Q_LORA, ha * NOPE)
    rope = uq[:, :, NOPE:]
    w["uq"] = jnp.concatenate(
        [nope, _pad_lanes(rope).reshape(Q_LORA, ha * LANES),
         _pad_lanes(_swap_halves(rope)).reshape(Q_LORA, ha * LANES)], axis=1)
    w["uk"] = w_uk.astype(BF16)
    w["uvt"] = w_uv.astype(BF16).T
    half = ha * V_MLA
    ob = w_o.astype(BF16)
    w["oa"], w["ob"] = ob[:half], ob[half:]
    w["up"], w["down"] = w_up.astype(BF16), w_down.astype(BF16)
    g = {"qa_n": g_qa[None, :NOPE], "qa_r": _pad_lanes(g_qa[None, NOPE:]),
         "qa_rs": _pad_lanes(_swap_halves(g_qa[None, NOPE:])),
         "ka_n": g_ka[None, :NOPE], "ka_r": _pad_lanes(g_ka[None, NOPE:])}
    return w, g


def _band_vectors(rel_bias):
    t = BAND_PAST
    e = np.arange(2 * t)
    e = np.where(e < t, e, e - 2 * t)
    idx = np.stack([np.clip(-e, -MAX_REL, MAX_REL), np.clip(t - e, -MAX_REL, MAX_REL)]) + MAX_REL
    return rel_bias[:, idx].astype(F32)


def _row_tile(n, pref):
    return pref if n % pref == 0 else n


def _layer_common(x, w, g, norm_mix, g_cq, g_ckv, g_qb, g_kb, keep):
    S = x.shape[1]
    return _proj(x, norm_mix, w, g_cq, g_ckv, g_qb, g_kb, keep=keep, tm=_row_tile(S, 256))


def kernel(x_prompt, x_sample, cache_mla_ckv, cache_mla_kpe, cache_band_k, cache_band_v,
           norm_mix, w_in, g_cq, w_uq, g_ckv, w_uk, w_uv, g_qa, g_ka, g_qb, g_kb, rel_bias,
           w_o, norm_ffn, w_up, w_down):
    depth = w_in.shape[0]
    assert depth == 1, "single-layer step"
    B, S, D = x_prompt.shape
    Bd, T, _ = x_sample.shape
    P = cache_mla_ckv.shape[2]
    Lb = cache_band_k.shape[2]
    keep_p = min(BAND_PAST, S)
    assert S % BAND_PAST == 0 and T <= CHUNK

    w, g = _prep_weights(w_in[0], w_uq[0], w_uk[0], w_uv[0], w_o[0], w_up[0], w_down[0],
                         g_qa[0], g_ka[0])
    ha = w["uk"].shape[1] // NOPE
    hb = rel_bias.shape[1]
    hw = hb * D_BAND
    gains = (g_cq, g_ckv, g_qb, g_kb)
    nm, nf = norm_mix, norm_ffn

    cqn, ckv, kpe_pad, qb, kb, vb, kb_tail, vb_tail = _layer_common(x_prompt, w, g, nm, *gains, keep_p)
    cos_p, sin_p = _rope_tables(jnp.arange(S, dtype=jnp.int32))
    tmq = _row_tile(S, 512)
    q = _q_up(cqn, w["uq"], g["qa_n"], g["qa_r"], g["qa_rs"], cos_p, sin_p, tm=tmq)
    k, vt = _kv_up(ckv, kpe_pad, w["uk"], w["uvt"], g["ka_n"], g["ka_r"], cos_p, sin_p, tm=tmq)
    oa = _mla_flash(q, k, vt, tq=tmq, tk=tmq, hg=2)
    ob = _band(qb, kb, vb, _band_vectors(rel_bias[0]))
    h = _merge(x_prompt, oa, ob, w["oa"], w["ob"], tm=_row_tile(S, 512))
    y_prompt = _ffn(h, nf, w["up"], w["down"], tm=_row_tile(S, 512), tf=1024)

    n_s = Bd * T
    xs = x_sample.reshape(1, n_s, D)
    cqn_s, ckv_s, kpe_pad_s, qb_s, kb_s, vb_s, kb_s32, vb_s32 = _layer_common(xs, w, g, nm, *gains, n_s)
    pos_s = P + jnp.arange(T, dtype=jnp.int32)
    cos_s, sin_s = _rope_tables(pos_s)
    q_s = _q_up(cqn_s.reshape(Bd, T, Q_LORA), w["uq"], g["qa_n"], g["qa_r"], g["qa_rs"],
                cos_s, sin_s, tm=T)
    n_hist = P + T
    n_hist_pad = -(-n_hist // LANES) * LANES
    padr = n_hist_pad - n_hist
    ckv_all = jnp.concatenate([cache_mla_ckv[0], ckv_s.reshape(Bd, T, KV_LORA),
                               jnp.zeros((Bd, padr, KV_LORA), F32)], axis=1)
    kpe_all = jnp.concatenate([_pad_lanes(cache_mla_kpe[0]), kpe_pad_s.reshape(Bd, T, LANES),
                               jnp.zeros((Bd, padr, LANES), F32)], axis=1)
    cos_h, sin_h = _rope_tables(jnp.arange(n_hist_pad, dtype=jnp.int32))
    k_s, vt_s = _kv_up(ckv_all, kpe_all, w["uk"], w["uvt"], g["ka_n"], g["ka_r"], cos_h, sin_h,
                       tm=n_hist_pad)
    hist_mask = jnp.where(jnp.arange(n_hist_pad) < n_hist, 0.0, NEG).astype(F32)
    hist_mask = jnp.broadcast_to(hist_mask[None, None, :], (1, T, n_hist_pad))
    oa_s = _dense_attn(
        q_s, k_s, vt_s, hist_mask,
        q_spec=pl.BlockSpec((1, 1, T, QK_PAD), lambda b, h: (b, h, 0, 0)),
        k_spec=pl.BlockSpec((1, 1, n_hist_pad, QK_PAD), lambda b, h: (b, h, 0, 0)),
        v_spec=pl.BlockSpec((1, 1, V_MLA, n_hist_pad), lambda b, h: (b, h, 0, 0)),
        n_batch=Bd, n_heads=ha, t=T, dv=V_MLA, scale=QK_MLA ** -0.5, v_transposed=True,
        name="mla_sample")
    n_band = Lb + T
    n_band_pad = -(-n_band // LANES) * LANES
    zb = jnp.zeros((Bd, n_band_pad - n_band, hw), BF16)
    kb_all = jnp.concatenate([cache_band_k[0].reshape(Bd, Lb, hw).astype(BF16),
                              kb_s.reshape(Bd, T, hw), zb], axis=1)
    vb_all = jnp.concatenate([cache_band_v[0].reshape(Bd, Lb, hw).astype(BF16),
                              vb_s.reshape(Bd, T, hw), zb], axis=1)
    band_kpos = jnp.concatenate([jnp.arange(P - Lb, P, dtype=jnp.int32), pos_s])
    band_rel = jnp.clip(pos_s[:, None] - band_kpos[None, :], -MAX_REL, MAX_REL) + MAX_REL
    bias_s = jnp.concatenate([rel_bias[0][:, band_rel].astype(F32),
                              jnp.full((hb, T, n_band_pad - n_band), NEG, F32)], axis=-1)
    ob_s = _dense_attn(
        qb_s.reshape(Bd, T, hw), kb_all, vb_all, bias_s,
        q_spec=pl.BlockSpec((1, T, D_BAND), lambda b, h: (b, 0, h)),
        k_spec=pl.BlockSpec((1, n_band_pad, D_BAND), lambda b, h: (b, 0, h)),
        v_spec=pl.BlockSpec((1, n_band_pad, D_BAND), lambda b, h: (b, 0, h)),
        n_batch=Bd, n_heads=hb, t=T, dv=D_BAND, scale=D_BAND ** -0.5, v_transposed=False,
        name="band_sample")
    h_s = _merge(xs, oa_s.reshape(1, n_s, ha * V_MLA), ob_s.reshape(1, n_s, hw), w["oa"], w["ob"],
                 tm=n_s)
    y_sample = _ffn(h_s, nf, w["up"], w["down"], tm=n_s, tf=1024).reshape(Bd, T, D)

    return (y_prompt, y_sample,
            ckv[None], kpe_pad[None, ..., :ROPE],
            kb_tail.reshape(B, keep_p, hb, D_BAND)[None], vb_tail.reshape(B, keep_p, hb, D_BAND)[None],
            ckv_s.reshape(Bd, T, KV_LORA)[None], kpe_pad_s.reshape(Bd, T, LANES)[None, ..., :ROPE],
            kb_s32.reshape(Bd, T, hb, D_BAND)[None], vb_s32.reshape(Bd, T, hb, D_BAND)[None])
```

```python
import functools

import jax
import jax.numpy as jnp
import numpy as np
from jax import lax
from jax.experimental import pallas as pl
from jax.experimental.pallas import tpu as pltpu

CHUNK = 64
EPS = 1e-6
NOPE = 128
ROPE = 64
HALF_ROPE = ROPE // 2
QK_MLA = NOPE + ROPE
V_MLA = 128
D_BAND = 128
BAND_CHUNKS = 8
BAND_PAST = BAND_CHUNKS * CHUNK
MAX_REL = 128
ROPE_BASE = 10000.0
Q_LORA = 512
KV_LORA = 256
NEG = -1e30
LOG2E = 1.4426950408889634

BOUND_MARGIN = 1.02
FAST_LIMIT = 60.0

LANES = 128
QK_PAD = 2 * LANES
AUG_LANE = ROPE
VMEM_LIMIT = 56 * 1024 * 1024

BF16 = jnp.bfloat16
F32 = jnp.float32

NT_DIMS = (((1,), (1,)), ((), ()))


def _params(*sem):
    return pltpu.CompilerParams(dimension_semantics=sem, vmem_limit_bytes=VMEM_LIMIT)


def _rms(x, g):
    ms = jnp.mean(x * x, axis=-1, keepdims=True)
    return x * lax.rsqrt(ms + EPS) * g


def _dot(a, b):
    return jnp.dot(a, b, preferred_element_type=F32)


def _dot_nt(a, b):
    return lax.dot_general(a, b, NT_DIMS, preferred_element_type=F32)


def _proj_kernel(x_ref, nm_ref, wcq_ref, wckv_ref, wkpe_ref, wqb_ref, wkb_ref, wvb_ref,
                 gcq_ref, gckv_ref, gqb_ref, gkb_ref,
                 cqn_ref, ckv_ref, kpe_ref, qb_ref, kb_ref, vb_ref, kbt_ref, vbt_ref,
                 *, n_heads, tail_start):
    t = pl.program_id(1)
    xn = _rms(x_ref[0], nm_ref[...]).astype(BF16)
    cqn_ref[0] = _rms(_dot(xn, wcq_ref[...]), gcq_ref[...]).astype(BF16)
    ckv_ref[0] = _rms(_dot(xn, wckv_ref[...]), gckv_ref[...])
    kpe_ref[0] = _dot(xn, wkpe_ref[...])
    qb = _dot(xn, wqb_ref[...])
    kb = _dot(xn, wkb_ref[...])
    vb = _dot(xn, wvb_ref[...])
    vb_ref[0] = vb.astype(BF16)
    kbn = []
    for h in range(n_heads):
        sl = slice(h * D_BAND, (h + 1) * D_BAND)
        qb_ref[0, :, sl] = _rms(qb[:, sl], gqb_ref[...]).astype(BF16)
        kbn.append(_rms(kb[:, sl], gkb_ref[...]))
    kbn = jnp.concatenate(kbn, axis=-1)
    kb_ref[0] = kbn.astype(BF16)

    @pl.when(t >= tail_start)
    def _():
        kbt_ref[0] = kbn
        vbt_ref[0] = vb


def _proj(x, norm_mix, w, g_cq, g_ckv, g_qb, g_kb, *, keep, tm):
    B, S, D = x.shape
    n_heads = w["qb"].shape[1] // D_BAND
    hw = n_heads * D_BAND
    nt = S // tm
    tail_start = nt - keep // tm
    row = lambda b, t: (b, t, 0)
    const = lambda b, t: (0, 0)
    tail = lambda b, t: (b, jnp.maximum(t - tail_start, 0), 0)
    wspec = lambda a: pl.BlockSpec(a.shape, const)
    out_shapes = (
        jax.ShapeDtypeStruct((B, S, Q_LORA), BF16),
        jax.ShapeDtypeStruct((B, S, KV_LORA), F32),
        jax.ShapeDtypeStruct((B, S, LANES), F32),
        jax.ShapeDtypeStruct((B, S, hw), BF16),
        jax.ShapeDtypeStruct((B, S, hw), BF16),
        jax.ShapeDtypeStruct((B, S, hw), BF16),
        jax.ShapeDtypeStruct((B, keep, hw), F32),
        jax.ShapeDtypeStruct((B, keep, hw), F32),
    )
    out_specs = (
        pl.BlockSpec((1, tm, Q_LORA), row),
        pl.BlockSpec((1, tm, KV_LORA), row),
        pl.BlockSpec((1, tm, LANES), row),
        pl.BlockSpec((1, tm, hw), row),
        pl.BlockSpec((1, tm, hw), row),
        pl.BlockSpec((1, tm, hw), row),
        pl.BlockSpec((1, tm, hw), tail),
        pl.BlockSpec((1, tm, hw), tail),
    )
    weights = (w["cq"], w["ckv"], w["kpe"], w["qb"], w["kb"], w["vb"])
    gains = (g_cq, g_ckv, g_qb, g_kb)
    return pl.pallas_call(
        functools.partial(_proj_kernel, n_heads=n_heads, tail_start=tail_start),
        grid=(B, nt),
        in_specs=[pl.BlockSpec((1, tm, D), row), wspec(norm_mix)]
        + [wspec(a) for a in weights] + [wspec(a) for a in gains],
        out_specs=out_specs,
        out_shape=out_shapes,
        compiler_params=_params("arbitrary", "arbitrary"),
        name="proj",
    )(x, norm_mix, *weights, *gains)


def _q_up_kernel(cqn_ref, w_ref, gn_ref, gr_ref, grs_ref, aug_ref, cos_ref, sin_ref, q_ref,
                 *, n_heads):
    hw = n_heads * LANES
    y = _dot(cqn_ref[0], w_ref[...])
    cos = cos_ref[...]
    sin = sin_ref[...]
    for h in range(n_heads):
        sl = slice(h * LANES, (h + 1) * LANES)
        n = y[:, sl]
        r = y[:, hw + h * LANES: hw + (h + 1) * LANES]
        rs = y[:, 2 * hw + h * LANES: 2 * hw + (h + 1) * LANES]
        ss = jnp.sum(n * n, axis=-1, keepdims=True) + jnp.sum(r * r, axis=-1, keepdims=True)
        rinv = lax.rsqrt(ss * (1.0 / QK_MLA) + EPS)
        q_ref[0, h, :, 0:LANES] = (n * rinv * gn_ref[...]).astype(BF16)
        rot = (r * rinv * gr_ref[...]) * cos + (rs * rinv * grs_ref[...]) * sin
        q_ref[0, h, :, LANES:QK_PAD] = (rot + aug_ref[...]).astype(BF16)


def _q_up(cqn, w, g_n, g_r, g_rs, aug, cos, sin, *, tm):
    B, S, _ = cqn.shape
    n_heads = w.shape[1] // (3 * LANES)
    row = lambda b, t: (b, t, 0)
    const = lambda b, t: (0, 0)
    pos = lambda b, t: (t, 0)
    vec = pl.BlockSpec((1, LANES), const)
    return pl.pallas_call(
        functools.partial(_q_up_kernel, n_heads=n_heads),
        grid=(B, S // tm),
        in_specs=[pl.BlockSpec((1, tm, Q_LORA), row), pl.BlockSpec(w.shape, const),
                  vec, vec, vec, vec,
                  pl.BlockSpec((tm, LANES), pos), pl.BlockSpec((tm, LANES), pos)],
        out_specs=pl.BlockSpec((1, n_heads, tm, QK_PAD), lambda b, t: (b, 0, t, 0)),
        out_shape=jax.ShapeDtypeStruct((B, n_heads, S, QK_PAD), BF16),
        compiler_params=_params("arbitrary", "arbitrary"),
        name="q_up",
    )(cqn, w, g_n, g_r, g_rs, aug, cos, sin)


def _expand_kv(c, kpe, wuk, wuvt, g_n, g_r, k_aug, cos, sin, n_heads):
    cb = c.astype(BF16)
    kn = _dot(cb, wuk)
    vt = _dot_nt(wuvt, cb)
    a = kpe * g_r
    sspe = jnp.sum(kpe * kpe, axis=-1, keepdims=True)
    a_sw = pltpu.roll(a, HALF_ROPE, 1) + pltpu.roll(a, LANES - HALF_ROPE, 1)
    rot = a * cos + a_sw * sin
    keys = []
    for h in range(n_heads):
        n = kn[:, h * NOPE:(h + 1) * NOPE]
        ss = jnp.sum(n * n, axis=-1, keepdims=True) + sspe
        rinv = lax.rsqrt(ss * (1.0 / QK_MLA) + EPS)
        keys.append(jnp.concatenate([(n * rinv * g_n).astype(BF16),
                                     (rot * rinv + k_aug).astype(BF16)], axis=-1))
    return keys, vt


def _kv_up_kernel(ckv_ref, kpe_ref, wuk_ref, wuvt_ref, gn_ref, gr_ref, aug_ref, cos_ref, sin_ref,
                  k_ref, vt_ref, *, n_heads):
    keys, vt = _expand_kv(ckv_ref[0], kpe_ref[0], wuk_ref[...], wuvt_ref[...], gn_ref[...],
                          gr_ref[...], aug_ref[...], cos_ref[...], sin_ref[...], n_heads)
    for h in range(n_heads):
        k_ref[0, h] = keys[h]
        vt_ref[0, h] = vt[h * V_MLA:(h + 1) * V_MLA, :].astype(BF16)


def _kv_up(ckv, kpe_pad, w_uk, w_uvt, g_n, g_r, k_aug, cos, sin, *, tm):
    B, S, _ = ckv.shape
    n_heads = w_uk.shape[1] // NOPE
    row = lambda b, t: (b, t, 0)
    const = lambda b, t: (0, 0)
    pos = lambda b, t: (t, 0)
    vec = pl.BlockSpec((1, LANES), const)
    return pl.pallas_call(
        functools.partial(_kv_up_kernel, n_heads=n_heads),
        grid=(B, S // tm),
        in_specs=[pl.BlockSpec((1, tm, KV_LORA), row), pl.BlockSpec((1, tm, LANES), row),
                  pl.BlockSpec(w_uk.shape, const), pl.BlockSpec(w_uvt.shape, const),
                  vec, vec, vec,
                  pl.BlockSpec((tm, LANES), pos), pl.BlockSpec((tm, LANES), pos)],
        out_specs=(pl.BlockSpec((1, n_heads, tm, QK_PAD), lambda b, t: (b, 0, t, 0)),
                   pl.BlockSpec((1, n_heads, V_MLA, tm), lambda b, t: (b, 0, 0, t))),
        out_shape=(jax.ShapeDtypeStruct((B, n_heads, S, QK_PAD), BF16),
                   jax.ShapeDtypeStruct((B, n_heads, V_MLA, S), BF16)),
        compiler_params=_params("arbitrary", "arbitrary"),
        name="kv_up",
    )(ckv, kpe_pad, w_uk, w_uvt, g_n, g_r, k_aug, cos, sin)


def _mla_flash_kernel(q_ref, k_ref, vt_ref, o_ref, m_sc, l_sc, acc_sc, *, tq, tk, hg, bounded):
    i = pl.program_id(2)
    if not bounded:
        m_sc[...] = jnp.full(m_sc.shape, -jnp.inf, F32)
    l_sc[...] = jnp.zeros(l_sc.shape, F32)
    acc_sc[...] = jnp.zeros(acc_sc.shape, F32)

    def step(j, mask):
        k0 = pl.multiple_of(j * tk, tk)
        sts = [_dot_nt(k_ref[0, hh, pl.ds(k0, tk), :], q_ref[0, hh]) for hh in range(hg)]
        for hh, st in enumerate(sts):
            if mask is not None:
                st = jnp.where(mask, st, NEG)
            if bounded:
                p = jnp.exp2(st)
                l_sc[hh] += jnp.sum(p, axis=0, keepdims=True)
                acc_sc[hh] += _dot(vt_ref[0, hh, :, pl.ds(k0, tk)], p.astype(BF16))
            else:
                m_prev = m_sc[hh]
                m_new = jnp.maximum(m_prev, jnp.max(st, axis=0, keepdims=True))
                alpha = jnp.exp2(m_prev - m_new)
                p = jnp.exp2(st - m_new)
                l_sc[hh] = alpha * l_sc[hh] + jnp.sum(p, axis=0, keepdims=True)
                pv = _dot(vt_ref[0, hh, :, pl.ds(k0, tk)], p.astype(BF16))
                acc_sc[hh] = acc_sc[hh] * alpha + pv
                m_sc[hh] = m_new

    n_diag = tq // tk
    n_full = i * n_diag

    def body(j, carry):
        step(j, None)
        return carry

    lax.fori_loop(0, n_full, body, 0)
    q_chunk = lax.broadcasted_iota(jnp.int32, (tk, tq), 1) // CHUNK
    k_chunk = lax.broadcasted_iota(jnp.int32, (tk, tq), 0) // CHUNK
    for d in range(n_diag):
        step(n_full + d, k_chunk + d * (tk // CHUNK) <= q_chunk)
    for hh in range(hg):
        o_ref[0, :, hh * V_MLA:(hh + 1) * V_MLA] = (acc_sc[hh] / l_sc[hh]).T.astype(o_ref.dtype)


def _mla_flash(q, k, vt, *, tq, tk, hg, bounded):
    B, H, S, _ = q.shape
    return pl.pallas_call(
        functools.partial(_mla_flash_kernel, tq=tq, tk=tk, hg=hg, bounded=bounded),
        grid=(B, H // hg, S // tq),
        in_specs=[pl.BlockSpec((1, hg, tq, QK_PAD), lambda b, h, i: (b, h, i, 0)),
                  pl.BlockSpec((1, hg, S, QK_PAD), lambda b, h, i: (b, h, 0, 0)),
                  pl.BlockSpec((1, hg, V_MLA, S), lambda b, h, i: (b, h, 0, 0))],
        out_specs=pl.BlockSpec((1, tq, hg * V_MLA), lambda b, h, i: (b, i, h)),
        out_shape=jax.ShapeDtypeStruct((B, S, H * V_MLA), BF16),
        scratch_shapes=[pltpu.VMEM((hg, 1, tq), F32), pltpu.VMEM((hg, 1, tq), F32),
                        pltpu.VMEM((hg, V_MLA, tq), F32)],
        compiler_params=_params("arbitrary", "arbitrary", "arbitrary"),
        name="mla_flash_bounded" if bounded else "mla_flash",
    )(q, k, vt)


def _toeplitz(u_row, rows, cols):
    u = jnp.broadcast_to(u_row, (rows, u_row.shape[-1]))
    return pltpu.roll(u, 0, 1, stride=1, stride_axis=0)[:, :cols]


def _softmax_pv(parts, bounded):
    m = None
    if not bounded:
        for s, _ in parts:
            mx = jnp.max(s, axis=-1, keepdims=True)
            m = mx if m is None else jnp.maximum(m, mx)
    l = None
    acc = None
    for s, v in parts:
        p = jnp.exp2(s if bounded else s - m)
        ps = jnp.sum(p, axis=-1, keepdims=True)
        pv = _dot(p.astype(BF16), v)
        l = ps if l is None else l + ps
        acc = pv if acc is None else acc + pv
    return acc / l


def _band_kernel(q_ref, kc_ref, kp_ref, vc_ref, vp_ref, u_ref, o_ref, bc_sc, bp_sc, *, bounded):
    g = pl.program_id(2)
    tg = bc_sc.shape[0]

    @pl.when(g == 0)
    def _():
        qc = lax.broadcasted_iota(jnp.int32, (tg, tg), 0) // CHUNK
        kc = lax.broadcasted_iota(jnp.int32, (tg, tg), 1) // CHUNK
        bc_sc[...] = jnp.where(kc <= qc, _toeplitz(u_ref[0, 0:1, :], tg, tg), NEG)
        bp_sc[...] = jnp.where(kc >= qc, _toeplitz(u_ref[0, 1:2, :], tg, tg), NEG)

    q = q_ref[0]
    s_cur = _dot_nt(q, kc_ref[0]) + bc_sc[...]

    @pl.when(g == 0)
    def _():
        o_ref[0] = _softmax_pv([(s_cur, vc_ref[0])], bounded).astype(o_ref.dtype)

    @pl.when(g > 0)
    def _():
        s_prev = _dot_nt(q, kp_ref[0]) + bp_sc[...]
        o_ref[0] = _softmax_pv([(s_prev, vp_ref[0]), (s_cur, vc_ref[0])], bounded).astype(o_ref.dtype)


def _band(qb, kb, vb, u_tab, *, bounded):
    B, S, hw = qb.shape
    H = hw // D_BAND
    tg = BAND_PAST
    cur = lambda b, h, g: (b, g, h)
    prev = lambda b, h, g: (b, jnp.maximum(g - 1, 0), h)
    blk = (1, tg, D_BAND)
    return pl.pallas_call(
        functools.partial(_band_kernel, bounded=bounded),
        grid=(B, H, S // tg),
        in_specs=[pl.BlockSpec(blk, cur), pl.BlockSpec(blk, cur), pl.BlockSpec(blk, prev),
                  pl.BlockSpec(blk, cur), pl.BlockSpec(blk, prev),
                  pl.BlockSpec((1,) + u_tab.shape[1:], lambda b, h, g: (h, 0, 0))],
        out_specs=pl.BlockSpec(blk, cur),
        out_shape=jax.ShapeDtypeStruct((B, S, hw), BF16),
        scratch_shapes=[pltpu.VMEM((tg, tg), F32), pltpu.VMEM((tg, tg), F32)],
        compiler_params=_params("arbitrary", "arbitrary", "arbitrary"),
        name="band_bounded" if bounded else "band",
    )(qb, kb, kb, vb, vb, u_tab)


def _mla_sample_kernel(q_ref, cc_ref, pc_ref, cn_ref, pn_ref, wuk_ref, wuvt_ref, gn_ref, gr_ref,
                       aug_ref, cosc_ref, sinc_ref, cosn_ref, sinn_ref, o_ref, *, n_heads, t_new):
    args = (wuk_ref[...], wuvt_ref[...], gn_ref[...], gr_ref[...], aug_ref[...])
    kc, vtc = _expand_kv(cc_ref[0], pc_ref[0], *args, cosc_ref[...], sinc_ref[...], n_heads)
    kn, vtn = _expand_kv(cn_ref[0], pn_ref[0], *args, cosn_ref[...], sinn_ref[...], n_heads)
    rows_n = cn_ref.shape[1]
    new_ok = lax.broadcasted_iota(jnp.int32, (q_ref.shape[2], rows_n), 1) < t_new
    for h in range(n_heads):
        q = q_ref[0, h]
        s_c = _dot_nt(q, kc[h])
        s_n = jnp.where(new_ok, _dot_nt(q, kn[h]), NEG)
        m = jnp.maximum(jnp.max(s_c, axis=-1, keepdims=True), jnp.max(s_n, axis=-1, keepdims=True))
        p_c = jnp.exp2(s_c - m)
        p_n = jnp.exp2(s_n - m)
        l = jnp.sum(p_c, axis=-1, keepdims=True) + jnp.sum(p_n, axis=-1, keepdims=True)
        sl = slice(h * V_MLA, (h + 1) * V_MLA)
        pv = (_dot_nt(p_c.astype(BF16), vtc[sl].astype(BF16))
              + _dot_nt(p_n.astype(BF16), vtn[sl].astype(BF16)))
        o_ref[0, :, sl] = (pv / l).astype(o_ref.dtype)


def _mla_sample(q, ckv_c, kpe_c, ckv_n, kpe_n, w_uk, w_uvt, g_n, g_r, k_aug, tabs_c, tabs_n, *, t_new):
    Bd, H, T, _ = q.shape
    P = ckv_c.shape[1]
    rn = ckv_n.shape[1]
    const = lambda b: (0, 0)
    bat = lambda b: (b, 0, 0)
    vec = pl.BlockSpec((1, LANES), const)
    return pl.pallas_call(
        functools.partial(_mla_sample_kernel, n_heads=H, t_new=t_new),
        grid=(Bd,),
        in_specs=[pl.BlockSpec((1, H, T, QK_PAD), lambda b: (b, 0, 0, 0)),
                  pl.BlockSpec((1, P, KV_LORA), bat), pl.BlockSpec((1, P, LANES), bat),
                  pl.BlockSpec((1, rn, KV_LORA), bat), pl.BlockSpec((1, rn, LANES), bat),
                  pl.BlockSpec(w_uk.shape, const), pl.BlockSpec(w_uvt.shape, const), vec, vec, vec,
                  pl.BlockSpec((P, LANES), const), pl.BlockSpec((P, LANES), const),
                  pl.BlockSpec((rn, LANES), const), pl.BlockSpec((rn, LANES), const)],
        out_specs=pl.BlockSpec((1, T, H * V_MLA), bat),
        out_shape=jax.ShapeDtypeStruct((Bd, T, H * V_MLA), BF16),
        compiler_params=_params("arbitrary"),
        name="mla_sample",
    )(q, ckv_c, kpe_c, ckv_n, kpe_n, w_uk, w_uvt, g_n, g_r, k_aug, *tabs_c, *tabs_n)


def _band_sample_kernel(q_ref, kc_ref, vc_ref, kn_ref, vn_ref, uc_ref, un_ref, o_ref,
                        *, n_heads, t_new):
    T = q_ref.shape[1]
    lc = kc_ref.shape[1]
    rn = kn_ref.shape[1]
    new_ok = lax.broadcasted_iota(jnp.int32, (T, rn), 1) < t_new
    for h in range(n_heads):
        sl = slice(h * D_BAND, (h + 1) * D_BAND)
        q = q_ref[0, :, sl]
        s_c = _dot_nt(q, kc_ref[0, :, sl].astype(BF16)) + _toeplitz(uc_ref[h:h + 1, :], T, lc)
        s_n = jnp.where(new_ok, _dot_nt(q, kn_ref[0, :, sl]) + _toeplitz(un_ref[h:h + 1, :], T, rn), NEG)
        parts = [(s_c, vc_ref[0, :, sl].astype(BF16)), (s_n, vn_ref[0, :, sl])]
        o_ref[0, :, sl] = _softmax_pv(parts, False).astype(o_ref.dtype)


def _band_sample(q, k_c, v_c, k_n, v_n, u_c, u_n, *, t_new):
    Bd, T, hw = q.shape
    H = hw // D_BAND
    bat = lambda b: (b, 0, 0)
    const = lambda b: (0, 0)
    return pl.pallas_call(
        functools.partial(_band_sample_kernel, n_heads=H, t_new=t_new),
        grid=(Bd,),
        in_specs=[pl.BlockSpec((1, T, hw), bat),
                  pl.BlockSpec((1,) + k_c.shape[1:], bat), pl.BlockSpec((1,) + v_c.shape[1:], bat),
                  pl.BlockSpec((1,) + k_n.shape[1:], bat), pl.BlockSpec((1,) + v_n.shape[1:], bat),
                  pl.BlockSpec(u_c.shape, const), pl.BlockSpec(u_n.shape, const)],
        out_specs=pl.BlockSpec((1, T, hw), bat),
        out_shape=jax.ShapeDtypeStruct((Bd, T, hw), BF16),
        compiler_params=_params("arbitrary"),
        name="band_sample",
    )(q, k_c, v_c, k_n, v_n, u_c, u_n)


def _merge_kernel(x_ref, oa_ref, ob_ref, woa_ref, wob_ref, h_ref):
    h_ref[0] = x_ref[0] + _dot(oa_ref[0], woa_ref[...]) + _dot(ob_ref[0], wob_ref[...])


def _merge(x, oa, ob, w_oa, w_ob, *, tm):
    B, S, D = x.shape
    row = lambda b, t: (b, t, 0)
    const = lambda b, t: (0, 0)
    return pl.pallas_call(
        _merge_kernel,
        grid=(B, S // tm),
        in_specs=[pl.BlockSpec((1, tm, D), row), pl.BlockSpec((1, tm, oa.shape[-1]), row),
                  pl.BlockSpec((1, tm, ob.shape[-1]), row),
                  pl.BlockSpec(w_oa.shape, const), pl.BlockSpec(w_ob.shape, const)],
        out_specs=pl.BlockSpec((1, tm, D), row),
        out_shape=jax.ShapeDtypeStruct((B, S, D), F32),
        compiler_params=_params("arbitrary", "arbitrary"),
        name="merge",
    )(x, oa, ob, w_oa, w_ob)


def _ffn_kernel(h_ref, g_ref, wup_ref, wdn_ref, y_ref, hn_sc, acc_sc):
    j = pl.program_id(2)

    @pl.when(j == 0)
    def _():
        h = h_ref[0]
        hn_sc[...] = _rms(h, g_ref[...]).astype(BF16)
        acc_sc[...] = h

    u = jnp.maximum(_dot(hn_sc[...], wup_ref[...]), 0.0)
    acc_sc[...] += _dot((u * u).astype(BF16), wdn_ref[...])

    @pl.when(j == pl.num_programs(2) - 1)
    def _():
        y_ref[0] = acc_sc[...]


def _ffn(h, g, w_up, w_down, *, tm, tf):
    B, S, D = h.shape
    F = w_up.shape[1]
    row = lambda b, t, j: (b, t, 0)
    return pl.pallas_call(
        _ffn_kernel,
        grid=(B, S // tm, F // tf),
        in_specs=[pl.BlockSpec((1, tm, D), row), pl.BlockSpec(g.shape, lambda b, t, j: (0, 0)),
                  pl.BlockSpec((D, tf), lambda b, t, j: (0, j)),
                  pl.BlockSpec((tf, D), lambda b, t, j: (j, 0))],
        out_specs=pl.BlockSpec((1, tm, D), row),
        out_shape=jax.ShapeDtypeStruct((B, S, D), F32),
        scratch_shapes=[pltpu.VMEM((tm, D), BF16), pltpu.VMEM((tm, D), F32)],
        compiler_params=_params("arbitrary", "arbitrary", "arbitrary"),
        name="ffn",
    )(h, g, w_up, w_down)


def _rope_tables(pos):
    inv = 1.0 / (ROPE_BASE ** (jnp.arange(0, ROPE, 2, dtype=F32) / ROPE))
    ang = pos.astype(F32)[:, None] * inv[None, :]
    c, s = jnp.cos(ang), jnp.sin(ang)
    z = jnp.zeros((pos.shape[0], LANES - ROPE), F32)
    return jnp.concatenate([c, c, z], axis=1), jnp.concatenate([-s, s, z], axis=1)


def _pad_lanes(a, width=LANES):
    return jnp.pad(a, [(0, 0)] * (a.ndim - 1) + [(0, width - a.shape[-1])])


def _pad_rows(a, rows):
    return jnp.pad(a, [(0, 0), (0, rows - a.shape[1]), (0, 0)])


def _swap_halves(a):
    return jnp.concatenate([a[..., HALF_ROPE:], a[..., :HALF_ROPE]], axis=-1)


def _prep_weights(w_in, w_uq, w_uk, w_uv, w_o, w_up, w_down):
    hb3 = (w_in.shape[1] - Q_LORA - KV_LORA - ROPE) // 3
    o = [0, Q_LORA, Q_LORA + KV_LORA, Q_LORA + KV_LORA + ROPE]
    o += [o[3] + hb3, o[3] + 2 * hb3, o[3] + 3 * hb3]
    wb = w_in.astype(BF16)
    w = {"cq": wb[:, o[0]:o[1]], "ckv": wb[:, o[1]:o[2]], "kpe": _pad_lanes(wb[:, o[2]:o[3]]),
         "qb": wb[:, o[3]:o[4]], "kb": wb[:, o[4]:o[5]], "vb": wb[:, o[5]:o[6]]}
    ha = w_uq.shape[1] // QK_MLA
    uq = w_uq.astype(BF16).reshape(Q_LORA, ha, QK_MLA)
    nope = uq[:, :, :NOPE].reshape(Q_LORA, ha * NOPE)
    rope = uq[:, :, NOPE:]
    w["uq"] = jnp.concatenate(
        [nope, _pad_lanes(rope).reshape(Q_LORA, ha * LANES),
         _pad_lanes(_swap_halves(rope)).reshape(Q_LORA, ha * LANES)], axis=1)
    w["uk"] = w_uk.astype(BF16)
    w["uvt"] = w_uv.astype(BF16).T
    half = ha * V_MLA
    ob = w_o.astype(BF16)
    w["oa"], w["ob"] = ob[:half], ob[half:]
    w["up"], w["down"] = w_up.astype(BF16), w_down.astype(BF16)
    return w


def _softmax_setup(g_qa, g_ka, g_qb, g_kb, rel_bias):
    c_a = QK_MLA ** -0.5 * LOG2E
    c_b = D_BAND ** -0.5 * LOG2E
    amax = lambda a: jnp.max(jnp.abs(a))
    bound_a = c_a * QK_MLA * amax(g_qa) * amax(g_ka) * BOUND_MARGIN
    bound_b = c_b * D_BAND * amax(g_qb) * amax(g_kb) * BOUND_MARGIN + LOG2E * amax(rel_bias)
    fast_a = bound_a <= FAST_LIMIT
    fast_b = bound_b <= FAST_LIMIT
    shift_a = jnp.where(fast_a, bound_a, 0.0)
    shift_b = jnp.where(fast_b, bound_b, 0.0)
    lane = jnp.arange(LANES)[None, :]
    g = {"qa_n": g_qa[None, :NOPE] * c_a, "qa_r": _pad_lanes(g_qa[None, NOPE:]) * c_a,
         "qa_rs": _pad_lanes(_swap_halves(g_qa[None, NOPE:])) * c_a,
         "q_aug": (lane == AUG_LANE).astype(F32),
         "ka_n": g_ka[None, :NOPE], "ka_r": _pad_lanes(g_ka[None, NOPE:]),
         "k_aug": jnp.where(lane == AUG_LANE, -shift_a, 0.0).astype(F32),
         "qb": g_qb * c_b}
    return g, fast_a, fast_b, shift_b


def _band_vectors(rel_bias, shift):
    t = BAND_PAST
    e = np.arange(2 * t)
    e = np.where(e < t, e, e - 2 * t)
    idx = np.stack([np.clip(-e, -MAX_REL, MAX_REL), np.clip(t - e, -MAX_REL, MAX_REL)]) + MAX_REL
    return (rel_bias[:, idx] * LOG2E - shift).astype(F32)


def _band_sample_vectors(rel_bias, lb, rn):
    lc = lb + LANES
    e = np.arange(lc)
    e = np.where(e < lb, e, e - lc)
    u_c = rel_bias[:, np.clip(lb - e, -MAX_REL, MAX_REL) + MAX_REL] * LOG2E
    e = np.arange(2 * rn)
    e = np.where(e < rn, e, e - 2 * rn)
    u_n = rel_bias[:, np.clip(-e, -MAX_REL, MAX_REL) + MAX_REL] * LOG2E
    return u_c.astype(F32), u_n.astype(F32)


def _row_tile(n, pref):
    return pref if n % pref == 0 else n


def kernel(x_prompt, x_sample, cache_mla_ckv, cache_mla_kpe, cache_band_k, cache_band_v,
           norm_mix, w_in, g_cq, w_uq, g_ckv, w_uk, w_uv, g_qa, g_ka, g_qb, g_kb, rel_bias,
           w_o, norm_ffn, w_up, w_down):
    depth = w_in.shape[0]
    assert depth == 1, "single-layer step"
    B, S, D = x_prompt.shape
    Bd, T, _ = x_sample.shape
    P = cache_mla_ckv.shape[2]
    Lb = cache_band_k.shape[2]
    keep_p = min(BAND_PAST, S)
    assert S % BAND_PAST == 0 and T <= CHUNK

    w = _prep_weights(w_in[0], w_uq[0], w_uk[0], w_uv[0], w_o[0], w_up[0], w_down[0])
    g, fast_a, fast_b, shift_b = _softmax_setup(g_qa[0], g_ka[0], g_qb, g_kb, rel_bias[0])
    ha = w["uk"].shape[1] // NOPE
    hb = rel_bias.shape[1]
    hw = hb * D_BAND
    gains = (g_cq, g_ckv, g["qb"], g_kb)
    q_args = (w["uq"], g["qa_n"], g["qa_r"], g["qa_rs"], g["q_aug"])
    kv_args = (w["uk"], w["uvt"], g["ka_n"], g["ka_r"], g["k_aug"])

    cqn, ckv, kpe_pad, qb, kb, vb, kb_tail, vb_tail = _proj(
        x_prompt, norm_mix, w, *gains, keep=keep_p, tm=_row_tile(S, 256))
    tabs_p = _rope_tables(jnp.arange(S, dtype=jnp.int32))
    tmq = _row_tile(S, 512)
    q = _q_up(cqn, *q_args, *tabs_p, tm=tmq)
    k, vt = _kv_up(ckv, kpe_pad, *kv_args, *tabs_p, tm=tmq)
    flash = lambda bounded: functools.partial(_mla_flash, tq=tmq, tk=tmq, hg=2, bounded=bounded)
    oa = lax.cond(fast_a, flash(True), flash(False), q, k, vt)
    u_tab = _band_vectors(rel_bias[0], shift_b)
    ob = lax.cond(fast_b, functools.partial(_band, bounded=True),
                  functools.partial(_band, bounded=False), qb, kb, vb, u_tab)
    h = _merge(x_prompt, oa, ob, w["oa"], w["ob"], tm=_row_tile(S, 512))
    y_prompt = _ffn(h, norm_ffn, w["up"], w["down"], tm=_row_tile(S, 512), tf=1024)

    n_s = Bd * T
    xs = x_sample.reshape(1, n_s, D)
    cqn_s, ckv_s, kpe_pad_s, qb_s, kb_s, vb_s, kb_s32, vb_s32 = _proj(
        xs, norm_mix, w, *gains, keep=n_s, tm=_row_tile(n_s, 256))
    pos_s = P + jnp.arange(T, dtype=jnp.int32)
    rn = LANES
    q_s = _q_up(cqn_s.reshape(Bd, T, Q_LORA), *q_args, *_rope_tables(pos_s), tm=T)
    oa_s = _mla_sample(
        q_s, cache_mla_ckv[0], _pad_lanes(cache_mla_kpe[0]),
        _pad_rows(ckv_s.reshape(Bd, T, KV_LORA), rn), _pad_rows(kpe_pad_s.reshape(Bd, T, LANES), rn),
        *kv_args, _rope_tables(jnp.arange(P, dtype=jnp.int32)),
        _rope_tables(P + jnp.arange(rn, dtype=jnp.int32)), t_new=T)
    u_c, u_n = _band_sample_vectors(rel_bias[0], Lb, rn)
    ob_s = _band_sample(
        qb_s.reshape(Bd, T, hw), cache_band_k[0].reshape(Bd, Lb, hw), cache_band_v[0].reshape(Bd, Lb, hw),
        _pad_rows(kb_s.reshape(Bd, T, hw), rn), _pad_rows(vb_s.reshape(Bd, T, hw), rn),
        u_c, u_n, t_new=T)
    h_s = _merge(xs, oa_s.reshape(1, n_s, ha * V_MLA), ob_s.reshape(1, n_s, hw), w["oa"], w["ob"],
                 tm=n_s)
    y_sample = _ffn(h_s, norm_ffn, w["up"], w["down"], tm=n_s, tf=1024).reshape(Bd, T, D)

    return (y_prompt, y_sample,
            ckv[None], kpe_pad[None, ..., :ROPE],
            kb_tail.reshape(B, keep_p, hb, D_BAND)[None], vb_tail.reshape(B, keep_p, hb, D_BAND)[None],
            ckv_s.reshape(Bd, T, KV_LORA)[None], kpe_pad_s.reshape(Bd, T, LANES)[None, ..., :ROPE],
            kb_s32.reshape(Bd, T, hb, D_BAND)[None], vb_s32.reshape(Bd, T, hb, D_BAND)[None])
```

```python
import functools

import jax
import jax.numpy as jnp
import numpy as np
from jax import lax
from jax.experimental import pallas as pl
from jax.experimental.pallas import tpu as pltpu

CHUNK = 64
EPS = 1e-6
NOPE = 128
ROPE = 64
HALF_ROPE = ROPE // 2
QK_MLA = NOPE + ROPE
V_MLA = 128
D_BAND = 128
BAND_CHUNKS = 8
BAND_PAST = BAND_CHUNKS * CHUNK
MAX_REL = 128
ROPE_BASE = 10000.0
Q_LORA = 512
KV_LORA = 256
NEG = -1e30
LOG2E = 1.4426950408889634

BOUND_MARGIN = 1.02
FAST_LIMIT = 60.0

LANES = 128
QK_PAD = 2 * LANES
AUG_LANE = ROPE
VMEM_LIMIT = 56 * 1024 * 1024

BF16 = jnp.bfloat16
F32 = jnp.float32

NT_DIMS = (((1,), (1,)), ((), ()))


def _params(*sem):
    return pltpu.CompilerParams(dimension_semantics=sem, vmem_limit_bytes=VMEM_LIMIT)


def _rms(x, g):
    ms = jnp.mean(x * x, axis=-1, keepdims=True)
    return x * lax.rsqrt(ms + EPS) * g


def _dot(a, b):
    return jnp.dot(a, b, preferred_element_type=F32)


def _dot_nt(a, b):
    return lax.dot_general(a, b, NT_DIMS, preferred_element_type=F32)


def _proj_kernel(x_ref, nm_ref, wcq_ref, wckv_ref, wkpe_ref, wqb_ref, wkb_ref, wvb_ref,
                 gcq_ref, gckv_ref, gqb_ref, gkb_ref,
                 cqn_ref, ckv_ref, kpe_ref, qb_ref, kb_ref, vb_ref, kbt_ref, vbt_ref,
                 *, n_heads, tail_start):
    t = pl.program_id(1)
    xn = _rms(x_ref[0], nm_ref[...]).astype(BF16)
    cqn_ref[0] = _rms(_dot(xn, wcq_ref[...]), gcq_ref[...]).astype(BF16)
    ckv_ref[0] = _rms(_dot(xn, wckv_ref[...]), gckv_ref[...])
    kpe_ref[0] = _dot(xn, wkpe_ref[...])
    qb = _dot(xn, wqb_ref[...])
    kb = _dot(xn, wkb_ref[...])
    vb = _dot(xn, wvb_ref[...])
    vb_ref[0] = vb.astype(BF16)
    kbn = []
    for h in range(n_heads):
        sl = slice(h * D_BAND, (h + 1) * D_BAND)
        qb_ref[0, :, sl] = _rms(qb[:, sl], gqb_ref[...]).astype(BF16)
        kbn.append(_rms(kb[:, sl], gkb_ref[...]))
    kbn = jnp.concatenate(kbn, axis=-1)
    kb_ref[0] = kbn.astype(BF16)

    @pl.when(t >= tail_start)
    def _():
        kbt_ref[0] = kbn
        vbt_ref[0] = vb


def _proj(x, norm_mix, w, g_cq, g_ckv, g_qb, g_kb, *, keep, tm):
    B, S, D = x.shape
    n_heads = w["qb"].shape[1] // D_BAND
    hw = n_heads * D_BAND
    nt = S // tm
    tail_start = nt - keep // tm
    row = lambda b, t: (b, t, 0)
    const = lambda b, t: (0, 0)
    tail = lambda b, t: (b, jnp.maximum(t - tail_start, 0), 0)
    wspec = lambda a: pl.BlockSpec(a.shape, const)
    out_shapes = (
        jax.ShapeDtypeStruct((B, S, Q_LORA), BF16),
        jax.ShapeDtypeStruct((B, S, KV_LORA), F32),
        jax.ShapeDtypeStruct((B, S, LANES), F32),
        jax.ShapeDtypeStruct((B, S, hw), BF16),
        jax.ShapeDtypeStruct((B, S, hw), BF16),
        jax.ShapeDtypeStruct((B, S, hw), BF16),
        jax.ShapeDtypeStruct((B, keep, hw), F32),
        jax.ShapeDtypeStruct((B, keep, hw), F32),
    )
    out_specs = (
        pl.BlockSpec((1, tm, Q_LORA), row),
        pl.BlockSpec((1, tm, KV_LORA), row),
        pl.BlockSpec((1, tm, LANES), row),
        pl.BlockSpec((1, tm, hw), row),
        pl.BlockSpec((1, tm, hw), row),
        pl.BlockSpec((1, tm, hw), row),
        pl.BlockSpec((1, tm, hw), tail),
        pl.BlockSpec((1, tm, hw), tail),
    )
    weights = (w["cq"], w["ckv"], w["kpe"], w["qb"], w["kb"], w["vb"])
    gains = (g_cq, g_ckv, g_qb, g_kb)
    return pl.pallas_call(
        functools.partial(_proj_kernel, n_heads=n_heads, tail_start=tail_start),
        grid=(B, nt),
        in_specs=[pl.BlockSpec((1, tm, D), row), wspec(norm_mix)]
        + [wspec(a) for a in weights] + [wspec(a) for a in gains],
        out_specs=out_specs,
        out_shape=out_shapes,
        compiler_params=_params("arbitrary", "arbitrary"),
        name="proj",
    )(x, norm_mix, *weights, *gains)


def _q_up_kernel(cqn_ref, w_ref, gn_ref, gr_ref, grs_ref, aug_ref, cos_ref, sin_ref, q_ref,
                 *, n_heads):
    hw = n_heads * LANES
    y = _dot(cqn_ref[0], w_ref[...])
    cos = cos_ref[...]
    sin = sin_ref[...]
    for h in range(n_heads):
        sl = slice(h * LANES, (h + 1) * LANES)
        n = y[:, sl]
        r = y[:, hw + h * LANES: hw + (h + 1) * LANES]
        rs = y[:, 2 * hw + h * LANES: 2 * hw + (h + 1) * LANES]
        ss = jnp.sum(n * n, axis=-1, keepdims=True) + jnp.sum(r * r, axis=-1, keepdims=True)
        rinv = lax.rsqrt(ss * (1.0 / QK_MLA) + EPS)
        q_ref[0, h, :, 0:LANES] = (n * rinv * gn_ref[...]).astype(BF16)
        rot = (r * rinv * gr_ref[...]) * cos + (rs * rinv * grs_ref[...]) * sin
        q_ref[0, h, :, LANES:QK_PAD] = (rot + aug_ref[...]).astype(BF16)


def _q_up(cqn, w, g_n, g_r, g_rs, aug, cos, sin, *, tm):
    B, S, _ = cqn.shape
    n_heads = w.shape[1] // (3 * LANES)
    row = lambda b, t: (b, t, 0)
    const = lambda b, t: (0, 0)
    pos = lambda b, t: (t, 0)
    vec = pl.BlockSpec((1, LANES), const)
    return pl.pallas_call(
        functools.partial(_q_up_kernel, n_heads=n_heads),
        grid=(B, S // tm),
        in_specs=[pl.BlockSpec((1, tm, Q_LORA), row), pl.BlockSpec(w.shape, const),
                  vec, vec, vec, vec,
                  pl.BlockSpec((tm, LANES), pos), pl.BlockSpec((tm, LANES), pos)],
        out_specs=pl.BlockSpec((1, n_heads, tm, QK_PAD), lambda b, t: (b, 0, t, 0)),
        out_shape=jax.ShapeDtypeStruct((B, n_heads, S, QK_PAD), BF16),
        compiler_params=_params("arbitrary", "arbitrary"),
        name="q_up",
    )(cqn, w, g_n, g_r, g_rs, aug, cos, sin)


def _expand_kv(c, kpe, wuk, wuvt, g_n, g_r, k_aug, cos, sin, n_heads):
    cb = c.astype(BF16)
    kn = _dot(cb, wuk)
    vt = _dot_nt(wuvt, cb)
    a = kpe * g_r
    sspe = jnp.sum(kpe * kpe, axis=-1, keepdims=True)
    a_sw = pltpu.roll(a, HALF_ROPE, 1) + pltpu.roll(a, LANES - HALF_ROPE, 1)
    rot = a * cos + a_sw * sin
    keys = []
    for h in range(n_heads):
        n = kn[:, h * NOPE:(h + 1) * NOPE]
        ss = jnp.sum(n * n, axis=-1, keepdims=True) + sspe
        rinv = lax.rsqrt(ss * (1.0 / QK_MLA) + EPS)
        keys.append(jnp.concatenate([(n * rinv * g_n).astype(BF16),
                                     (rot * rinv + k_aug).astype(BF16)], axis=-1))
    return keys, vt


def _kv_up_kernel(ckv_ref, kpe_ref, wuk_ref, wuvt_ref, gn_ref, gr_ref, aug_ref, cos_ref, sin_ref,
                  k_ref, vt_ref, *, n_heads):
    keys, vt = _expand_kv(ckv_ref[0], kpe_ref[0], wuk_ref[...], wuvt_ref[...], gn_ref[...],
                          gr_ref[...], aug_ref[...], cos_ref[...], sin_ref[...], n_heads)
    for h in range(n_heads):
        k_ref[0, h] = keys[h]
        vt_ref[0, h] = vt[h * V_MLA:(h + 1) * V_MLA, :].astype(BF16)


def _kv_up(ckv, kpe_pad, w_uk, w_uvt, g_n, g_r, k_aug, cos, sin, *, tm):
    B, S, _ = ckv.shape
    n_heads = w_uk.shape[1] // NOPE
    row = lambda b, t: (b, t, 0)
    const = lambda b, t: (0, 0)
    pos = lambda b, t: (t, 0)
    vec = pl.BlockSpec((1, LANES), const)
    return pl.pallas_call(
        functools.partial(_kv_up_kernel, n_heads=n_heads),
        grid=(B, S // tm),
        in_specs=[pl.BlockSpec((1, tm, KV_LORA), row), pl.BlockSpec((1, tm, LANES), row),
                  pl.BlockSpec(w_uk.shape, const), pl.BlockSpec(w_uvt.shape, const),
                  vec, vec, vec,
                  pl.BlockSpec((tm, LANES), pos), pl.BlockSpec((tm, LANES), pos)],
        out_specs=(pl.BlockSpec((1, n_heads, tm, QK_PAD), lambda b, t: (b, 0, t, 0)),
                   pl.BlockSpec((1, n_heads, V_MLA, tm), lambda b, t: (b, 0, 0, t))),
        out_shape=(jax.ShapeDtypeStruct((B, n_heads, S, QK_PAD), BF16),
                   jax.ShapeDtypeStruct((B, n_heads, V_MLA, S), BF16)),
        compiler_params=_params("arbitrary", "arbitrary"),
        name="kv_up",
    )(ckv, kpe_pad, w_uk, w_uvt, g_n, g_r, k_aug, cos, sin)


def _mla_flash_kernel(q_ref, k_ref, vt_ref, o_ref, m_sc, l_sc, acc_sc, *, tq, td, hg, bounded):
    i = pl.program_id(2)
    if not bounded:
        m_sc[...] = jnp.full(m_sc.shape, -jnp.inf, F32)
    l_sc[...] = jnp.zeros(l_sc.shape, F32)
    acc_sc[...] = jnp.zeros(acc_sc.shape, F32)

    def step(tiles):
        sts = [(hh, t, _dot_nt(k_ref[0, hh, pl.ds(t[0], t[1]), :], q_ref[0, hh, t[2]:, :]))
               for t in tiles for hh in range(hg)]
        for hh, (k0, kn, q0, mask), st in sts:
            if mask is not None:
                st = jnp.where(mask, st, NEG)
            vt = vt_ref[0, hh, :, pl.ds(k0, kn)]
            if bounded:
                p = jnp.exp2(st)
                l_sc[hh, :, q0:] += jnp.sum(p, axis=0, keepdims=True)
                acc_sc[hh, :, q0:] += _dot(vt, p.astype(BF16))
            else:
                m_prev = m_sc[hh, :, q0:]
                m_new = jnp.maximum(m_prev, jnp.max(st, axis=0, keepdims=True))
                alpha = jnp.exp2(m_prev - m_new)
                p = jnp.exp2(st - m_new)
                l_sc[hh, :, q0:] = alpha * l_sc[hh, :, q0:] + jnp.sum(p, axis=0, keepdims=True)
                acc_sc[hh, :, q0:] = acc_sc[hh, :, q0:] * alpha + _dot(vt, p.astype(BF16))
                m_sc[hh, :, q0:] = m_new

    def body(j, carry):
        step([(pl.multiple_of(j * tq, tq), tq, 0, None)])
        return carry

    lax.fori_loop(0, i, body, 0)
    diag = []
    for d in range(tq // td):
        nq = tq - d * td
        q_chunk = lax.broadcasted_iota(jnp.int32, (td, nq), 1) // CHUNK
        k_chunk = lax.broadcasted_iota(jnp.int32, (td, nq), 0) // CHUNK
        diag.append((pl.multiple_of(i * tq + d * td, td), td, d * td, k_chunk <= q_chunk))
    step(diag)
    for hh in range(hg):
        o_ref[0, :, hh * V_MLA:(hh + 1) * V_MLA] = (acc_sc[hh] / l_sc[hh]).T.astype(o_ref.dtype)


def _mla_flash(q, k, vt, *, tq, td, hg, bounded):
    B, H, S, _ = q.shape
    return pl.pallas_call(
        functools.partial(_mla_flash_kernel, tq=tq, td=td, hg=hg, bounded=bounded),
        grid=(B, H // hg, S // tq),
        in_specs=[pl.BlockSpec((1, hg, tq, QK_PAD), lambda b, h, i: (b, h, i, 0)),
                  pl.BlockSpec((1, hg, S, QK_PAD), lambda b, h, i: (b, h, 0, 0)),
                  pl.BlockSpec((1, hg, V_MLA, S), lambda b, h, i: (b, h, 0, 0))],
        out_specs=pl.BlockSpec((1, tq, hg * V_MLA), lambda b, h, i: (b, i, h)),
        out_shape=jax.ShapeDtypeStruct((B, S, H * V_MLA), BF16),
        scratch_shapes=[pltpu.VMEM((hg, 1, tq), F32), pltpu.VMEM((hg, 1, tq), F32),
                        pltpu.VMEM((hg, V_MLA, tq), F32)],
        compiler_params=_params("arbitrary", "arbitrary", "arbitrary"),
        name="mla_flash_bounded" if bounded else "mla_flash",
    )(q, k, vt)


def _toeplitz(u_row, rows, cols):
    u = jnp.broadcast_to(u_row, (rows, u_row.shape[-1]))
    return pltpu.roll(u, 0, 1, stride=1, stride_axis=0)[:, :cols]


def _softmax_pv(parts, bounded):
    m = None
    if not bounded:
        for s, _ in parts:
            mx = jnp.max(s, axis=-1, keepdims=True)
            m = mx if m is None else jnp.maximum(m, mx)
    l = None
    acc = None
    for s, v in parts:
        p = jnp.exp2(s if bounded else s - m)
        ps = jnp.sum(p, axis=-1, keepdims=True)
        pv = _dot(p.astype(BF16), v)
        l = ps if l is None else l + ps
        acc = pv if acc is None else acc + pv
    return acc / l


def _band_kernel(q_ref, kc_ref, kp_ref, vc_ref, vp_ref, u_ref, o_ref, bc_sc, bp_sc, *, n_heads, bounded):
    b, g = pl.program_id(0), pl.program_id(1)
    tg = bc_sc.shape[1]

    @pl.when((b == 0) & (g == 0))
    def _():
        qc = lax.broadcasted_iota(jnp.int32, (tg, tg), 0) // CHUNK
        kc = lax.broadcasted_iota(jnp.int32, (tg, tg), 1) // CHUNK
        for h in range(n_heads):
            bc_sc[h] = jnp.where(kc <= qc, _toeplitz(u_ref[h, 0:1, :], tg, tg), NEG)
            bp_sc[h] = jnp.where(kc >= qc, _toeplitz(u_ref[h, 1:2, :], tg, tg), NEG)

    no_prev = jnp.where(g == 0, NEG, 0.0)
    for h0 in range(0, n_heads, 2):
        scores = []
        for h in (h0, h0 + 1):
            sl = slice(h * D_BAND, (h + 1) * D_BAND)
            q = q_ref[0, :, sl]
            scores.append((sl, _dot_nt(q, kp_ref[0, :, sl]) + (bp_sc[h] + no_prev),
                           _dot_nt(q, kc_ref[0, :, sl]) + bc_sc[h]))
        for sl, s_prev, s_cur in scores:
            parts = [(s_prev, vp_ref[0, :, sl]), (s_cur, vc_ref[0, :, sl])]
            o_ref[0, :, sl] = _softmax_pv(parts, bounded).astype(o_ref.dtype)


def _band(qb, kb, vb, u_tab, *, bounded):
    B, S, hw = qb.shape
    H = hw // D_BAND
    tg = BAND_PAST
    cur = lambda b, g: (b, g, 0)
    prev = lambda b, g: (b, jnp.maximum(g - 1, 0), 0)
    blk = (1, tg, hw)
    return pl.pallas_call(
        functools.partial(_band_kernel, n_heads=H, bounded=bounded),
        grid=(B, S // tg),
        in_specs=[pl.BlockSpec(blk, cur), pl.BlockSpec(blk, cur), pl.BlockSpec(blk, prev),
                  pl.BlockSpec(blk, cur), pl.BlockSpec(blk, prev),
                  pl.BlockSpec(u_tab.shape, lambda b, g: (0, 0, 0))],
        out_specs=pl.BlockSpec(blk, cur),
        out_shape=jax.ShapeDtypeStruct((B, S, hw), BF16),
        scratch_shapes=[pltpu.VMEM((H, tg, tg), F32), pltpu.VMEM((H, tg, tg), F32)],
        compiler_params=_params("arbitrary", "arbitrary"),
        name="band_bounded" if bounded else "band",
    )(qb, kb, kb, vb, vb, u_tab)


def _mla_sample_kernel(q_ref, cc_ref, pc_ref, cn_ref, pn_ref, wuk_ref, wuvt_ref, gn_ref, gr_ref,
                       aug_ref, cosc_ref, sinc_ref, cosn_ref, sinn_ref, o_ref, *, n_heads, t_new):
    args = (wuk_ref[...], wuvt_ref[...], gn_ref[...], gr_ref[...], aug_ref[...])
    kc, vtc = _expand_kv(cc_ref[0], pc_ref[0], *args, cosc_ref[...], sinc_ref[...], n_heads)
    kn, vtn = _expand_kv(cn_ref[0], pn_ref[0], *args, cosn_ref[...], sinn_ref[...], n_heads)
    rows_n = cn_ref.shape[1]
    new_ok = lax.broadcasted_iota(jnp.int32, (q_ref.shape[2], rows_n), 1) < t_new
    for h in range(n_heads):
        q = q_ref[0, h]
        s_c = _dot_nt(q, kc[h])
        s_n = jnp.where(new_ok, _dot_nt(q, kn[h]), NEG)
        m = jnp.maximum(jnp.max(s_c, axis=-1, keepdims=True), jnp.max(s_n, axis=-1, keepdims=True))
        p_c = jnp.exp2(s_c - m)
        p_n = jnp.exp2(s_n - m)
        l = jnp.sum(p_c, axis=-1, keepdims=True) + jnp.sum(p_n, axis=-1, keepdims=True)
        sl = slice(h * V_MLA, (h + 1) * V_MLA)
        pv = (_dot_nt(p_c.astype(BF16), vtc[sl].astype(BF16))
              + _dot_nt(p_n.astype(BF16), vtn[sl].astype(BF16)))
        o_ref[0, :, sl] = (pv / l).astype(o_ref.dtype)


def _mla_sample(q, ckv_c, kpe_c, ckv_n, kpe_n, w_uk, w_uvt, g_n, g_r, k_aug, tabs_c, tabs_n, *, t_new):
    Bd, H, T, _ = q.shape
    P = ckv_c.shape[1]
    rn = ckv_n.shape[1]
    const = lambda b: (0, 0)
    bat = lambda b: (b, 0, 0)
    vec = pl.BlockSpec((1, LANES), const)
    return pl.pallas_call(
        functools.partial(_mla_sample_kernel, n_heads=H, t_new=t_new),
        grid=(Bd,),
        in_specs=[pl.BlockSpec((1, H, T, QK_PAD), lambda b: (b, 0, 0, 0)),
                  pl.BlockSpec((1, P, KV_LORA), bat), pl.BlockSpec((1, P, LANES), bat),
                  pl.BlockSpec((1, rn, KV_LORA), bat), pl.BlockSpec((1, rn, LANES), bat),
                  pl.BlockSpec(w_uk.shape, const), pl.BlockSpec(w_uvt.shape, const), vec, vec, vec,
                  pl.BlockSpec((P, LANES), const), pl.BlockSpec((P, LANES), const),
                  pl.BlockSpec((rn, LANES), const), pl.BlockSpec((rn, LANES), const)],
        out_specs=pl.BlockSpec((1, T, H * V_MLA), bat),
        out_shape=jax.ShapeDtypeStruct((Bd, T, H * V_MLA), BF16),
        compiler_params=_params("arbitrary"),
        name="mla_sample",
    )(q, ckv_c, kpe_c, ckv_n, kpe_n, w_uk, w_uvt, g_n, g_r, k_aug, *tabs_c, *tabs_n)


def _band_sample_kernel(q_ref, kc_ref, vc_ref, kn_ref, vn_ref, uc_ref, un_ref, o_ref,
                        *, n_heads, t_new):
    T = q_ref.shape[1]
    lc = kc_ref.shape[2]
    rn = kn_ref.shape[1]
    new_ok = lax.broadcasted_iota(jnp.int32, (T, rn), 1) < t_new
    for h in range(n_heads):
        sl = slice(h * D_BAND, (h + 1) * D_BAND)
        q = q_ref[0, :, sl]
        s_c = _dot_nt(q, kc_ref[0, 0, :, h, :].astype(BF16)) + _toeplitz(uc_ref[h:h + 1, :], T, lc)
        s_n = jnp.where(new_ok, _dot_nt(q, kn_ref[0, :, sl]) + _toeplitz(un_ref[h:h + 1, :], T, rn), NEG)
        parts = [(s_c, vc_ref[0, 0, :, h, :].astype(BF16)), (s_n, vn_ref[0, :, sl])]
        o_ref[0, :, sl] = _softmax_pv(parts, False).astype(o_ref.dtype)


def _band_sample(q, k_c, v_c, k_n, v_n, u_c, u_n, *, t_new):
    Bd, T, hw = q.shape
    H = hw // D_BAND
    bat = lambda b: (b, 0, 0)
    const = lambda b: (0, 0)
    return pl.pallas_call(
        functools.partial(_band_sample_kernel, n_heads=H, t_new=t_new),
        grid=(Bd,),
        in_specs=[pl.BlockSpec((1, T, hw), bat),
                  pl.BlockSpec((1, 1) + k_c.shape[2:], lambda b: (0, b, 0, 0, 0)),
                  pl.BlockSpec((1, 1) + v_c.shape[2:], lambda b: (0, b, 0, 0, 0)),
                  pl.BlockSpec((1,) + k_n.shape[1:], bat), pl.BlockSpec((1,) + v_n.shape[1:], bat),
                  pl.BlockSpec(u_c.shape, const), pl.BlockSpec(u_n.shape, const)],
        out_specs=pl.BlockSpec((1, T, hw), bat),
        out_shape=jax.ShapeDtypeStruct((Bd, T, hw), BF16),
        compiler_params=_params("arbitrary"),
        name="band_sample",
    )(q, k_c, v_c, k_n, v_n, u_c, u_n)


def _merge_kernel(x_ref, oa_ref, ob_ref, woa_ref, wob_ref, h_ref):
    h_ref[0] = x_ref[0] + _dot(oa_ref[0], woa_ref[...]) + _dot(ob_ref[0], wob_ref[...])


def _merge(x, oa, ob, w_oa, w_ob, *, tm):
    B, S, D = x.shape
    row = lambda b, t: (b, t, 0)
    const = lambda b, t: (0, 0)
    return pl.pallas_call(
        _merge_kernel,
        grid=(B, S // tm),
        in_specs=[pl.BlockSpec((1, tm, D), row), pl.BlockSpec((1, tm, oa.shape[-1]), row),
                  pl.BlockSpec((1, tm, ob.shape[-1]), row),
                  pl.BlockSpec(w_oa.shape, const), pl.BlockSpec(w_ob.shape, const)],
        out_specs=pl.BlockSpec((1, tm, D), row),
        out_shape=jax.ShapeDtypeStruct((B, S, D), F32),
        compiler_params=_params("arbitrary", "arbitrary"),
        name="merge",
    )(x, oa, ob, w_oa, w_ob)


def _ffn_kernel(h_ref, g_ref, wup_ref, wdn_ref, y_ref, hn_sc, acc_sc):
    j = pl.program_id(2)

    @pl.when(j == 0)
    def _():
        h = h_ref[0]
        hn_sc[...] = _rms(h, g_ref[...]).astype(BF16)
        acc_sc[...] = h

    u = jnp.maximum(_dot(hn_sc[...], wup_ref[...]), 0.0)
    acc_sc[...] += _dot((u * u).astype(BF16), wdn_ref[...])

    @pl.when(j == pl.num_programs(2) - 1)
    def _():
        y_ref[0] = acc_sc[...]


def _ffn(h, g, w_up, w_down, *, tm, tf):
    B, S, D = h.shape
    F = w_up.shape[1]
    row = lambda b, t, j: (b, t, 0)
    return pl.pallas_call(
        _ffn_kernel,
        grid=(B, S // tm, F // tf),
        in_specs=[pl.BlockSpec((1, tm, D), row), pl.BlockSpec(g.shape, lambda b, t, j: (0, 0)),
                  pl.BlockSpec((D, tf), lambda b, t, j: (0, j)),
                  pl.BlockSpec((tf, D), lambda b, t, j: (j, 0))],
        out_specs=pl.BlockSpec((1, tm, D), row),
        out_shape=jax.ShapeDtypeStruct((B, S, D), F32),
        scratch_shapes=[pltpu.VMEM((tm, D), BF16), pltpu.VMEM((tm, D), F32)],
        compiler_params=_params("arbitrary", "arbitrary", "arbitrary"),
        name="ffn",
    )(h, g, w_up, w_down)


def _rope_tables(pos):
    inv = 1.0 / (ROPE_BASE ** (jnp.arange(0, ROPE, 2, dtype=F32) / ROPE))
    ang = pos.astype(F32)[:, None] * inv[None, :]
    c, s = jnp.cos(ang), jnp.sin(ang)
    z = jnp.zeros((pos.shape[0], LANES - ROPE), F32)
    return jnp.concatenate([c, c, z], axis=1), jnp.concatenate([-s, s, z], axis=1)


def _pad_lanes(a, width=LANES):
    return jnp.pad(a, [(0, 0)] * (a.ndim - 1) + [(0, width - a.shape[-1])])


def _pad_rows(a, rows):
    return jnp.pad(a, [(0, 0), (0, rows - a.shape[1]), (0, 0)])


def _swap_halves(a):
    return jnp.concatenate([a[..., HALF_ROPE:], a[..., :HALF_ROPE]], axis=-1)


def _prep_weights(w_in, w_uq, w_uk, w_uv, w_o, w_up, w_down):
    hb3 = (w_in.shape[1] - Q_LORA - KV_LORA - ROPE) // 3
    o = [0, Q_LORA, Q_LORA + KV_LORA, Q_LORA + KV_LORA + ROPE]
    o += [o[3] + hb3, o[3] + 2 * hb3, o[3] + 3 * hb3]
    names = ("cq", "ckv", "kpe", "qb", "kb", "vb")
    w = {n: w_in[:, o[i]:o[i + 1]].astype(BF16) for i, n in enumerate(names)}
    w["kpe"] = _pad_lanes(w["kpe"])
    ha = w_uq.shape[1] // QK_MLA
    uq = w_uq.astype(BF16).reshape(Q_LORA, ha, QK_MLA)
    nope = uq[:, :, :NOPE].reshape(Q_LORA, ha * NOPE)
    rope = uq[:, :, NOPE:]
    w["uq"] = jnp.concatenate(
        [nope, _pad_lanes(rope).reshape(Q_LORA, ha * LANES),
         _pad_lanes(_swap_halves(rope)).reshape(Q_LORA, ha * LANES)], axis=1)
    w["uk"] = w_uk.astype(BF16)
    w["uvt"] = w_uv.astype(BF16).T
    half = ha * V_MLA
    ob = w_o.astype(BF16)
    w["oa"], w["ob"] = ob[:half], ob[half:]
    w["up"], w["down"] = w_up.astype(BF16), w_down.astype(BF16)
    return w


def _softmax_setup(g_qa, g_ka, g_qb, g_kb, rel_bias):
    c_a = QK_MLA ** -0.5 * LOG2E
    c_b = D_BAND ** -0.5 * LOG2E
    amax = lambda a: jnp.max(jnp.abs(a))
    bound_a = c_a * QK_MLA * amax(g_qa) * amax(g_ka) * BOUND_MARGIN
    bound_b = c_b * D_BAND * amax(g_qb) * amax(g_kb) * BOUND_MARGIN + LOG2E * amax(rel_bias)
    fast_a = bound_a <= FAST_LIMIT
    fast_b = bound_b <= FAST_LIMIT
    shift_a = jnp.where(fast_a, bound_a, 0.0)
    shift_b = jnp.where(fast_b, bound_b, 0.0)
    lane = jnp.arange(LANES)[None, :]
    g = {"qa_n": g_qa[None, :NOPE] * c_a, "qa_r": _pad_lanes(g_qa[None, NOPE:]) * c_a,
         "qa_rs": _pad_lanes(_swap_halves(g_qa[None, NOPE:])) * c_a,
         "q_aug": (lane == AUG_LANE).astype(F32),
         "ka_n": g_ka[None, :NOPE], "ka_r": _pad_lanes(g_ka[None, NOPE:]),
         "k_aug": jnp.where(lane == AUG_LANE, -shift_a, 0.0).astype(F32),
         "qb": g_qb * c_b}
    return g, fast_a, fast_b, shift_b


def _band_vectors(rel_bias, shift):
    t = BAND_PAST
    e = np.arange(2 * t)
    e = np.where(e < t, e, e - 2 * t)
    idx = np.stack([np.clip(-e, -MAX_REL, MAX_REL), np.clip(t - e, -MAX_REL, MAX_REL)]) + MAX_REL
    return (rel_bias[:, idx] * LOG2E - shift).astype(F32)


def _band_sample_vectors(rel_bias, lb, rn):
    lc = lb + LANES
    e = np.arange(lc)
    e = np.where(e < lb, e, e - lc)
    u_c = rel_bias[:, np.clip(lb - e, -MAX_REL, MAX_REL) + MAX_REL] * LOG2E
    e = np.arange(2 * rn)
    e = np.where(e < rn, e, e - 2 * rn)
    u_n = rel_bias[:, np.clip(-e, -MAX_REL, MAX_REL) + MAX_REL] * LOG2E
    return u_c.astype(F32), u_n.astype(F32)


def _row_tile(n, pref):
    return pref if n % pref == 0 else n


def kernel(x_prompt, x_sample, cache_mla_ckv, cache_mla_kpe, cache_band_k, cache_band_v,
           norm_mix, w_in, g_cq, w_uq, g_ckv, w_uk, w_uv, g_qa, g_ka, g_qb, g_kb, rel_bias,
           w_o, norm_ffn, w_up, w_down):
    depth = w_in.shape[0]
    assert depth == 1, "single-layer step"
    B, S, D = x_prompt.shape
    Bd, T, _ = x_sample.shape
    P = cache_mla_ckv.shape[2]
    Lb = cache_band_k.shape[2]
    keep_p = min(BAND_PAST, S)
    assert S % BAND_PAST == 0 and T <= CHUNK

    w = _prep_weights(w_in[0], w_uq[0], w_uk[0], w_uv[0], w_o[0], w_up[0], w_down[0])
    g, fast_a, fast_b, shift_b = _softmax_setup(g_qa[0], g_ka[0], g_qb, g_kb, rel_bias[0])
    ha = w["uk"].shape[1] // NOPE
    hb = rel_bias.shape[1]
    hw = hb * D_BAND
    gains = (g_cq, g_ckv, g["qb"], g_kb)
    q_args = (w["uq"], g["qa_n"], g["qa_r"], g["qa_rs"], g["q_aug"])
    kv_args = (w["uk"], w["uvt"], g["ka_n"], g["ka_r"], g["k_aug"])

    cqn, ckv, kpe_pad, qb, kb, vb, kb_tail, vb_tail = _proj(
        x_prompt, norm_mix, w, *gains, keep=keep_p, tm=_row_tile(S, 256))
    tabs_p = _rope_tables(jnp.arange(S, dtype=jnp.int32))
    tmq = _row_tile(S, 512)
    q = _q_up(cqn, *q_args, *tabs_p, tm=tmq)
    k, vt = _kv_up(ckv, kpe_pad, *kv_args, *tabs_p, tm=tmq)
    tqa = _row_tile(S, 1024)
    flash = lambda bounded: functools.partial(_mla_flash, tq=tqa, td=min(tqa, 512), hg=2, bounded=bounded)
    oa = lax.cond(fast_a, flash(True), flash(False), q, k, vt)
    u_tab = _band_vectors(rel_bias[0], shift_b)
    ob = lax.cond(fast_b, functools.partial(_band, bounded=True),
                  functools.partial(_band, bounded=False), qb, kb, vb, u_tab)
    h = _merge(x_prompt, oa, ob, w["oa"], w["ob"], tm=_row_tile(S, 512))
    y_prompt = _ffn(h, norm_ffn, w["up"], w["down"], tm=_row_tile(S, 512), tf=1024)

    n_s = Bd * T
    xs = x_sample.reshape(1, n_s, D)
    cqn_s, ckv_s, kpe_pad_s, qb_s, kb_s, vb_s, kb_s32, vb_s32 = _proj(
        xs, norm_mix, w, *gains, keep=n_s, tm=_row_tile(n_s, 256))
    pos_s = P + jnp.arange(T, dtype=jnp.int32)
    rn = LANES
    q_s = _q_up(cqn_s.reshape(Bd, T, Q_LORA), *q_args, *_rope_tables(pos_s), tm=T)
    oa_s = _mla_sample(
        q_s, cache_mla_ckv[0], _pad_lanes(cache_mla_kpe[0]),
        _pad_rows(ckv_s.reshape(Bd, T, KV_LORA), rn), _pad_rows(kpe_pad_s.reshape(Bd, T, LANES), rn),
        *kv_args, _rope_tables(jnp.arange(P, dtype=jnp.int32)),
        _rope_tables(P + jnp.arange(rn, dtype=jnp.int32)), t_new=T)
    u_c, u_n = _band_sample_vectors(rel_bias[0], Lb, rn)
    ob_s = _band_sample(
        qb_s.reshape(Bd, T, hw), cache_band_k, cache_band_v,
        _pad_rows(kb_s.reshape(Bd, T, hw), rn), _pad_rows(vb_s.reshape(Bd, T, hw), rn),
        u_c, u_n, t_new=T)
    h_s = _merge(xs, oa_s.reshape(1, n_s, ha * V_MLA), ob_s.reshape(1, n_s, hw), w["oa"], w["ob"],
                 tm=n_s)
    y_sample = _ffn(h_s, norm_ffn, w["up"], w["down"], tm=n_s, tf=1024).reshape(Bd, T, D)

    return (y_prompt, y_sample,
            ckv[None], kpe_pad[None, ..., :ROPE],
            kb_tail.reshape(B, keep_p, hb, D_BAND)[None], vb_tail.reshape(B, keep_p, hb, D_BAND)[None],
            ckv_s.reshape(Bd, T, KV_LORA)[None], kpe_pad_s.reshape(Bd, T, LANES)[None, ..., :ROPE],
            kb_s32.reshape(Bd, T, hb, D_BAND)[None], vb_s32.reshape(Bd, T, hb, D_BAND)[None])
```

```python
import functools

import jax
import jax.numpy as jnp
import numpy as np
from jax import lax
from jax.experimental import pallas as pl
from jax.experimental.pallas import tpu as pltpu

CHUNK = 64
EPS = 1e-6
NOPE = 128
ROPE = 64
HALF_ROPE = ROPE // 2
QK_MLA = NOPE + ROPE
V_MLA = 128
D_BAND = 128
BAND_CHUNKS = 8
BAND_PAST = BAND_CHUNKS * CHUNK
MAX_REL = 128
ROPE_BASE = 10000.0
Q_LORA = 512
KV_LORA = 256
NEG = -1e30
LOG2E = 1.4426950408889634

BOUND_MARGIN = 1.02
FAST_LIMIT = 60.0

LANES = 128
QK_PAD = 2 * LANES
AUG_LANE = ROPE
VMEM_LIMIT = 56 * 1024 * 1024

BF16 = jnp.bfloat16
F32 = jnp.float32

NT_DIMS = (((1,), (1,)), ((), ()))


def _params(*sem):
    return pltpu.CompilerParams(dimension_semantics=sem, vmem_limit_bytes=VMEM_LIMIT)


def _resident(a):
    return pl.BlockSpec(a.shape, lambda *_: (0,) * a.ndim, pipeline_mode=pl.Buffered(1))


def _rms(x, g):
    ms = jnp.mean(x * x, axis=-1, keepdims=True)
    return x * lax.rsqrt(ms + EPS) * g


def _dot(a, b):
    return jnp.dot(a, b, preferred_element_type=F32)


def _dot_nt(a, b):
    return lax.dot_general(a, b, NT_DIMS, preferred_element_type=F32)


def _proj_kernel(x_ref, nm_ref, wcq_ref, wckv_ref, wkpe_ref, wqb_ref, wkb_ref, wvb_ref,
                 gcq_ref, gckv_ref, gqb_ref, gkb_ref,
                 cqn_ref, ckv_ref, kpe_ref, qb_ref, kb_ref, vb_ref, kbt_ref, vbt_ref,
                 *, n_heads, tail_start):
    t = pl.program_id(1)
    xn = _rms(x_ref[0], nm_ref[...]).astype(BF16)
    cqn_ref[0] = _rms(_dot(xn, wcq_ref[...]), gcq_ref[...]).astype(BF16)
    ckv_ref[0] = _rms(_dot(xn, wckv_ref[...]), gckv_ref[...])
    kpe_ref[0] = _dot(xn, wkpe_ref[...])
    qb = _dot(xn, wqb_ref[...])
    kb = _dot(xn, wkb_ref[...])
    vb = _dot(xn, wvb_ref[...])
    vb_ref[0] = vb.astype(BF16)
    kbn = []
    for h in range(n_heads):
        sl = slice(h * D_BAND, (h + 1) * D_BAND)
        qb_ref[0, :, sl] = _rms(qb[:, sl], gqb_ref[...]).astype(BF16)
        kbn.append(_rms(kb[:, sl], gkb_ref[...]))
    kbn = jnp.concatenate(kbn, axis=-1)
    kb_ref[0] = kbn.astype(BF16)

    @pl.when(t >= tail_start)
    def _():
        kbt_ref[0] = kbn
        vbt_ref[0] = vb


def _proj(x, norm_mix, w, g_cq, g_ckv, g_qb, g_kb, *, keep, tm):
    B, S, D = x.shape
    n_heads = w["qb"].shape[1] // D_BAND
    hw = n_heads * D_BAND
    nt = S // tm
    tail_start = nt - keep // tm
    row = lambda b, t: (b, t, 0)
    tail = lambda b, t: (b, jnp.maximum(t - tail_start, 0), 0)
    out_shapes = (
        jax.ShapeDtypeStruct((B, S, Q_LORA), BF16),
        jax.ShapeDtypeStruct((B, S, KV_LORA), F32),
        jax.ShapeDtypeStruct((B, S, LANES), F32),
        jax.ShapeDtypeStruct((B, S, hw), BF16),
        jax.ShapeDtypeStruct((B, S, hw), BF16),
        jax.ShapeDtypeStruct((B, S, hw), BF16),
        jax.ShapeDtypeStruct((B, keep, hw), F32),
        jax.ShapeDtypeStruct((B, keep, hw), F32),
    )
    out_specs = (
        pl.BlockSpec((1, tm, Q_LORA), row),
        pl.BlockSpec((1, tm, KV_LORA), row),
        pl.BlockSpec((1, tm, LANES), row),
        pl.BlockSpec((1, tm, hw), row),
        pl.BlockSpec((1, tm, hw), row),
        pl.BlockSpec((1, tm, hw), row),
        pl.BlockSpec((1, tm, hw), tail),
        pl.BlockSpec((1, tm, hw), tail),
    )
    weights = (w["cq"], w["ckv"], w["kpe"], w["qb"], w["kb"], w["vb"])
    gains = (g_cq, g_ckv, g_qb, g_kb)
    return pl.pallas_call(
        functools.partial(_proj_kernel, n_heads=n_heads, tail_start=tail_start),
        grid=(B, nt),
        in_specs=[pl.BlockSpec((1, tm, D), row), _resident(norm_mix)]
        + [_resident(a) for a in weights] + [_resident(a) for a in gains],
        out_specs=out_specs,
        out_shape=out_shapes,
        compiler_params=_params("arbitrary", "arbitrary"),
        name="proj",
    )(x, norm_mix, *weights, *gains)


def _q_up_kernel(cqn_ref, w_ref, gn_ref, gr_ref, grs_ref, aug_ref, cos_ref, sin_ref, q_ref,
                 *, n_heads):
    hw = n_heads * LANES
    y = _dot(cqn_ref[0], w_ref[...])
    cos = cos_ref[...]
    sin = sin_ref[...]
    for h in range(n_heads):
        sl = slice(h * LANES, (h + 1) * LANES)
        n = y[:, sl]
        r = y[:, hw + h * LANES: hw + (h + 1) * LANES]
        rs = y[:, 2 * hw + h * LANES: 2 * hw + (h + 1) * LANES]
        ss = jnp.sum(n * n, axis=-1, keepdims=True) + jnp.sum(r * r, axis=-1, keepdims=True)
        rinv = lax.rsqrt(ss * (1.0 / QK_MLA) + EPS)
        q_ref[0, h, :, 0:LANES] = (n * rinv * gn_ref[...]).astype(BF16)
        rot = (r * rinv * gr_ref[...]) * cos + (rs * rinv * grs_ref[...]) * sin
        q_ref[0, h, :, LANES:QK_PAD] = (rot + aug_ref[...]).astype(BF16)


def _q_up(cqn, w, g_n, g_r, g_rs, aug, cos, sin, *, tm):
    B, S, _ = cqn.shape
    n_heads = w.shape[1] // (3 * LANES)
    row = lambda b, t: (b, t, 0)
    pos = lambda b, t: (t, 0)
    return pl.pallas_call(
        functools.partial(_q_up_kernel, n_heads=n_heads),
        grid=(B, S // tm),
        in_specs=[pl.BlockSpec((1, tm, Q_LORA), row), _resident(w),
                  _resident(g_n), _resident(g_r), _resident(g_rs), _resident(aug),
                  pl.BlockSpec((tm, LANES), pos), pl.BlockSpec((tm, LANES), pos)],
        out_specs=pl.BlockSpec((1, n_heads, tm, QK_PAD), lambda b, t: (b, 0, t, 0)),
        out_shape=jax.ShapeDtypeStruct((B, n_heads, S, QK_PAD), BF16),
        compiler_params=_params("arbitrary", "arbitrary"),
        name="q_up",
    )(cqn, w, g_n, g_r, g_rs, aug, cos, sin)


def _expand_kv(c, kpe, wuk, wuvt, g_n, g_r, k_aug, cos, sin, n_heads):
    cb = c.astype(BF16)
    kn = _dot(cb, wuk)
    vt = _dot_nt(wuvt, cb)
    a = kpe * g_r
    sspe = jnp.sum(kpe * kpe, axis=-1, keepdims=True)
    a_sw = pltpu.roll(a, HALF_ROPE, 1) + pltpu.roll(a, LANES - HALF_ROPE, 1)
    rot = a * cos + a_sw * sin
    keys = []
    for h in range(n_heads):
        n = kn[:, h * NOPE:(h + 1) * NOPE]
        ss = jnp.sum(n * n, axis=-1, keepdims=True) + sspe
        rinv = lax.rsqrt(ss * (1.0 / QK_MLA) + EPS)
        keys.append(jnp.concatenate([(n * rinv * g_n).astype(BF16),
                                     (rot * rinv + k_aug).astype(BF16)], axis=-1))
    return keys, vt


def _kv_up_kernel(ckv_ref, kpe_ref, wuk_ref, wuvt_ref, gn_ref, gr_ref, aug_ref, cos_ref, sin_ref,
                  k_ref, vt_ref, *, n_heads):
    keys, vt = _expand_kv(ckv_ref[0], kpe_ref[0], wuk_ref[...], wuvt_ref[...], gn_ref[...],
                          gr_ref[...], aug_ref[...], cos_ref[...], sin_ref[...], n_heads)
    for h in range(n_heads):
        k_ref[0, h] = keys[h]
        vt_ref[0, h] = vt[h * V_MLA:(h + 1) * V_MLA, :].astype(BF16)


def _kv_up(ckv, kpe_pad, w_uk, w_uvt, g_n, g_r, k_aug, cos, sin, *, tm):
    B, S, _ = ckv.shape
    n_heads = w_uk.shape[1] // NOPE
    row = lambda b, t: (b, t, 0)
    pos = lambda b, t: (t, 0)
    return pl.pallas_call(
        functools.partial(_kv_up_kernel, n_heads=n_heads),
        grid=(B, S // tm),
        in_specs=[pl.BlockSpec((1, tm, KV_LORA), row), pl.BlockSpec((1, tm, LANES), row),
                  _resident(w_uk), _resident(w_uvt),
                  _resident(g_n), _resident(g_r), _resident(k_aug),
                  pl.BlockSpec((tm, LANES), pos), pl.BlockSpec((tm, LANES), pos)],
        out_specs=(pl.BlockSpec((1, n_heads, tm, QK_PAD), lambda b, t: (b, 0, t, 0)),
                   pl.BlockSpec((1, n_heads, V_MLA, tm), lambda b, t: (b, 0, 0, t))),
        out_shape=(jax.ShapeDtypeStruct((B, n_heads, S, QK_PAD), BF16),
                   jax.ShapeDtypeStruct((B, n_heads, V_MLA, S), BF16)),
        compiler_params=_params("arbitrary", "arbitrary"),
        name="kv_up",
    )(ckv, kpe_pad, w_uk, w_uvt, g_n, g_r, k_aug, cos, sin)


def _mla_flash_kernel(q_ref, k_ref, vt_ref, o_ref, m_sc, l_sc, acc_sc, *, tq, td, hg, bounded):
    i = pl.program_id(2)
    if not bounded:
        m_sc[...] = jnp.full(m_sc.shape, -jnp.inf, F32)
    l_sc[...] = jnp.zeros(l_sc.shape, F32)
    acc_sc[...] = jnp.zeros(acc_sc.shape, F32)

    def step(tiles):
        sts = [(hh, t, _dot_nt(k_ref[0, hh, pl.ds(t[0], t[1]), :], q_ref[0, hh, t[2]:, :]))
               for t in tiles for hh in range(hg)]
        for hh, (k0, kn, q0, mask), st in sts:
            if mask is not None:
                st = jnp.where(mask, st, NEG)
            vt = vt_ref[0, hh, :, pl.ds(k0, kn)]
            if bounded:
                p = jnp.exp2(st)
                l_sc[hh, :, q0:] += jnp.sum(p, axis=0, keepdims=True)
                acc_sc[hh, :, q0:] += _dot(vt, p.astype(BF16))
            else:
                m_prev = m_sc[hh, :, q0:]
                m_new = jnp.maximum(m_prev, jnp.max(st, axis=0, keepdims=True))
                alpha = jnp.exp2(m_prev - m_new)
                p = jnp.exp2(st - m_new)
                l_sc[hh, :, q0:] = alpha * l_sc[hh, :, q0:] + jnp.sum(p, axis=0, keepdims=True)
                acc_sc[hh, :, q0:] = acc_sc[hh, :, q0:] * alpha + _dot(vt, p.astype(BF16))
                m_sc[hh, :, q0:] = m_new

    def body(j, carry):
        step([(pl.multiple_of(j * tq, tq), tq, 0, None)])
        return carry

    lax.fori_loop(0, i, body, 0)
    diag = []
    for d in range(tq // td):
        nq = tq - d * td
        q_chunk = lax.broadcasted_iota(jnp.int32, (td, nq), 1) // CHUNK
        k_chunk = lax.broadcasted_iota(jnp.int32, (td, nq), 0) // CHUNK
        diag.append((pl.multiple_of(i * tq + d * td, td), td, d * td, k_chunk <= q_chunk))
    step(diag)
    for hh in range(hg):
        o_ref[0, :, hh * V_MLA:(hh + 1) * V_MLA] = (acc_sc[hh] / l_sc[hh]).T.astype(o_ref.dtype)


def _mla_flash(q, k, vt, *, tq, td, hg, bounded):
    B, H, S, _ = q.shape
    return pl.pallas_call(
        functools.partial(_mla_flash_kernel, tq=tq, td=td, hg=hg, bounded=bounded),
        grid=(B, H // hg, S // tq),
        in_specs=[pl.BlockSpec((1, hg, tq, QK_PAD), lambda b, h, i: (b, h, i, 0)),
                  pl.BlockSpec((1, hg, S, QK_PAD), lambda b, h, i: (b, h, 0, 0)),
                  pl.BlockSpec((1, hg, V_MLA, S), lambda b, h, i: (b, h, 0, 0))],
        out_specs=pl.BlockSpec((1, tq, hg * V_MLA), lambda b, h, i: (b, i, h)),
        out_shape=jax.ShapeDtypeStruct((B, S, H * V_MLA), BF16),
        scratch_shapes=[pltpu.VMEM((hg, 1, tq), F32), pltpu.VMEM((hg, 1, tq), F32),
                        pltpu.VMEM((hg, V_MLA, tq), F32)],
        compiler_params=_params("arbitrary", "arbitrary", "arbitrary"),
        name="mla_flash_bounded" if bounded else "mla_flash",
    )(q, k, vt)


def _toeplitz(u_row, rows, cols, stride=1):
    u = jnp.broadcast_to(u_row, (rows, u_row.shape[-1]))
    return pltpu.roll(u, 0, 1, stride=stride, stride_axis=0)[:, :cols]


def _softmax_pv(parts, bounded):
    m = None
    if not bounded:
        for s, _ in parts:
            mx = jnp.max(s, axis=-1, keepdims=True)
            m = mx if m is None else jnp.maximum(m, mx)
    l = None
    acc = None
    for s, v in parts:
        p = jnp.exp2(s if bounded else s - m)
        ps = jnp.sum(p, axis=-1, keepdims=True)
        pv = _dot(p.astype(BF16), v)
        l = ps if l is None else l + ps
        acc = pv if acc is None else acc + pv
    return acc / l


def _band_kernel(q_ref, kc_ref, kp_ref, vc_ref, vp_ref, u_ref, o_ref, bc_sc, bp_sc, *, n_heads, bounded):
    b, g = pl.program_id(0), pl.program_id(1)
    tg = bc_sc.shape[1]

    @pl.when((b == 0) & (g == 0))
    def _():
        qc = lax.broadcasted_iota(jnp.int32, (tg, tg), 0) // CHUNK
        kc = lax.broadcasted_iota(jnp.int32, (tg, tg), 1) // CHUNK
        for h in range(n_heads):
            bc_sc[h] = jnp.where(kc <= qc, _toeplitz(u_ref[h, 0:1, :], tg, tg), NEG)
            bp_sc[h] = jnp.where(kc >= qc, _toeplitz(u_ref[h, 1:2, :], tg, tg), NEG)

    no_prev = jnp.where(g == 0, NEG, 0.0)
    for h0 in range(0, n_heads, 2):
        scores = []
        for h in (h0, h0 + 1):
            sl = slice(h * D_BAND, (h + 1) * D_BAND)
            q = q_ref[0, :, sl]
            scores.append((sl, _dot_nt(q, kp_ref[0, :, sl]) + (bp_sc[h] + no_prev),
                           _dot_nt(q, kc_ref[0, :, sl]) + bc_sc[h]))
        for sl, s_prev, s_cur in scores:
            parts = [(s_prev, vp_ref[0, :, sl]), (s_cur, vc_ref[0, :, sl])]
            o_ref[0, :, sl] = _softmax_pv(parts, bounded).astype(o_ref.dtype)


def _band(qb, kb, vb, u_tab, *, bounded):
    B, S, hw = qb.shape
    H = hw // D_BAND
    tg = BAND_PAST
    cur = lambda b, g: (b, g, 0)
    prev = lambda b, g: (b, jnp.maximum(g - 1, 0), 0)
    blk = (1, tg, hw)
    return pl.pallas_call(
        functools.partial(_band_kernel, n_heads=H, bounded=bounded),
        grid=(B, S // tg),
        in_specs=[pl.BlockSpec(blk, cur), pl.BlockSpec(blk, cur), pl.BlockSpec(blk, prev),
                  pl.BlockSpec(blk, cur), pl.BlockSpec(blk, prev),
                  _resident(u_tab)],
        out_specs=pl.BlockSpec(blk, cur),
        out_shape=jax.ShapeDtypeStruct((B, S, hw), BF16),
        scratch_shapes=[pltpu.VMEM((H, tg, tg), F32), pltpu.VMEM((H, tg, tg), F32)],
        compiler_params=_params("arbitrary", "arbitrary"),
        name="band_bounded" if bounded else "band",
    )(qb, kb, kb, vb, vb, u_tab)


def _mla_sample_kernel(q_ref, cc_ref, pc_ref, cn_ref, pn_ref, wuk_ref, wuvt_ref, gn_ref, gr_ref,
                       aug_ref, cosc_ref, sinc_ref, cosn_ref, sinn_ref, o_ref, *, n_heads, t_new):
    args = (wuk_ref[...], wuvt_ref[...], gn_ref[...], gr_ref[...], aug_ref[...])
    kc, vtc = _expand_kv(cc_ref[0], pc_ref[0], *args, cosc_ref[...], sinc_ref[...], n_heads)
    kn, vtn = _expand_kv(cn_ref[0], pn_ref[0], *args, cosn_ref[...], sinn_ref[...], n_heads)
    rows_n = cn_ref.shape[1]
    new_ok = lax.broadcasted_iota(jnp.int32, (q_ref.shape[2], rows_n), 1) < t_new
    scores = [(_dot_nt(q_ref[0, h], kc[h]), jnp.where(new_ok, _dot_nt(q_ref[0, h], kn[h]), NEG))
              for h in range(n_heads)]
    for h, (s_c, s_n) in enumerate(scores):
        m = jnp.maximum(jnp.max(s_c, axis=-1, keepdims=True), jnp.max(s_n, axis=-1, keepdims=True))
        p_c = jnp.exp2(s_c - m)
        p_n = jnp.exp2(s_n - m)
        l = jnp.sum(p_c, axis=-1, keepdims=True) + jnp.sum(p_n, axis=-1, keepdims=True)
        sl = slice(h * V_MLA, (h + 1) * V_MLA)
        pv = (_dot_nt(p_c.astype(BF16), vtc[sl].astype(BF16))
              + _dot_nt(p_n.astype(BF16), vtn[sl].astype(BF16)))
        o_ref[0, :, sl] = (pv / l).astype(o_ref.dtype)


def _mla_sample(q, ckv_c, kpe_c, ckv_n, kpe_n, w_uk, w_uvt, g_n, g_r, k_aug, tabs_c, tabs_n, *, t_new):
    Bd, H, T, _ = q.shape
    P = ckv_c.shape[1]
    rn = ckv_n.shape[1]
    bat = lambda b: (b, 0, 0)
    return pl.pallas_call(
        functools.partial(_mla_sample_kernel, n_heads=H, t_new=t_new),
        grid=(Bd,),
        in_specs=[pl.BlockSpec((1, H, T, QK_PAD), lambda b: (b, 0, 0, 0)),
                  pl.BlockSpec((1, P, KV_LORA), bat), pl.BlockSpec((1, P, LANES), bat),
                  pl.BlockSpec((1, rn, KV_LORA), bat), pl.BlockSpec((1, rn, LANES), bat),
                  _resident(w_uk), _resident(w_uvt), _resident(g_n), _resident(g_r), _resident(k_aug)]
        + [_resident(t) for t in (*tabs_c, *tabs_n)],
        out_specs=pl.BlockSpec((1, T, H * V_MLA), bat),
        out_shape=jax.ShapeDtypeStruct((Bd, T, H * V_MLA), BF16),
        compiler_params=_params("arbitrary"),
        name="mla_sample",
    )(q, ckv_c, kpe_c, ckv_n, kpe_n, w_uk, w_uvt, g_n, g_r, k_aug, *tabs_c, *tabs_n)


def _band_sample_kernel(q_ref, kc_ref, vc_ref, kn_ref, vn_ref, wc_ref, wn_ref, o_ref, *, n_heads):
    q = q_ref[0]
    T = q.shape[0] // n_heads
    tab = lambda w_ref, cols: jnp.concatenate(
        [_toeplitz(w_ref[h:h + 1, :], T, cols, stride=n_heads) for h in range(n_heads)], axis=0)
    s_c = _dot_nt(q, kc_ref[0].astype(BF16)) + tab(wc_ref, kc_ref.shape[1])
    s_n = _dot_nt(q, kn_ref[0]) + tab(wn_ref, kn_ref.shape[1])
    parts = [(s_c, vc_ref[0].astype(BF16)), (s_n, vn_ref[0])]
    o_ref[0] = _softmax_pv(parts, False).astype(o_ref.dtype)


def _band_sample(q, k_c, v_c, k_n, v_n, w_c, w_n, *, n_heads):
    Bd = q.shape[0]
    bat = lambda b: (b, 0, 0)
    blk = lambda a: pl.BlockSpec((1,) + a.shape[1:], bat)
    return pl.pallas_call(
        functools.partial(_band_sample_kernel, n_heads=n_heads),
        grid=(Bd,),
        in_specs=[blk(q), blk(k_c), blk(v_c), blk(k_n), blk(v_n), _resident(w_c), _resident(w_n)],
        out_specs=blk(q),
        out_shape=jax.ShapeDtypeStruct(q.shape, BF16),
        compiler_params=_params("arbitrary"),
        name="band_sample",
    )(q, k_c, v_c, k_n, v_n, w_c, w_n)


def _merge_kernel(x_ref, oa_ref, ob_ref, woa_ref, wob_ref, h_ref):
    h_ref[0] = x_ref[0] + _dot(oa_ref[0], woa_ref[...]) + _dot(ob_ref[0], wob_ref[...])


def _merge(x, oa, ob, w_oa, w_ob, *, tm):
    B, S, D = x.shape
    row = lambda b, t: (b, t, 0)
    return pl.pallas_call(
        _merge_kernel,
        grid=(B, S // tm),
        in_specs=[pl.BlockSpec((1, tm, D), row), pl.BlockSpec((1, tm, oa.shape[-1]), row),
                  pl.BlockSpec((1, tm, ob.shape[-1]), row),
                  _resident(w_oa), _resident(w_ob)],
        out_specs=pl.BlockSpec((1, tm, D), row),
        out_shape=jax.ShapeDtypeStruct((B, S, D), F32),
        compiler_params=_params("arbitrary", "arbitrary"),
        name="merge",
    )(x, oa, ob, w_oa, w_ob)


def _ffn_kernel(h_ref, g_ref, wup_ref, wdn_ref, y_ref, hn_sc):
    j = pl.program_id(2)

    @pl.when(j == 0)
    def _():
        h = h_ref[0]
        hn_sc[...] = _rms(h, g_ref[...]).astype(BF16)
        y_ref[0] = h

    u = jnp.maximum(_dot(hn_sc[...], wup_ref[...]), 0.0)
    y_ref[0] += _dot((u * u).astype(BF16), wdn_ref[...])


def _ffn(h, g, w_up, w_down, *, tm, tf):
    B, S, D = h.shape
    F = w_up.shape[1]
    row = lambda b, t, j: (b, t, 0)
    return pl.pallas_call(
        _ffn_kernel,
        grid=(B, S // tm, F // tf),
        in_specs=[pl.BlockSpec((1, tm, D), row), _resident(g),
                  pl.BlockSpec((D, tf), lambda b, t, j: (0, j)),
                  pl.BlockSpec((tf, D), lambda b, t, j: (j, 0))],
        out_specs=pl.BlockSpec((1, tm, D), row),
        out_shape=jax.ShapeDtypeStruct((B, S, D), F32),
        scratch_shapes=[pltpu.VMEM((tm, D), BF16)],
        compiler_params=_params("arbitrary", "arbitrary", "arbitrary"),
        name="ffn",
    )(h, g, w_up, w_down)


def _rope_tables(pos):
    inv = 1.0 / (ROPE_BASE ** (jnp.arange(0, ROPE, 2, dtype=F32) / ROPE))
    ang = pos.astype(F32)[:, None] * inv[None, :]
    c, s = jnp.cos(ang), jnp.sin(ang)
    z = jnp.zeros((pos.shape[0], LANES - ROPE), F32)
    return jnp.concatenate([c, c, z], axis=1), jnp.concatenate([-s, s, z], axis=1)


def _pad_lanes(a, width=LANES):
    return jnp.pad(a, [(0, 0)] * (a.ndim - 1) + [(0, width - a.shape[-1])])


def _pad_rows(a, rows):
    return jnp.pad(a, [(0, 0), (0, rows - a.shape[1]), (0, 0)])


def _swap_halves(a):
    return jnp.concatenate([a[..., HALF_ROPE:], a[..., :HALF_ROPE]], axis=-1)


def _prep_weights(w_in, w_uq, w_uk, w_uv, w_o, w_up, w_down):
    hb3 = (w_in.shape[1] - Q_LORA - KV_LORA - ROPE) // 3
    o = [0, Q_LORA, Q_LORA + KV_LORA, Q_LORA + KV_LORA + ROPE]
    o += [o[3] + hb3, o[3] + 2 * hb3, o[3] + 3 * hb3]
    names = ("cq", "ckv", "kpe", "qb", "kb", "vb")
    w = {n: w_in[:, o[i]:o[i + 1]].astype(BF16) for i, n in enumerate(names)}
    w["kpe"] = _pad_lanes(w["kpe"])
    ha = w_uq.shape[1] // QK_MLA
    uq = w_uq.astype(BF16).reshape(Q_LORA, ha, QK_MLA)
    nope = uq[:, :, :NOPE].reshape(Q_LORA, ha * NOPE)
    rope = uq[:, :, NOPE:]
    w["uq"] = jnp.concatenate(
        [nope, _pad_lanes(rope).reshape(Q_LORA, ha * LANES),
         _pad_lanes(_swap_halves(rope)).reshape(Q_LORA, ha * LANES)], axis=1)
    w["uk"] = w_uk.astype(BF16)
    w["uvt"] = w_uv.astype(BF16).T
    half = ha * V_MLA
    ob = w_o.astype(BF16)
    w["oa"], w["ob"] = ob[:half], ob[half:]
    w["up"], w["down"] = w_up.astype(BF16), w_down.astype(BF16)
    return w


def _softmax_setup(g_qa, g_ka, g_qb, g_kb, rel_bias):
    c_a = QK_MLA ** -0.5 * LOG2E
    c_b = D_BAND ** -0.5 * LOG2E
    amax = lambda a: jnp.max(jnp.abs(a))
    bound_a = c_a * QK_MLA * amax(g_qa) * amax(g_ka) * BOUND_MARGIN
    bound_b = c_b * D_BAND * amax(g_qb) * amax(g_kb) * BOUND_MARGIN + LOG2E * amax(rel_bias)
    fast_a = bound_a <= FAST_LIMIT
    fast_b = bound_b <= FAST_LIMIT
    shift_a = jnp.where(fast_a, bound_a, 0.0)
    shift_b = jnp.where(fast_b, bound_b, 0.0)
    lane = jnp.arange(LANES)[None, :]
    g = {"qa_n": g_qa[None, :NOPE] * c_a, "qa_r": _pad_lanes(g_qa[None, NOPE:]) * c_a,
         "qa_rs": _pad_lanes(_swap_halves(g_qa[None, NOPE:])) * c_a,
         "q_aug": (lane == AUG_LANE).astype(F32),
         "ka_n": g_ka[None, :NOPE], "ka_r": _pad_lanes(g_ka[None, NOPE:]),
         "k_aug": jnp.where(lane == AUG_LANE, -shift_a, 0.0).astype(F32),
         "qb": g_qb * c_b}
    return g, fast_a, fast_b, shift_b


def _band_vectors(rel_bias, shift):
    t = BAND_PAST
    e = np.arange(2 * t)
    e = np.where(e < t, e, e - 2 * t)
    idx = np.stack([np.clip(-e, -MAX_REL, MAX_REL), np.clip(t - e, -MAX_REL, MAX_REL)]) + MAX_REL
    return (rel_bias[:, idx] * LOG2E - shift).astype(F32)


def _band_sample_vectors(rel_bias, lb):
    H = rel_bias.shape[0]

    def interleave(dist, n_pos, n):
        e = np.arange(n)
        e = np.where(e < n_pos, e, e - n)
        u = rel_bias[:, np.clip(dist - e, -MAX_REL, MAX_REL) + MAX_REL] * LOG2E
        own = np.arange(H)[:, None, None] == np.arange(H)[None, None, :]
        return jnp.where(own, u[:, :, None], NEG).reshape(H, n * H).astype(F32)

    return interleave(lb, lb, lb + LANES), interleave(0, CHUNK, 2 * CHUNK)


def _row_tile(n, pref):
    return pref if n % pref == 0 else n


def kernel(x_prompt, x_sample, cache_mla_ckv, cache_mla_kpe, cache_band_k, cache_band_v,
           norm_mix, w_in, g_cq, w_uq, g_ckv, w_uk, w_uv, g_qa, g_ka, g_qb, g_kb, rel_bias,
           w_o, norm_ffn, w_up, w_down):
    depth = w_in.shape[0]
    assert depth == 1, "single-layer step"
    B, S, D = x_prompt.shape
    Bd, T, _ = x_sample.shape
    P = cache_mla_ckv.shape[2]
    Lb = cache_band_k.shape[2]
    keep_p = min(BAND_PAST, S)
    assert S % BAND_PAST == 0 and T <= CHUNK

    w = _prep_weights(w_in[0], w_uq[0], w_uk[0], w_uv[0], w_o[0], w_up[0], w_down[0])
    g, fast_a, fast_b, shift_b = _softmax_setup(g_qa[0], g_ka[0], g_qb, g_kb, rel_bias[0])
    ha = w["uk"].shape[1] // NOPE
    hb = rel_bias.shape[1]
    hw = hb * D_BAND
    gains = (g_cq, g_ckv, g["qb"], g_kb)
    q_args = (w["uq"], g["qa_n"], g["qa_r"], g["qa_rs"], g["q_aug"])
    kv_args = (w["uk"], w["uvt"], g["ka_n"], g["ka_r"], g["k_aug"])

    cqn, ckv, kpe_pad, qb, kb, vb, kb_tail, vb_tail = _proj(
        x_prompt, norm_mix, w, *gains, keep=keep_p, tm=_row_tile(S, 512))
    rn = LANES
    tabs = _rope_tables(jnp.arange(max(S, P + rn), dtype=jnp.int32))
    tabs_p = tuple(t[:S] for t in tabs)
    tmq = _row_tile(S, 512)
    q = _q_up(cqn, *q_args, *tabs_p, tm=tmq)
    k, vt = _kv_up(ckv, kpe_pad, *kv_args, *tabs_p, tm=_row_tile(S, 1024))
    tqa = _row_tile(S, 1024)
    flash = lambda bounded: functools.partial(_mla_flash, tq=tqa, td=min(tqa, 512), hg=2, bounded=bounded)
    oa = lax.cond(fast_a, flash(True), flash(False), q, k, vt)
    u_tab = _band_vectors(rel_bias[0], shift_b)
    ob = lax.cond(fast_b, functools.partial(_band, bounded=True),
                  functools.partial(_band, bounded=False), qb, kb, vb, u_tab)
    h = _merge(x_prompt, oa, ob, w["oa"], w["ob"], tm=_row_tile(S, 512))
    y_prompt = _ffn(h, norm_ffn, w["up"], w["down"], tm=_row_tile(S, 512), tf=2048)

    n_s = Bd * T
    xs = x_sample.reshape(1, n_s, D)
    cqn_s, ckv_s, kpe_pad_s, qb_s, kb_s, vb_s, kb_s32, vb_s32 = _proj(
        xs, norm_mix, w, *gains, keep=n_s, tm=_row_tile(n_s, 256))
    q_s = _q_up(cqn_s.reshape(Bd, T, Q_LORA), *q_args, *(t[P:P + T] for t in tabs), tm=T)
    oa_s = _mla_sample(
        q_s, cache_mla_ckv[0], _pad_lanes(cache_mla_kpe[0]),
        _pad_rows(ckv_s.reshape(Bd, T, KV_LORA), rn), _pad_rows(kpe_pad_s.reshape(Bd, T, LANES), rn),
        *kv_args, tuple(t[:P] for t in tabs), tuple(t[P:P + rn] for t in tabs), t_new=T)
    w_c, w_n = _band_sample_vectors(rel_bias[0], Lb)
    head_major = lambda a: a.reshape(Bd, T, hb, D_BAND).transpose(0, 2, 1, 3).reshape(Bd, hb * T, D_BAND)
    ob_s = _band_sample(
        head_major(qb_s), cache_band_k.reshape(Bd, Lb * hb, D_BAND), cache_band_v.reshape(Bd, Lb * hb, D_BAND),
        kb_s.reshape(Bd, T * hb, D_BAND), vb_s.reshape(Bd, T * hb, D_BAND), w_c, w_n, n_heads=hb)
    ob_s = ob_s.reshape(Bd, hb, T, D_BAND).transpose(0, 2, 1, 3)
    h_s = _merge(xs, oa_s.reshape(1, n_s, ha * V_MLA), ob_s.reshape(1, n_s, hw), w["oa"], w["ob"],
                 tm=n_s)
    y_sample = _ffn(h_s, norm_ffn, w["up"], w["down"], tm=n_s, tf=2048).reshape(Bd, T, D)

    return (y_prompt, y_sample,
            ckv[None], kpe_pad[None, ..., :ROPE],
            kb_tail.reshape(B, keep_p, hb, D_BAND)[None], vb_tail.reshape(B, keep_p, hb, D_BAND)[None],
            ckv_s.reshape(Bd, T, KV_LORA)[None], kpe_pad_s.reshape(Bd, T, LANES)[None, ..., :ROPE],
            kb_s32.reshape(Bd, T, hb, D_BAND)[None], vb_s32.reshape(Bd, T, hb, D_BAND)[None])
```

```python
import functools

import jax
import jax.numpy as jnp
import numpy as np
from jax import lax
from jax.experimental import pallas as pl
from jax.experimental.pallas import tpu as pltpu

CHUNK = 64
EPS = 1e-6
NOPE = 128
ROPE = 64
HALF_ROPE = ROPE // 2
QK_MLA = NOPE + ROPE
V_MLA = 128
D_BAND = 128
BAND_CHUNKS = 8
BAND_PAST = BAND_CHUNKS * CHUNK
MAX_REL = 128
ROPE_BASE = 10000.0
Q_LORA = 512
KV_LORA = 256
NEG = -1e30
LOG2E = 1.4426950408889634

BOUND_MARGIN = 1.02
FAST_LIMIT = 60.0

LANES = 128
QK_PAD = 2 * LANES
AUG_LANE = ROPE
VMEM_LIMIT = 56 * 1024 * 1024

BF16 = jnp.bfloat16
F32 = jnp.float32

NT_DIMS = (((1,), (1,)), ((), ()))


def _params(*sem):
    return pltpu.CompilerParams(dimension_semantics=sem, vmem_limit_bytes=VMEM_LIMIT)


def _resident(a):
    return pl.BlockSpec(a.shape, lambda *_: (0,) * a.ndim, pipeline_mode=pl.Buffered(1))


def _rms(x, g):
    ms = jnp.mean(x * x, axis=-1, keepdims=True)
    return x * lax.rsqrt(ms + EPS) * g


def _dot(a, b):
    return jnp.dot(a, b, preferred_element_type=F32)


def _dot_nt(a, b):
    return lax.dot_general(a, b, NT_DIMS, preferred_element_type=F32)


def _proj_kernel(x_ref, nm_ref, wcq_ref, wckv_ref, wkpe_ref, wqb_ref, wkb_ref, wvb_ref,
                 gcq_ref, gckv_ref, gqb_ref, gkb_ref,
                 cqn_ref, ckv_ref, kpe_ref, qb_ref, kb_ref, vb_ref, kbt_ref, vbt_ref,
                 *, n_heads, tail_start):
    t = pl.program_id(1)
    xn = _rms(x_ref[0], nm_ref[...]).astype(BF16)
    cqn_ref[0] = _rms(_dot(xn, wcq_ref[...]), gcq_ref[...]).astype(BF16)
    ckv_ref[0] = _rms(_dot(xn, wckv_ref[...]), gckv_ref[...])
    kpe_ref[0] = _dot(xn, wkpe_ref[...])
    qb = _dot(xn, wqb_ref[...])
    kb = _dot(xn, wkb_ref[...])
    vb = _dot(xn, wvb_ref[...])
    vb_ref[0] = vb.astype(BF16)
    kbn = []
    for h in range(n_heads):
        sl = slice(h * D_BAND, (h + 1) * D_BAND)
        qb_ref[0, :, sl] = _rms(qb[:, sl], gqb_ref[...]).astype(BF16)
        kbn.append(_rms(kb[:, sl], gkb_ref[...]))
    kbn = jnp.concatenate(kbn, axis=-1)
    kb_ref[0] = kbn.astype(BF16)

    @pl.when(t >= tail_start)
    def _():
        kbt_ref[0] = kbn
        vbt_ref[0] = vb


def _proj(x, norm_mix, w, g_cq, g_ckv, g_qb, g_kb, *, keep, tm):
    B, S, D = x.shape
    n_heads = w["qb"].shape[1] // D_BAND
    hw = n_heads * D_BAND
    nt = S // tm
    tail_start = nt - keep // tm
    row = lambda b, t: (b, t, 0)
    tail = lambda b, t: (b, jnp.maximum(t - tail_start, 0), 0)
    out_shapes = (
        jax.ShapeDtypeStruct((B, S, Q_LORA), BF16),
        jax.ShapeDtypeStruct((B, S, KV_LORA), F32),
        jax.ShapeDtypeStruct((B, S, LANES), F32),
        jax.ShapeDtypeStruct((B, S, hw), BF16),
        jax.ShapeDtypeStruct((B, S, hw), BF16),
        jax.ShapeDtypeStruct((B, S, hw), BF16),
        jax.ShapeDtypeStruct((B, keep, hw), F32),
        jax.ShapeDtypeStruct((B, keep, hw), F32),
    )
    out_specs = (
        pl.BlockSpec((1, tm, Q_LORA), row),
        pl.BlockSpec((1, tm, KV_LORA), row),
        pl.BlockSpec((1, tm, LANES), row),
        pl.BlockSpec((1, tm, hw), row),
        pl.BlockSpec((1, tm, hw), row),
        pl.BlockSpec((1, tm, hw), row),
        pl.BlockSpec((1, tm, hw), tail),
        pl.BlockSpec((1, tm, hw), tail),
    )
    weights = (w["cq"], w["ckv"], w["kpe"], w["qb"], w["kb"], w["vb"])
    gains = (g_cq, g_ckv, g_qb, g_kb)
    return pl.pallas_call(
        functools.partial(_proj_kernel, n_heads=n_heads, tail_start=tail_start),
        grid=(B, nt),
        in_specs=[pl.BlockSpec((1, tm, D), row), _resident(norm_mix)]
        + [_resident(a) for a in weights] + [_resident(a) for a in gains],
        out_specs=out_specs,
        out_shape=out_shapes,
        compiler_params=_params("arbitrary", "arbitrary"),
        name="proj",
    )(x, norm_mix, *weights, *gains)


def _q_up_kernel(cqn_ref, w_ref, gn_ref, gr_ref, grs_ref, aug_ref, cos_ref, sin_ref, q_ref,
                 *, n_heads):
    hw = n_heads * LANES
    hp = hw // 2
    y = _dot(cqn_ref[0], w_ref[...])
    low = lax.broadcasted_iota(jnp.int32, (1, LANES), 1) < ROPE
    for j in range(n_heads // 2):
        r = y[:, hw + j * LANES: hw + (j + 1) * LANES]
        rs = y[:, hw + hp + j * LANES: hw + hp + (j + 1) * LANES]
        r2 = r * r
        r2_of = (jnp.where(low, r2, 0.0), jnp.where(low, 0.0, r2))
        rinv = []
        for e in range(2):
            h = 2 * j + e
            n = y[:, h * LANES:(h + 1) * LANES]
            ss = jnp.sum(n * n + r2_of[e], axis=-1, keepdims=True)
            rinv.append(lax.rsqrt(ss * (1.0 / QK_MLA) + EPS))
            q_ref[0, h, :, 0:LANES] = (n * rinv[e] * gn_ref[...]).astype(BF16)
        rot = ((r * gr_ref[...]) * cos_ref[...] + (rs * grs_ref[...]) * sin_ref[...]) \
            * jnp.where(low, rinv[0], rinv[1])
        q_ref[0, 2 * j, :, LANES:QK_PAD] = (jnp.where(low, rot, 0.0) + aug_ref[...]).astype(BF16)
        q_ref[0, 2 * j + 1, :, LANES:QK_PAD] = (
            jnp.where(low, pltpu.roll(rot, ROPE, 1), 0.0) + aug_ref[...]).astype(BF16)


def _q_up(cqn, w, g_n, g_r, g_rs, aug, cos, sin, *, tm):
    B, S, _ = cqn.shape
    n_heads = w.shape[1] // (2 * LANES)
    row = lambda b, t: (b, t, 0)
    pos = lambda b, t: (t, 0)
    return pl.pallas_call(
        functools.partial(_q_up_kernel, n_heads=n_heads),
        grid=(B, S // tm),
        in_specs=[pl.BlockSpec((1, tm, Q_LORA), row), _resident(w),
                  _resident(g_n), _resident(g_r), _resident(g_rs), _resident(aug),
                  pl.BlockSpec((tm, LANES), pos), pl.BlockSpec((tm, LANES), pos)],
        out_specs=pl.BlockSpec((1, n_heads, tm, QK_PAD), lambda b, t: (b, 0, t, 0)),
        out_shape=jax.ShapeDtypeStruct((B, n_heads, S, QK_PAD), BF16),
        compiler_params=_params("arbitrary", "arbitrary"),
        name="q_up",
    )(cqn, w, g_n, g_r, g_rs, aug, cos, sin)


def _expand_kv(c, kpe, wuk, wuvt, g_n, g_r, k_aug, cos, sin, n_heads):
    cb = c.astype(BF16)
    kn = _dot(cb, wuk)
    vt = _dot_nt(wuvt, cb)
    a = kpe * g_r
    sspe = jnp.sum(kpe * kpe, axis=-1, keepdims=True)
    a_sw = pltpu.roll(a, HALF_ROPE, 1) + pltpu.roll(a, LANES - HALF_ROPE, 1)
    rot = a * cos + a_sw * sin
    keys = []
    for h in range(n_heads):
        n = kn[:, h * NOPE:(h + 1) * NOPE]
        ss = jnp.sum(n * n, axis=-1, keepdims=True) + sspe
        rinv = lax.rsqrt(ss * (1.0 / QK_MLA) + EPS)
        keys.append(jnp.concatenate([(n * rinv * g_n).astype(BF16),
                                     (rot * rinv + k_aug).astype(BF16)], axis=-1))
    return keys, vt


def _kv_up_kernel(ckv_ref, kpe_ref, wuk_ref, wuvt_ref, gn_ref, gr_ref, aug_ref, cos_ref, sin_ref,
                  k_ref, vt_ref, *, n_heads):
    keys, vt = _expand_kv(ckv_ref[0], kpe_ref[0], wuk_ref[...], wuvt_ref[...], gn_ref[...],
                          gr_ref[...], aug_ref[...], cos_ref[...], sin_ref[...], n_heads)
    for h in range(n_heads):
        k_ref[0, h] = keys[h]
        vt_ref[0, h] = vt[h * V_MLA:(h + 1) * V_MLA, :].astype(BF16)


def _kv_up(ckv, kpe_pad, w_uk, w_uvt, g_n, g_r, k_aug, cos, sin, *, tm):
    B, S, _ = ckv.shape
    n_heads = w_uk.shape[1] // NOPE
    row = lambda b, t: (b, t, 0)
    pos = lambda b, t: (t, 0)
    return pl.pallas_call(
        functools.partial(_kv_up_kernel, n_heads=n_heads),
        grid=(B, S // tm),
        in_specs=[pl.BlockSpec((1, tm, KV_LORA), row), pl.BlockSpec((1, tm, LANES), row),
                  _resident(w_uk), _resident(w_uvt),
                  _resident(g_n), _resident(g_r), _resident(k_aug),
                  pl.BlockSpec((tm, LANES), pos), pl.BlockSpec((tm, LANES), pos)],
        out_specs=(pl.BlockSpec((1, n_heads, tm, QK_PAD), lambda b, t: (b, 0, t, 0)),
                   pl.BlockSpec((1, n_heads, V_MLA, tm), lambda b, t: (b, 0, 0, t))),
        out_shape=(jax.ShapeDtypeStruct((B, n_heads, S, QK_PAD), BF16),
                   jax.ShapeDtypeStruct((B, n_heads, V_MLA, S), BF16)),
        compiler_params=_params("arbitrary", "arbitrary"),
        name="kv_up",
    )(ckv, kpe_pad, w_uk, w_uvt, g_n, g_r, k_aug, cos, sin)


def _mla_flash_kernel(q_ref, k_ref, vt_ref, o_ref, m_sc, l_sc, acc_sc, *, tq, td, hg, bounded):
    i = pl.program_id(2)
    if not bounded:
        m_sc[...] = jnp.full(m_sc.shape, -jnp.inf, F32)
    l_sc[...] = jnp.zeros(l_sc.shape, F32)
    acc_sc[...] = jnp.zeros(acc_sc.shape, F32)

    def step(tiles):
        sts = [(hh, t, _dot_nt(k_ref[0, hh, pl.ds(t[0], t[1]), :], q_ref[0, hh, t[2]:, :]))
               for t in tiles for hh in range(hg)]
        for hh, (k0, kn, q0, mask), st in sts:
            if mask is not None:
                st = jnp.where(mask, st, NEG)
            vt = vt_ref[0, hh, :, pl.ds(k0, kn)]
            if bounded:
                p = jnp.exp2(st)
                l_sc[hh, :, q0:] += jnp.sum(p, axis=0, keepdims=True)
                acc_sc[hh, :, q0:] += _dot(vt, p.astype(BF16))
            else:
                m_prev = m_sc[hh, :, q0:]
                m_new = jnp.maximum(m_prev, jnp.max(st, axis=0, keepdims=True))
                alpha = jnp.exp2(m_prev - m_new)
                p = jnp.exp2(st - m_new)
                l_sc[hh, :, q0:] = alpha * l_sc[hh, :, q0:] + jnp.sum(p, axis=0, keepdims=True)
                acc_sc[hh, :, q0:] = acc_sc[hh, :, q0:] * alpha + _dot(vt, p.astype(BF16))
                m_sc[hh, :, q0:] = m_new

    def body(j, carry):
        step([(pl.multiple_of(j * tq, tq), tq, 0, None)])
        return carry

    lax.fori_loop(0, i, body, 0)
    diag = []
    for d in range(tq // td):
        nq = tq - d * td
        q_chunk = lax.broadcasted_iota(jnp.int32, (td, nq), 1) // CHUNK
        k_chunk = lax.broadcasted_iota(jnp.int32, (td, nq), 0) // CHUNK
        diag.append((pl.multiple_of(i * tq + d * td, td), td, d * td, k_chunk <= q_chunk))
    step(diag)
    for hh in range(hg):
        o_ref[0, :, hh * V_MLA:(hh + 1) * V_MLA] = (acc_sc[hh] / l_sc[hh]).T.astype(o_ref.dtype)


def _mla_flash(q, k, vt, *, tq, td, hg, bounded):
    B, H, S, _ = q.shape
    return pl.pallas_call(
        functools.partial(_mla_flash_kernel, tq=tq, td=td, hg=hg, bounded=bounded),
        grid=(B, H // hg, S // tq),
        in_specs=[pl.BlockSpec((1, hg, tq, QK_PAD), lambda b, h, i: (b, h, i, 0)),
                  pl.BlockSpec((1, hg, S, QK_PAD), lambda b, h, i: (b, h, 0, 0)),
                  pl.BlockSpec((1, hg, V_MLA, S), lambda b, h, i: (b, h, 0, 0))],
        out_specs=pl.BlockSpec((1, tq, hg * V_MLA), lambda b, h, i: (b, i, h)),
        out_shape=jax.ShapeDtypeStruct((B, S, H * V_MLA), BF16),
        scratch_shapes=[pltpu.VMEM((hg, 1, tq), F32), pltpu.VMEM((hg, 1, tq), F32),
                        pltpu.VMEM((hg, V_MLA, tq), F32)],
        compiler_params=_params("arbitrary", "arbitrary", "arbitrary"),
        name="mla_flash_bounded" if bounded else "mla_flash",
    )(q, k, vt)


def _toeplitz(u_row, rows, cols, stride=1):
    u = jnp.broadcast_to(u_row, (rows, u_row.shape[-1]))
    return pltpu.roll(u, 0, 1, stride=stride, stride_axis=0)[:, :cols]


def _softmax_pv(parts, bounded):
    m = None
    if not bounded:
        for s, _ in parts:
            mx = jnp.max(s, axis=-1, keepdims=True)
            m = mx if m is None else jnp.maximum(m, mx)
    l = None
    acc = None
    for s, v in parts:
        p = jnp.exp2(s if bounded else s - m)
        ps = jnp.sum(p, axis=-1, keepdims=True)
        pv = _dot(p.astype(BF16), v)
        l = ps if l is None else l + ps
        acc = pv if acc is None else acc + pv
    return acc / l


def _band_kernel(q_ref, kc_ref, kp_ref, vc_ref, vp_ref, u_ref, o_ref, bc_sc, bp_sc,
                 *, n_heads, splits, bounded):
    b, g = pl.program_id(0), pl.program_id(1)
    tg = bc_sc.shape[1]

    @pl.when((b == 0) & (g == 0))
    def _():
        qc = lax.broadcasted_iota(jnp.int32, (tg, tg), 0) // CHUNK
        kc = lax.broadcasted_iota(jnp.int32, (tg, tg), 1) // CHUNK
        for h in range(n_heads):
            bc_sc[h] = jnp.where(kc <= qc, _toeplitz(u_ref[h, 0:1, :], tg, tg), NEG)
            bp_sc[h] = jnp.where(kc >= qc, _toeplitz(u_ref[h, 1:2, :], tg, tg), NEG)

    no_prev = jnp.where(g == 0, NEG, 0.0)
    tr = tg // splits
    for h0 in range(0, n_heads, 2):
        scores = []
        for h in (h0, h0 + 1):
            sl = slice(h * D_BAND, (h + 1) * D_BAND)
            for a in range(splits):
                rows, old, new = slice(a * tr, (a + 1) * tr), slice(a * tr, tg), slice(0, (a + 1) * tr)
                q = q_ref[0, rows, sl]
                scores.append((rows, old, new, sl,
                               _dot_nt(q, kp_ref[0, old, sl]) + (bp_sc[h, rows, old] + no_prev),
                               _dot_nt(q, kc_ref[0, new, sl]) + bc_sc[h, rows, new]))
        for rows, old, new, sl, s_prev, s_cur in scores:
            parts = [(s_prev, vp_ref[0, old, sl]), (s_cur, vc_ref[0, new, sl])]
            o_ref[0, rows, sl] = _softmax_pv(parts, bounded).astype(o_ref.dtype)


def _band(qb, kb, vb, u_tab, *, bounded):
    B, S, hw = qb.shape
    H = hw // D_BAND
    tg = BAND_PAST
    cur = lambda b, g: (b, g, 0)
    prev = lambda b, g: (b, jnp.maximum(g - 1, 0), 0)
    blk = (1, tg, hw)
    return pl.pallas_call(
        functools.partial(_band_kernel, n_heads=H, splits=4, bounded=bounded),
        grid=(B, S // tg),
        in_specs=[pl.BlockSpec(blk, cur), pl.BlockSpec(blk, cur), pl.BlockSpec(blk, prev),
                  pl.BlockSpec(blk, cur), pl.BlockSpec(blk, prev),
                  _resident(u_tab)],
        out_specs=pl.BlockSpec(blk, cur),
        out_shape=jax.ShapeDtypeStruct((B, S, hw), BF16),
        scratch_shapes=[pltpu.VMEM((H, tg, tg), F32), pltpu.VMEM((H, tg, tg), F32)],
        compiler_params=_params("arbitrary", "arbitrary"),
        name="band_bounded" if bounded else "band",
    )(qb, kb, kb, vb, vb, u_tab)


def _mla_sample_kernel(q_ref, cc_ref, pc_ref, cn_ref, pn_ref, wuk_ref, wuvt_ref, gn_ref, gr_ref,
                       aug_ref, cosc_ref, sinc_ref, cosn_ref, sinn_ref, o_ref, *, n_heads, t_new):
    args = (wuk_ref[...], wuvt_ref[...], gn_ref[...], gr_ref[...], aug_ref[...])
    kc, vtc = _expand_kv(cc_ref[0], pc_ref[0], *args, cosc_ref[...], sinc_ref[...], n_heads)
    kn, vtn = _expand_kv(cn_ref[0], pn_ref[0], *args, cosn_ref[...], sinn_ref[...], n_heads)
    rows_n = cn_ref.shape[1]
    new_ok = lax.broadcasted_iota(jnp.int32, (q_ref.shape[2], rows_n), 1) < t_new
    scores = [(_dot_nt(q_ref[0, h], kc[h]), jnp.where(new_ok, _dot_nt(q_ref[0, h], kn[h]), NEG))
              for h in range(n_heads)]
    for h, (s_c, s_n) in enumerate(scores):
        m = jnp.maximum(jnp.max(s_c, axis=-1, keepdims=True), jnp.max(s_n, axis=-1, keepdims=True))
        p_c = jnp.exp2(s_c - m)
        p_n = jnp.exp2(s_n - m)
        l = jnp.sum(p_c, axis=-1, keepdims=True) + jnp.sum(p_n, axis=-1, keepdims=True)
        sl = slice(h * V_MLA, (h + 1) * V_MLA)
        pv = (_dot_nt(p_c.astype(BF16), vtc[sl].astype(BF16))
              + _dot_nt(p_n.astype(BF16), vtn[sl].astype(BF16)))
        o_ref[0, :, sl] = (pv / l).astype(o_ref.dtype)


def _mla_sample(q, ckv_c, kpe_c, ckv_n, kpe_n, w_uk, w_uvt, g_n, g_r, k_aug, tabs_c, tabs_n, *, t_new):
    Bd, H, T, _ = q.shape
    P = ckv_c.shape[1]
    rn = ckv_n.shape[1]
    bat = lambda b: (b, 0, 0)
    return pl.pallas_call(
        functools.partial(_mla_sample_kernel, n_heads=H, t_new=t_new),
        grid=(Bd,),
        in_specs=[pl.BlockSpec((1, H, T, QK_PAD), lambda b: (b, 0, 0, 0)),
                  pl.BlockSpec((1, P, KV_LORA), bat), pl.BlockSpec((1, P, LANES), bat),
                  pl.BlockSpec((1, rn, KV_LORA), bat), pl.BlockSpec((1, rn, LANES), bat),
                  _resident(w_uk), _resident(w_uvt), _resident(g_n), _resident(g_r), _resident(k_aug)]
        + [_resident(t) for t in (*tabs_c, *tabs_n)],
        out_specs=pl.BlockSpec((1, T, H * V_MLA), bat),
        out_shape=jax.ShapeDtypeStruct((Bd, T, H * V_MLA), BF16),
        compiler_params=_params("arbitrary"),
        name="mla_sample",
    )(q, ckv_c, kpe_c, ckv_n, kpe_n, w_uk, w_uvt, g_n, g_r, k_aug, *tabs_c, *tabs_n)


def _band_sample_kernel(q_ref, kc_ref, vc_ref, kn_ref, vn_ref, wc_ref, wn_ref, o_ref, *, n_heads):
    q = q_ref[0]
    T = q.shape[0] // n_heads
    tab = lambda w_ref, cols: jnp.concatenate(
        [_toeplitz(w_ref[h:h + 1, :], T, cols, stride=n_heads) for h in range(n_heads)], axis=0)
    s_c = _dot_nt(q, kc_ref[0].astype(BF16)) + tab(wc_ref, kc_ref.shape[1])
    s_n = _dot_nt(q, kn_ref[0]) + tab(wn_ref, kn_ref.shape[1])
    parts = [(s_c, vc_ref[0].astype(BF16)), (s_n, vn_ref[0])]
    o_ref[0] = _softmax_pv(parts, False).astype(o_ref.dtype)


def _band_sample(q, k_c, v_c, k_n, v_n, w_c, w_n, *, n_heads):
    Bd = q.shape[0]
    bat = lambda b: (b, 0, 0)
    blk = lambda a: pl.BlockSpec((1,) + a.shape[1:], bat)
    return pl.pallas_call(
        functools.partial(_band_sample_kernel, n_heads=n_heads),
        grid=(Bd,),
        in_specs=[blk(q), blk(k_c), blk(v_c), blk(k_n), blk(v_n), _resident(w_c), _resident(w_n)],
        out_specs=blk(q),
        out_shape=jax.ShapeDtypeStruct(q.shape, BF16),
        compiler_params=_params("arbitrary"),
        name="band_sample",
    )(q, k_c, v_c, k_n, v_n, w_c, w_n)


def _merge_kernel(x_ref, oa_ref, ob_ref, woa_ref, wob_ref, h_ref):
    h_ref[0] = x_ref[0] + _dot(oa_ref[0], woa_ref[...]) + _dot(ob_ref[0], wob_ref[...])


def _merge(x, oa, ob, w_oa, w_ob, *, tm):
    B, S, D = x.shape
    row = lambda b, t: (b, t, 0)
    return pl.pallas_call(
        _merge_kernel,
        grid=(B, S // tm),
        in_specs=[pl.BlockSpec((1, tm, D), row), pl.BlockSpec((1, tm, oa.shape[-1]), row),
                  pl.BlockSpec((1, tm, ob.shape[-1]), row),
                  _resident(w_oa), _resident(w_ob)],
        out_specs=pl.BlockSpec((1, tm, D), row),
        out_shape=jax.ShapeDtypeStruct((B, S, D), F32),
        compiler_params=_params("arbitrary", "arbitrary"),
        name="merge",
    )(x, oa, ob, w_oa, w_ob)


def _ffn_kernel(h_ref, g_ref, wup_ref, wdn_ref, y_ref, hn_sc):
    j = pl.program_id(2)

    @pl.when(j == 0)
    def _():
        h = h_ref[0]
        hn_sc[...] = _rms(h, g_ref[...]).astype(BF16)
        y_ref[0] = h

    u = jnp.maximum(_dot(hn_sc[...], wup_ref[...]), 0.0)
    y_ref[0] += _dot((u * u).astype(BF16), wdn_ref[...])


def _ffn(h, g, w_up, w_down, *, tm, tf):
    B, S, D = h.shape
    F = w_up.shape[1]
    row = lambda b, t, j: (b, t, 0)
    return pl.pallas_call(
        _ffn_kernel,
        grid=(B, S // tm, F // tf),
        in_specs=[pl.BlockSpec((1, tm, D), row), _resident(g),
                  pl.BlockSpec((D, tf), lambda b, t, j: (0, j)),
                  pl.BlockSpec((tf, D), lambda b, t, j: (j, 0))],
        out_specs=pl.BlockSpec((1, tm, D), row),
        out_shape=jax.ShapeDtypeStruct((B, S, D), F32),
        scratch_shapes=[pltpu.VMEM((tm, D), BF16)],
        compiler_params=_params("arbitrary", "arbitrary", "arbitrary"),
        name="ffn",
    )(h, g, w_up, w_down)


def _rope_tables(pos):
    inv = 1.0 / (ROPE_BASE ** (jnp.arange(0, ROPE, 2, dtype=F32) / ROPE))
    ang = pos.astype(F32)[:, None] * inv[None, :]
    c, s = jnp.cos(ang), jnp.sin(ang)
    z = jnp.zeros((pos.shape[0], LANES - ROPE), F32)
    return ((jnp.concatenate([c, c, z], axis=1), jnp.concatenate([-s, s, z], axis=1)),
            (jnp.concatenate([c, c, c, c], axis=1), jnp.concatenate([-s, s, -s, s], axis=1)))


def _pad_lanes(a, width=LANES):
    return jnp.pad(a, [(0, 0)] * (a.ndim - 1) + [(0, width - a.shape[-1])])


def _pad_rows(a, rows):
    return jnp.pad(a, [(0, 0), (0, rows - a.shape[1]), (0, 0)])


def _swap_halves(a):
    return jnp.concatenate([a[..., HALF_ROPE:], a[..., :HALF_ROPE]], axis=-1)


def _prep_weights(w_in, w_uq, w_uk, w_uv, w_o, w_up, w_down):
    hb3 = (w_in.shape[1] - Q_LORA - KV_LORA - ROPE) // 3
    o = [0, Q_LORA, Q_LORA + KV_LORA, Q_LORA + KV_LORA + ROPE]
    o += [o[3] + hb3, o[3] + 2 * hb3, o[3] + 3 * hb3]
    names = ("cq", "ckv", "kpe", "qb", "kb", "vb")
    w = {n: w_in[:, o[i]:o[i + 1]].astype(BF16) for i, n in enumerate(names)}
    w["kpe"] = _pad_lanes(w["kpe"])
    ha = w_uq.shape[1] // QK_MLA
    uq = w_uq.astype(BF16).reshape(Q_LORA, ha, QK_MLA)
    nope = uq[:, :, :NOPE].reshape(Q_LORA, ha * NOPE)
    rope = uq[:, :, NOPE:]
    w["uq"] = jnp.concatenate(
        [nope, rope.reshape(Q_LORA, ha * ROPE), _swap_halves(rope).reshape(Q_LORA, ha * ROPE)], axis=1)
    w["uk"] = w_uk.astype(BF16)
    w["uvt"] = w_uv.astype(BF16).T
    half = ha * V_MLA
    ob = w_o.astype(BF16)
    w["oa"], w["ob"] = ob[:half], ob[half:]
    w["up"], w["down"] = w_up.astype(BF16), w_down.astype(BF16)
    return w


def _softmax_setup(g_qa, g_ka, g_qb, g_kb, rel_bias):
    c_a = QK_MLA ** -0.5 * LOG2E
    c_b = D_BAND ** -0.5 * LOG2E
    amax = lambda a: jnp.max(jnp.abs(a))
    bound_a = c_a * QK_MLA * amax(g_qa) * amax(g_ka) * BOUND_MARGIN
    bound_b = c_b * D_BAND * amax(g_qb) * amax(g_kb) * BOUND_MARGIN + LOG2E * amax(rel_bias)
    fast_a = bound_a <= FAST_LIMIT
    fast_b = bound_b <= FAST_LIMIT
    shift_a = jnp.where(fast_a, bound_a, 0.0)
    shift_b = jnp.where(fast_b, bound_b, 0.0)
    lane = jnp.arange(LANES)[None, :]
    twice = lambda a: jnp.concatenate([a, a], axis=-1)
    g = {"qa_n": g_qa[None, :NOPE] * c_a, "qa_r": twice(g_qa[None, NOPE:]) * c_a,
         "qa_rs": twice(_swap_halves(g_qa[None, NOPE:])) * c_a,
         "q_aug": (lane == AUG_LANE).astype(F32),
         "ka_n": g_ka[None, :NOPE], "ka_r": _pad_lanes(g_ka[None, NOPE:]),
         "k_aug": jnp.where(lane == AUG_LANE, -shift_a, 0.0).astype(F32),
         "qb": g_qb * c_b}
    return g, fast_a, fast_b, shift_b


def _band_vectors(rel_bias, shift):
    t = BAND_PAST
    e = np.arange(2 * t)
    e = np.where(e < t, e, e - 2 * t)
    idx = np.stack([np.clip(-e, -MAX_REL, MAX_REL), np.clip(t - e, -MAX_REL, MAX_REL)]) + MAX_REL
    return (rel_bias[:, idx] * LOG2E - shift).astype(F32)


def _band_sample_vectors(rel_bias, lb):
    H = rel_bias.shape[0]

    def interleave(dist, n_pos, n):
        e = np.arange(n)
        e = np.where(e < n_pos, e, e - n)
        u = rel_bias[:, np.clip(dist - e, -MAX_REL, MAX_REL) + MAX_REL] * LOG2E
        own = np.arange(H)[:, None, None] == np.arange(H)[None, None, :]
        return jnp.where(own, u[:, :, None], NEG).reshape(H, n * H).astype(F32)

    return interleave(lb, lb, lb + LANES), interleave(0, CHUNK, 2 * CHUNK)


def _row_tile(n, pref):
    return pref if n % pref == 0 else n


def kernel(x_prompt, x_sample, cache_mla_ckv, cache_mla_kpe, cache_band_k, cache_band_v,
           norm_mix, w_in, g_cq, w_uq, g_ckv, w_uk, w_uv, g_qa, g_ka, g_qb, g_kb, rel_bias,
           w_o, norm_ffn, w_up, w_down):
    depth = w_in.shape[0]
    assert depth == 1, "single-layer step"
    B, S, D = x_prompt.shape
    Bd, T, _ = x_sample.shape
    P = cache_mla_ckv.shape[2]
    Lb = cache_band_k.shape[2]
    keep_p = min(BAND_PAST, S)
    assert S % BAND_PAST == 0 and T <= CHUNK

    w = _prep_weights(w_in[0], w_uq[0], w_uk[0], w_uv[0], w_o[0], w_up[0], w_down[0])
    g, fast_a, fast_b, shift_b = _softmax_setup(g_qa[0], g_ka[0], g_qb, g_kb, rel_bias[0])
    ha = w["uk"].shape[1] // NOPE
    hb = rel_bias.shape[1]
    hw = hb * D_BAND
    gains = (g_cq, g_ckv, g["qb"], g_kb)
    q_args = (w["uq"], g["qa_n"], g["qa_r"], g["qa_rs"], g["q_aug"])
    kv_args = (w["uk"], w["uvt"], g["ka_n"], g["ka_r"], g["k_aug"])

    cqn, ckv, kpe_pad, qb, kb, vb, kb_tail, vb_tail = _proj(
        x_prompt, norm_mix, w, *gains, keep=keep_p, tm=_row_tile(S, 512))
    rn = LANES
    tabs, tabs_q = _rope_tables(jnp.arange(max(S, P + rn), dtype=jnp.int32))
    tabs_p = tuple(t[:S] for t in tabs)
    tmq = _row_tile(S, 512)
    q = _q_up(cqn, *q_args, *(t[:S] for t in tabs_q), tm=tmq)
    k, vt = _kv_up(ckv, kpe_pad, *kv_args, *tabs_p, tm=_row_tile(S, 1024))
    tqa = _row_tile(S, 1024)
    flash = lambda bounded: functools.partial(_mla_flash, tq=tqa, td=min(tqa, 512), hg=2, bounded=bounded)
    oa = lax.cond(fast_a, flash(True), flash(False), q, k, vt)
    u_tab = _band_vectors(rel_bias[0], shift_b)
    ob = lax.cond(fast_b, functools.partial(_band, bounded=True),
                  functools.partial(_band, bounded=False), qb, kb, vb, u_tab)
    h = _merge(x_prompt, oa, ob, w["oa"], w["ob"], tm=_row_tile(S, 512))
    y_prompt = _ffn(h, norm_ffn, w["up"], w["down"], tm=_row_tile(S, 512), tf=2048)

    n_s = Bd * T
    xs = x_sample.reshape(1, n_s, D)
    cqn_s, ckv_s, kpe_pad_s, qb_s, kb_s, vb_s, kb_s32, vb_s32 = _proj(
        xs, norm_mix, w, *gains, keep=n_s, tm=_row_tile(n_s, 256))
    q_s = _q_up(cqn_s.reshape(Bd, T, Q_LORA), *q_args, *(t[P:P + T] for t in tabs_q), tm=T)
    oa_s = _mla_sample(
        q_s, cache_mla_ckv[0], _pad_lanes(cache_mla_kpe[0]),
        _pad_rows(ckv_s.reshape(Bd, T, KV_LORA), rn), _pad_rows(kpe_pad_s.reshape(Bd, T, LANES), rn),
        *kv_args, tuple(t[:P] for t in tabs), tuple(t[P:P + rn] for t in tabs), t_new=T)
    w_c, w_n = _band_sample_vectors(rel_bias[0], Lb)
    head_major = lambda a: a.reshape(Bd, T, hb, D_BAND).transpose(0, 2, 1, 3).reshape(Bd, hb * T, D_BAND)
    ob_s = _band_sample(
        head_major(qb_s), cache_band_k.reshape(Bd, Lb * hb, D_BAND), cache_band_v.reshape(Bd, Lb * hb, D_BAND),
        kb_s.reshape(Bd, T * hb, D_BAND), vb_s.reshape(Bd, T * hb, D_BAND), w_c, w_n, n_heads=hb)
    ob_s = ob_s.reshape(Bd, hb, T, D_BAND).transpose(0, 2, 1, 3)
    h_s = _merge(xs, oa_s.reshape(1, n_s, ha * V_MLA), ob_s.reshape(1, n_s, hw), w["oa"], w["ob"],
                 tm=n_s)
    y_sample = _ffn(h_s, norm_ffn, w["up"], w["down"], tm=n_s, tf=2048).reshape(Bd, T, D)

    return (y_prompt, y_sample,
            ckv[None], kpe_pad[None, ..., :ROPE],
            kb_tail.reshape(B, keep_p, hb, D_BAND)[None], vb_tail.reshape(B, keep_p, hb, D_BAND)[None],
            ckv_s.reshape(Bd, T, KV_LORA)[None], kpe_pad_s.reshape(Bd, T, LANES)[None, ..., :ROPE],
            kb_s32.reshape(Bd, T, hb, D_BAND)[None], vb_s32.reshape(Bd, T, hb, D_BAND)[None])
```

```python
import functools

import jax
import jax.numpy as jnp
import numpy as np
from jax import lax
from jax.experimental import pallas as pl
from jax.experimental.pallas import tpu as pltpu

CHUNK = 64
EPS = 1e-6
NOPE = 128
ROPE = 64
HALF_ROPE = ROPE // 2
QK_MLA = NOPE + ROPE
V_MLA = 128
D_BAND = 128
BAND_CHUNKS = 8
BAND_PAST = BAND_CHUNKS * CHUNK
MAX_REL = 128
ROPE_BASE = 10000.0
Q_LORA = 512
KV_LORA = 256
NEG = -1e30
LOG2E = 1.4426950408889634

BOUND_MARGIN = 1.02
FAST_LIMIT = 60.0

LANES = 128
BF16_ROWS = 16
QK_PAD = 2 * LANES
AUG_LANE = ROPE
VMEM_LIMIT = 56 * 1024 * 1024

BF16 = jnp.bfloat16
F32 = jnp.float32

NT_DIMS = (((1,), (1,)), ((), ()))


def _params(*sem):
    return pltpu.CompilerParams(dimension_semantics=sem, vmem_limit_bytes=VMEM_LIMIT)


def _resident(a):
    return pl.BlockSpec(a.shape, lambda *_: (0,) * a.ndim, pipeline_mode=pl.Buffered(1))


def _rms(x, g):
    ms = jnp.mean(x * x, axis=-1, keepdims=True)
    return x * lax.rsqrt(ms + EPS) * g


def _dot(a, b):
    return jnp.dot(a, b, preferred_element_type=F32)


def _dot_nt(a, b):
    return lax.dot_general(a, b, NT_DIMS, preferred_element_type=F32)


def _proj_kernel(x_ref, nm_ref, wcq_ref, wckv_ref, wkpe_ref, wqb_ref, wkb_ref, wvb_ref,
                 gcq_ref, gckv_ref, gqb_ref, gkb_ref,
                 cqn_ref, ckv_ref, kpe_ref, kpe_out_ref, qb_ref, kb_ref, vb_ref, kbt_ref, vbt_ref,
                 *, n_heads, tail_start):
    t = pl.program_id(1)
    xn = _rms(x_ref[0], nm_ref[...]).astype(BF16)
    cqn_ref[0] = _rms(_dot(xn, wcq_ref[...]), gcq_ref[...]).astype(BF16)
    ckv_ref[0] = _rms(_dot(xn, wckv_ref[...]), gckv_ref[...])
    kpe = _dot(xn, wkpe_ref[...])
    kpe_ref[0] = kpe
    kpe_out_ref[0] = kpe[:, :ROPE]
    qb = _dot(xn, wqb_ref[...])
    kb = _dot(xn, wkb_ref[...])
    vb = _dot(xn, wvb_ref[...])
    vb_ref[0] = vb.astype(BF16)
    kbn = []
    for h in range(n_heads):
        sl = slice(h * D_BAND, (h + 1) * D_BAND)
        qb_ref[0, :, sl] = _rms(qb[:, sl], gqb_ref[...]).astype(BF16)
        kbn.append(_rms(kb[:, sl], gkb_ref[...]))
    kbn = jnp.concatenate(kbn, axis=-1)
    kb_ref[0] = kbn.astype(BF16)

    @pl.when(t >= tail_start)
    def _():
        kbt_ref[0] = kbn
        vbt_ref[0] = vb


def _proj(x, norm_mix, w, g_cq, g_ckv, g_qb, g_kb, *, keep, tm):
    B, S, D = x.shape
    n_heads = w["qb"].shape[1] // D_BAND
    hw = n_heads * D_BAND
    nt = S // tm
    tail_start = nt - keep // tm
    row = lambda b, t: (b, t, 0)
    tail = lambda b, t: (b, jnp.maximum(t - tail_start, 0), 0)
    out_shapes = (
        jax.ShapeDtypeStruct((B, S, Q_LORA), BF16),
        jax.ShapeDtypeStruct((B, S, KV_LORA), F32),
        jax.ShapeDtypeStruct((B, S, LANES), F32),
        jax.ShapeDtypeStruct((B, S, ROPE), F32),
        jax.ShapeDtypeStruct((B, S, hw), BF16),
        jax.ShapeDtypeStruct((B, S, hw), BF16),
        jax.ShapeDtypeStruct((B, S, hw), BF16),
        jax.ShapeDtypeStruct((B, keep, hw), F32),
        jax.ShapeDtypeStruct((B, keep, hw), F32),
    )
    out_specs = (
        pl.BlockSpec((1, tm, Q_LORA), row),
        pl.BlockSpec((1, tm, KV_LORA), row),
        pl.BlockSpec((1, tm, LANES), row),
        pl.BlockSpec((1, tm, ROPE), row),
        pl.BlockSpec((1, tm, hw), row),
        pl.BlockSpec((1, tm, hw), row),
        pl.BlockSpec((1, tm, hw), row),
        pl.BlockSpec((1, tm, hw), tail),
        pl.BlockSpec((1, tm, hw), tail),
    )
    weights = (w["cq"], w["ckv"], w["kpe"], w["qb"], w["kb"], w["vb"])
    gains = (g_cq, g_ckv, g_qb, g_kb)
    return pl.pallas_call(
        functools.partial(_proj_kernel, n_heads=n_heads, tail_start=tail_start),
        grid=(B, nt),
        in_specs=[pl.BlockSpec((1, tm, D), row), _resident(norm_mix)]
        + [_resident(a) for a in weights] + [_resident(a) for a in gains],
        out_specs=out_specs,
        out_shape=out_shapes,
        compiler_params=_params("arbitrary", "arbitrary"),
        name="proj",
    )(x, norm_mix, *weights, *gains)


def _q_up_kernel(cqn_ref, w_ref, gn_ref, gr_ref, grs_ref, aug_ref, cos_ref, sin_ref, q_ref,
                 *, n_heads):
    hw = n_heads * LANES
    hp = hw // 2
    y = _dot(cqn_ref[0], w_ref[...])
    low = lax.broadcasted_iota(jnp.int32, (1, LANES), 1) < ROPE
    for j in range(n_heads // 2):
        r = y[:, hw + j * LANES: hw + (j + 1) * LANES]
        rs = y[:, hw + hp + j * LANES: hw + hp + (j + 1) * LANES]
        r2 = r * r
        r2_of = (jnp.where(low, r2, 0.0), jnp.where(low, 0.0, r2))
        rinv = []
        for e in range(2):
            h = 2 * j + e
            n = y[:, h * LANES:(h + 1) * LANES]
            ss = jnp.sum(n * n + r2_of[e], axis=-1, keepdims=True)
            rinv.append(lax.rsqrt(ss * (1.0 / QK_MLA) + EPS))
            q_ref[0, h, :, 0:LANES] = (n * rinv[e] * gn_ref[...]).astype(BF16)
        rot = ((r * gr_ref[...]) * cos_ref[...] + (rs * grs_ref[...]) * sin_ref[...]) \
            * jnp.where(low, rinv[0], rinv[1])
        q_ref[0, 2 * j, :, LANES:QK_PAD] = (jnp.where(low, rot, 0.0) + aug_ref[...]).astype(BF16)
        q_ref[0, 2 * j + 1, :, LANES:QK_PAD] = (
            jnp.where(low, pltpu.roll(rot, ROPE, 1), 0.0) + aug_ref[...]).astype(BF16)


def _q_up(cqn, w, g_n, g_r, g_rs, aug, cos, sin, *, tm, pos0=0):
    B, S, _ = cqn.shape
    n_heads = w.shape[1] // (2 * LANES)
    assert pos0 % tm == 0
    row = lambda b, t: (b, t, 0)
    pos = lambda b, t: (pos0 // tm + t, 0)
    return pl.pallas_call(
        functools.partial(_q_up_kernel, n_heads=n_heads),
        grid=(B, S // tm),
        in_specs=[pl.BlockSpec((1, tm, Q_LORA), row), _resident(w),
                  _resident(g_n), _resident(g_r), _resident(g_rs), _resident(aug),
                  pl.BlockSpec((tm, LANES), pos), pl.BlockSpec((tm, LANES), pos)],
        out_specs=pl.BlockSpec((1, n_heads, tm, QK_PAD), lambda b, t: (b, 0, t, 0)),
        out_shape=jax.ShapeDtypeStruct((B, n_heads, S, QK_PAD), BF16),
        compiler_params=_params("arbitrary", "arbitrary"),
        name="q_up",
    )(cqn, w, g_n, g_r, g_rs, aug, cos, sin)


def _expand_kv(c, kpe, wuk, wuvt, g_n, g_r, k_aug, cos, sin, n_heads):
    cb = c.astype(BF16)
    kn = _dot(cb, wuk)
    vt = _dot_nt(wuvt, cb)
    a = kpe * g_r
    sspe = jnp.sum(kpe * kpe, axis=-1, keepdims=True)
    a_sw = pltpu.roll(a, HALF_ROPE, 1) + pltpu.roll(a, LANES - HALF_ROPE, 1)
    rot = a * cos + a_sw * sin
    keys = []
    for h in range(n_heads):
        n = kn[:, h * NOPE:(h + 1) * NOPE]
        ss = jnp.sum(n * n, axis=-1, keepdims=True) + sspe
        rinv = lax.rsqrt(ss * (1.0 / QK_MLA) + EPS)
        keys.append(jnp.concatenate([(n * rinv * g_n).astype(BF16),
                                     (rot * rinv + k_aug).astype(BF16)], axis=-1))
    return keys, vt


def _kv_up_kernel(ckv_ref, kpe_ref, wuk_ref, wuvt_ref, gn_ref, gr_ref, aug_ref, cos_ref, sin_ref,
                  k_ref, vt_ref, *, n_heads):
    keys, vt = _expand_kv(ckv_ref[0], kpe_ref[0], wuk_ref[...], wuvt_ref[...], gn_ref[...],
                          gr_ref[...], aug_ref[...], cos_ref[...], sin_ref[...], n_heads)
    for h in range(n_heads):
        k_ref[0, h] = keys[h]
        vt_ref[0, h] = vt[h * V_MLA:(h + 1) * V_MLA, :].astype(BF16)


def _kv_up(ckv, kpe_pad, w_uk, w_uvt, g_n, g_r, k_aug, cos, sin, *, tm):
    B, S, _ = ckv.shape
    n_heads = w_uk.shape[1] // NOPE
    row = lambda b, t: (b, t, 0)
    pos = lambda b, t: (t, 0)
    return pl.pallas_call(
        functools.partial(_kv_up_kernel, n_heads=n_heads),
        grid=(B, S // tm),
        in_specs=[pl.BlockSpec((1, tm, KV_LORA), row), pl.BlockSpec((1, tm, LANES), row),
                  _resident(w_uk), _resident(w_uvt),
                  _resident(g_n), _resident(g_r), _resident(k_aug),
                  pl.BlockSpec((tm, LANES), pos), pl.BlockSpec((tm, LANES), pos)],
        out_specs=(pl.BlockSpec((1, n_heads, tm, QK_PAD), lambda b, t: (b, 0, t, 0)),
                   pl.BlockSpec((1, n_heads, V_MLA, tm), lambda b, t: (b, 0, 0, t))),
        out_shape=(jax.ShapeDtypeStruct((B, n_heads, S, QK_PAD), BF16),
                   jax.ShapeDtypeStruct((B, n_heads, V_MLA, S), BF16)),
        compiler_params=_params("arbitrary", "arbitrary"),
        name="kv_up",
    )(ckv, kpe_pad, w_uk, w_uvt, g_n, g_r, k_aug, cos, sin)


def _mla_flash_kernel(*refs, tq, td, hg, bounded, n_riders):
    q_ref, k_ref, vt_ref = refs[:3]
    o_ref = refs[3 + n_riders]
    m_sc, l_sc, acc_sc = refs[4 + 2 * n_riders:]
    for src, dst in zip(refs[3:3 + n_riders], refs[4 + n_riders:4 + 2 * n_riders]):
        dst[...] = src[...].astype(BF16)
    i = pl.program_id(2)
    if not bounded:
        m_sc[...] = jnp.full(m_sc.shape, -jnp.inf, F32)
    l_sc[...] = jnp.zeros(l_sc.shape, F32)
    acc_sc[...] = jnp.zeros(acc_sc.shape, F32)

    def step(tiles):
        sts = [(hh, t, _dot_nt(k_ref[0, hh, pl.ds(t[0], t[1]), :], q_ref[0, hh, t[2]:, :]))
               for t in tiles for hh in range(hg)]
        for hh, (k0, kn, q0, mask), st in sts:
            if mask is not None:
                st = jnp.where(mask, st, NEG)
            vt = vt_ref[0, hh, :, pl.ds(k0, kn)]
            if bounded:
                p = jnp.exp2(st)
                l_sc[hh, :, q0:] += jnp.sum(p, axis=0, keepdims=True)
                acc_sc[hh, :, q0:] += _dot(vt, p.astype(BF16))
            else:
                m_prev = m_sc[hh, :, q0:]
                m_new = jnp.maximum(m_prev, jnp.max(st, axis=0, keepdims=True))
                alpha = jnp.exp2(m_prev - m_new)
                p = jnp.exp2(st - m_new)
                l_sc[hh, :, q0:] = alpha * l_sc[hh, :, q0:] + jnp.sum(p, axis=0, keepdims=True)
                acc_sc[hh, :, q0:] = acc_sc[hh, :, q0:] * alpha + _dot(vt, p.astype(BF16))
                m_sc[hh, :, q0:] = m_new

    def body(j, carry):
        step([(pl.multiple_of(j * tq, tq), tq, 0, None)])
        return carry

    lax.fori_loop(0, i, body, 0)
    diag = []
    for d in range(tq // td):
        nq = tq - d * td
        q_chunk = lax.broadcasted_iota(jnp.int32, (td, nq), 1) // CHUNK
        k_chunk = lax.broadcasted_iota(jnp.int32, (td, nq), 0) // CHUNK
        diag.append((pl.multiple_of(i * tq + d * td, td), td, d * td, k_chunk <= q_chunk))
    step(diag)
    for hh in range(hg):
        o_ref[0, :, hh * V_MLA:(hh + 1) * V_MLA] = (acc_sc[hh] / l_sc[hh]).T.astype(o_ref.dtype)


def _flash_steps(q, tq, hg):
    B, H, S, _ = q.shape
    return B * (H // hg) * (S // tq)


def _can_ride(a, n_steps):
    return a.shape[0] % n_steps == 0 and (a.shape[0] // n_steps) % BF16_ROWS == 0


def _mla_flash(q, k, vt, *riders, tq, td, hg, bounded):
    B, H, S, _ = q.shape
    nh, nq = H // hg, S // tq
    step = lambda b, h, i: ((b * nh + h) * nq + i, 0)
    rider_spec = lambda a: pl.BlockSpec((a.shape[0] // (B * nh * nq), a.shape[1]), step)
    return pl.pallas_call(
        functools.partial(_mla_flash_kernel, tq=tq, td=td, hg=hg, bounded=bounded, n_riders=len(riders)),
        grid=(B, nh, nq),
        in_specs=[pl.BlockSpec((1, hg, tq, QK_PAD), lambda b, h, i: (b, h, i, 0)),
                  pl.BlockSpec((1, hg, S, QK_PAD), lambda b, h, i: (b, h, 0, 0)),
                  pl.BlockSpec((1, hg, V_MLA, S), lambda b, h, i: (b, h, 0, 0))]
        + [rider_spec(a) for a in riders],
        out_specs=[pl.BlockSpec((1, tq, hg * V_MLA), lambda b, h, i: (b, i, h))]
        + [rider_spec(a) for a in riders],
        out_shape=[jax.ShapeDtypeStruct((B, S, H * V_MLA), BF16)]
        + [jax.ShapeDtypeStruct(a.shape, BF16) for a in riders],
        scratch_shapes=[pltpu.VMEM((hg, 1, tq), F32), pltpu.VMEM((hg, 1, tq), F32),
                        pltpu.VMEM((hg, V_MLA, tq), F32)],
        compiler_params=_params("arbitrary", "arbitrary", "arbitrary"),
        name="mla_flash_bounded" if bounded else "mla_flash",
    )(q, k, vt, *riders)


def _toeplitz(u_row, rows, cols, stride=1):
    u = jnp.broadcast_to(u_row, (rows, u_row.shape[-1]))
    return pltpu.roll(u, 0, 1, stride=stride, stride_axis=0)[:, :cols]


def _softmax_pv(parts, bounded):
    m = None
    if not bounded:
        for s, _ in parts:
            mx = jnp.max(s, axis=-1, keepdims=True)
            m = mx if m is None else jnp.maximum(m, mx)
    l = None
    acc = None
    for s, v in parts:
        p = jnp.exp2(s if bounded else s - m)
        ps = jnp.sum(p, axis=-1, keepdims=True)
        pv = _dot(p.astype(BF16), v)
        l = ps if l is None else l + ps
        acc = pv if acc is None else acc + pv
    return acc / l


def _band_kernel(q_ref, kc_ref, kp_ref, vc_ref, vp_ref, u_ref, o_ref, bc_sc, bp_sc,
                 *, n_heads, splits, bounded):
    b, g = pl.program_id(0), pl.program_id(1)
    tg = bc_sc.shape[1]

    @pl.when((b == 0) & (g == 0))
    def _():
        qc = lax.broadcasted_iota(jnp.int32, (tg, tg), 0) // CHUNK
        kc = lax.broadcasted_iota(jnp.int32, (tg, tg), 1) // CHUNK
        for h in range(n_heads):
            bc_sc[h] = jnp.where(kc <= qc, _toeplitz(u_ref[h, 0:1, :], tg, tg), NEG)
            bp_sc[h] = jnp.where(kc >= qc, _toeplitz(u_ref[h, 1:2, :], tg, tg), NEG)

    no_prev = jnp.where(g == 0, NEG, 0.0)
    tr = tg // splits
    for h0 in range(0, n_heads, 2):
        scores = []
        for h in (h0, h0 + 1):
            sl = slice(h * D_BAND, (h + 1) * D_BAND)
            for a in range(splits):
                rows, old, new = slice(a * tr, (a + 1) * tr), slice(a * tr, tg), slice(0, (a + 1) * tr)
                q = q_ref[0, rows, sl]
                scores.append((rows, old, new, sl,
                               _dot_nt(q, kp_ref[0, old, sl]) + (bp_sc[h, rows, old] + no_prev),
                               _dot_nt(q, kc_ref[0, new, sl]) + bc_sc[h, rows, new]))
        for rows, old, new, sl, s_prev, s_cur in scores:
            parts = [(s_prev, vp_ref[0, old, sl]), (s_cur, vc_ref[0, new, sl])]
            o_ref[0, rows, sl] = _softmax_pv(parts, bounded).astype(o_ref.dtype)


def _band(qb, kb, vb, u_tab, *, bounded):
    B, S, hw = qb.shape
    H = hw // D_BAND
    tg = BAND_PAST
    cur = lambda b, g: (b, g, 0)
    prev = lambda b, g: (b, jnp.maximum(g - 1, 0), 0)
    blk = (1, tg, hw)
    return pl.pallas_call(
        functools.partial(_band_kernel, n_heads=H, splits=4, bounded=bounded),
        grid=(B, S // tg),
        in_specs=[pl.BlockSpec(blk, cur), pl.BlockSpec(blk, cur), pl.BlockSpec(blk, prev),
                  pl.BlockSpec(blk, cur), pl.BlockSpec(blk, prev),
                  _resident(u_tab)],
        out_specs=pl.BlockSpec(blk, cur),
        out_shape=jax.ShapeDtypeStruct((B, S, hw), BF16),
        scratch_shapes=[pltpu.VMEM((H, tg, tg), F32), pltpu.VMEM((H, tg, tg), F32)],
        compiler_params=_params("arbitrary", "arbitrary"),
        name="band_bounded" if bounded else "band",
    )(qb, kb, kb, vb, vb, u_tab)


def _mla_sample_kernel(q_ref, cc_ref, pc_ref, cn_ref, pn_ref, wuk_ref, wuvt_ref, gn_ref, gr_ref,
                       aug_ref, cosc_ref, sinc_ref, cosn_ref, sinn_ref, o_ref, *, n_heads, t_new):
    args = (wuk_ref[...], wuvt_ref[...], gn_ref[...], gr_ref[...], aug_ref[...])
    pc = jnp.concatenate([pc_ref[0], jnp.zeros((pc_ref.shape[1], LANES - ROPE), F32)], axis=-1)
    kc, vtc = _expand_kv(cc_ref[0], pc, *args, cosc_ref[...], sinc_ref[...], n_heads)
    kn, vtn = _expand_kv(cn_ref[0], pn_ref[0], *args, cosn_ref[...], sinn_ref[...], n_heads)
    rows_n = cn_ref.shape[1]
    new_ok = lax.broadcasted_iota(jnp.int32, (q_ref.shape[2], rows_n), 1) < t_new
    scores = [(_dot_nt(q_ref[0, h], kc[h]), jnp.where(new_ok, _dot_nt(q_ref[0, h], kn[h]), NEG))
              for h in range(n_heads)]
    for h, (s_c, s_n) in enumerate(scores):
        m = jnp.maximum(jnp.max(s_c, axis=-1, keepdims=True), jnp.max(s_n, axis=-1, keepdims=True))
        p_c = jnp.exp2(s_c - m)
        p_n = jnp.exp2(s_n - m)
        l = jnp.sum(p_c, axis=-1, keepdims=True) + jnp.sum(p_n, axis=-1, keepdims=True)
        sl = slice(h * V_MLA, (h + 1) * V_MLA)
        pv = (_dot_nt(p_c.astype(BF16), vtc[sl].astype(BF16))
              + _dot_nt(p_n.astype(BF16), vtn[sl].astype(BF16)))
        o_ref[0, :, sl] = (pv / l).astype(o_ref.dtype)


def _mla_sample(q, ckv_c, kpe_c, ckv_n, kpe_n, w_uk, w_uvt, g_n, g_r, k_aug, tabs, *, t_new):
    Bd, H, T, _ = q.shape
    P = ckv_c.shape[1]
    rn = ckv_n.shape[1]
    assert P % rn == 0
    old = lambda: pl.BlockSpec((P, LANES), lambda b: (0, 0), pipeline_mode=pl.Buffered(1))
    new = lambda: pl.BlockSpec((rn, LANES), lambda b: (P // rn, 0), pipeline_mode=pl.Buffered(1))
    bat = lambda b: (b, 0, 0)
    return pl.pallas_call(
        functools.partial(_mla_sample_kernel, n_heads=H, t_new=t_new),
        grid=(Bd,),
        in_specs=[pl.BlockSpec((1, H, T, QK_PAD), lambda b: (b, 0, 0, 0)),
                  pl.BlockSpec((1, P, KV_LORA), bat), pl.BlockSpec((1, P, ROPE), bat),
                  pl.BlockSpec((1, rn, KV_LORA), bat), pl.BlockSpec((1, rn, LANES), bat),
                  _resident(w_uk), _resident(w_uvt), _resident(g_n), _resident(g_r), _resident(k_aug),
                  old(), old(), new(), new()],
        out_specs=pl.BlockSpec((1, T, H * V_MLA), bat),
        out_shape=jax.ShapeDtypeStruct((Bd, T, H * V_MLA), BF16),
        compiler_params=_params("arbitrary"),
        name="mla_sample",
    )(q, ckv_c, kpe_c, ckv_n, kpe_n, w_uk, w_uvt, g_n, g_r, k_aug, *tabs, *tabs)


def _band_sample_kernel(q_ref, kc_ref, vc_ref, kn_ref, vn_ref, wc_ref, wn_ref, o_ref, *, n_heads):
    T = q_ref.shape[1]
    heads = [slice(h * D_BAND, (h + 1) * D_BAND) for h in range(n_heads)]
    q = jnp.concatenate([q_ref[0, :, sl] for sl in heads], axis=0)
    tab = lambda w_ref, cols: jnp.concatenate(
        [_toeplitz(w_ref[h:h + 1, :], T, cols, stride=n_heads) for h in range(n_heads)], axis=0)
    s_c = _dot_nt(q, kc_ref[0].astype(BF16)) + tab(wc_ref, kc_ref.shape[1])
    s_n = _dot_nt(q, kn_ref[0]) + tab(wn_ref, kn_ref.shape[1])
    parts = [(s_c, vc_ref[0].astype(BF16)), (s_n, vn_ref[0])]
    o = _softmax_pv(parts, False).astype(o_ref.dtype)
    for h, sl in enumerate(heads):
        o_ref[0, :, sl] = o[h * T:(h + 1) * T]


def _band_sample(q, k_c, v_c, k_n, v_n, w_c, w_n, *, n_heads):
    Bd = q.shape[0]
    bat = lambda b: (b, 0, 0)
    blk = lambda a: pl.BlockSpec((1,) + a.shape[1:], bat)
    return pl.pallas_call(
        functools.partial(_band_sample_kernel, n_heads=n_heads),
        grid=(Bd,),
        in_specs=[blk(q), blk(k_c), blk(v_c), blk(k_n), blk(v_n), _resident(w_c), _resident(w_n)],
        out_specs=blk(q),
        out_shape=jax.ShapeDtypeStruct(q.shape, BF16),
        compiler_params=_params("arbitrary"),
        name="band_sample",
    )(q, k_c, v_c, k_n, v_n, w_c, w_n)


def _merge_kernel(x_ref, oa_ref, ob_ref, wo_ref, h_ref):
    half = oa_ref.shape[-1]
    h_ref[0] = x_ref[0] + _dot(oa_ref[0], wo_ref[:half, :]) + _dot(ob_ref[0], wo_ref[half:, :])


def _merge(x, oa, ob, w_o, *, tm):
    B, S, D = x.shape
    row = lambda b, t: (b, t, 0)
    return pl.pallas_call(
        _merge_kernel,
        grid=(B, S // tm),
        in_specs=[pl.BlockSpec((1, tm, D), row), pl.BlockSpec((1, tm, oa.shape[-1]), row),
                  pl.BlockSpec((1, tm, ob.shape[-1]), row),
                  _resident(w_o)],
        out_specs=pl.BlockSpec((1, tm, D), row),
        out_shape=jax.ShapeDtypeStruct((B, S, D), F32),
        compiler_params=_params("arbitrary", "arbitrary"),
        name="merge",
    )(x, oa, ob, w_o)


def _ffn_kernel(h_ref, g_ref, wup_ref, wdn_ref, y_ref, hn_sc):
    j = pl.program_id(2)

    @pl.when(j == 0)
    def _():
        h = h_ref[0]
        hn_sc[...] = _rms(h, g_ref[...]).astype(BF16)
        y_ref[0] = h

    u = jnp.maximum(_dot(hn_sc[...], wup_ref[...]), 0.0)
    y_ref[0] += _dot((u * u).astype(BF16), wdn_ref[...])


def _ffn(h, g, w_up, w_down, *, tm, tf):
    B, S, D = h.shape
    F = w_up.shape[1]
    row = lambda b, t, j: (b, t, 0)
    return pl.pallas_call(
        _ffn_kernel,
        grid=(B, S // tm, F // tf),
        in_specs=[pl.BlockSpec((1, tm, D), row), _resident(g),
                  pl.BlockSpec((D, tf), lambda b, t, j: (0, j)),
                  pl.BlockSpec((tf, D), lambda b, t, j: (j, 0))],
        out_specs=pl.BlockSpec((1, tm, D), row),
        out_shape=jax.ShapeDtypeStruct((B, S, D), F32),
        scratch_shapes=[pltpu.VMEM((tm, D), BF16)],
        compiler_params=_params("arbitrary", "arbitrary", "arbitrary"),
        name="ffn",
    )(h, g, w_up, w_down)


def _rope_tables(pos):
    inv = 1.0 / (ROPE_BASE ** (jnp.arange(0, ROPE, 2, dtype=F32) / ROPE))
    ang = pos.astype(F32)[:, None] * inv[None, :]
    c, s = lax.optimization_barrier((jnp.cos(ang), jnp.sin(ang)))
    z = jnp.zeros((pos.shape[0], LANES - ROPE), F32)
    return ((jnp.concatenate([c, c, z], axis=1), jnp.concatenate([-s, s, z], axis=1)),
            (jnp.concatenate([c, c, c, c], axis=1), jnp.concatenate([-s, s, -s, s], axis=1)))


def _pad_lanes(a, width=LANES):
    return jnp.pad(a, [(0, 0)] * (a.ndim - 1) + [(0, width - a.shape[-1])])


def _pad_rows(a, rows):
    return jnp.pad(a, [(0, 0), (0, rows - a.shape[1]), (0, 0)])


def _swap_halves(a):
    return jnp.concatenate([a[..., HALF_ROPE:], a[..., :HALF_ROPE]], axis=-1)


def _prep_weights(w_in, w_uq, w_uk, w_uv):
    hb3 = (w_in.shape[1] - Q_LORA - KV_LORA - ROPE) // 3
    o = [0, Q_LORA, Q_LORA + KV_LORA, Q_LORA + KV_LORA + ROPE]
    o += [o[3] + hb3, o[3] + 2 * hb3, o[3] + 3 * hb3]
    names = ("cq", "ckv", "kpe", "qb", "kb", "vb")
    w = {n: w_in[:, o[i]:o[i + 1]].astype(BF16) for i, n in enumerate(names)}
    w["kpe"] = _pad_lanes(w["kpe"])
    ha = w_uq.shape[1] // QK_MLA
    uq = w_uq.astype(BF16).reshape(Q_LORA, ha, QK_MLA)
    nope = uq[:, :, :NOPE].reshape(Q_LORA, ha * NOPE)
    rope = uq[:, :, NOPE:]
    w["uq"] = jnp.concatenate(
        [nope, rope.reshape(Q_LORA, ha * ROPE), _swap_halves(rope).reshape(Q_LORA, ha * ROPE)], axis=1)
    w["uk"] = w_uk.astype(BF16)
    w["uvt"] = w_uv.astype(BF16).T
    return w


def _softmax_setup(g_qa, g_ka, g_qb, g_kb, rel_bias):
    c_a = QK_MLA ** -0.5 * LOG2E
    c_b = D_BAND ** -0.5 * LOG2E
    amax = lambda a: jnp.max(jnp.abs(a))
    bound_a = c_a * QK_MLA * amax(g_qa) * amax(g_ka) * BOUND_MARGIN
    bound_b = c_b * D_BAND * amax(g_qb) * amax(g_kb) * BOUND_MARGIN + LOG2E * amax(rel_bias)
    fast_a = bound_a <= FAST_LIMIT
    fast_b = bound_b <= FAST_LIMIT
    shift_a = jnp.where(fast_a, bound_a, 0.0)
    shift_b = jnp.where(fast_b, bound_b, 0.0)
    lane = jnp.arange(LANES)[None, :]
    twice = lambda a: jnp.concatenate([a, a], axis=-1)
    g = {"qa_n": g_qa[None, :NOPE] * c_a, "qa_r": twice(g_qa[None, NOPE:]) * c_a,
         "qa_rs": twice(_swap_halves(g_qa[None, NOPE:])) * c_a,
         "q_aug": (lane == AUG_LANE).astype(F32),
         "ka_n": g_ka[None, :NOPE], "ka_r": _pad_lanes(g_ka[None, NOPE:]),
         "k_aug": jnp.where(lane == AUG_LANE, -shift_a, 0.0).astype(F32),
         "qb": g_qb * c_b}
    return g, fast_a, fast_b, shift_b


def _band_vectors(rel_bias, shift):
    t = BAND_PAST
    e = np.arange(2 * t)
    e = np.where(e < t, e, e - 2 * t)
    idx = np.stack([np.clip(-e, -MAX_REL, MAX_REL), np.clip(t - e, -MAX_REL, MAX_REL)]) + MAX_REL
    return (rel_bias[:, idx] * LOG2E - shift).astype(F32)


def _band_sample_vectors(rel_bias, lb):
    H = rel_bias.shape[0]

    def interleave(dist, n_pos, n):
        e = np.arange(n)
        e = np.where(e < n_pos, e, e - n)
        u = rel_bias[:, np.clip(dist - e, -MAX_REL, MAX_REL) + MAX_REL] * LOG2E
        own = np.arange(H)[:, None, None] == np.arange(H)[None, None, :]
        return jnp.where(own, u[:, :, None], NEG).reshape(H, n * H).astype(F32)

    return interleave(lb, lb, lb + LANES), interleave(0, CHUNK, 2 * CHUNK)


def _row_tile(n, pref):
    return pref if n % pref == 0 else n


def kernel(x_prompt, x_sample, cache_mla_ckv, cache_mla_kpe, cache_band_k, cache_band_v,
           norm_mix, w_in, g_cq, w_uq, g_ckv, w_uk, w_uv, g_qa, g_ka, g_qb, g_kb, rel_bias,
           w_o, norm_ffn, w_up, w_down):
    depth = w_in.shape[0]
    assert depth == 1, "single-layer step"
    B, S, D = x_prompt.shape
    Bd, T, _ = x_sample.shape
    P = cache_mla_ckv.shape[2]
    Lb = cache_band_k.shape[2]
    keep_p = min(BAND_PAST, S)
    assert S % BAND_PAST == 0 and T <= CHUNK

    w = _prep_weights(w_in[0], w_uq[0], w_uk[0], w_uv[0])
    g, fast_a, fast_b, shift_b = _softmax_setup(g_qa[0], g_ka[0], g_qb, g_kb, rel_bias[0])
    ha = w["uk"].shape[1] // NOPE
    hb = rel_bias.shape[1]
    hw = hb * D_BAND
    gains = (g_cq, g_ckv, g["qb"], g_kb)
    q_args = (w["uq"], g["qa_n"], g["qa_r"], g["qa_rs"], g["q_aug"])
    kv_args = (w["uk"], w["uvt"], g["ka_n"], g["ka_r"], g["k_aug"])

    cqn, ckv, kpe_pad, kpe, qb, kb, vb, kb_tail, vb_tail = _proj(
        x_prompt, norm_mix, w, *gains, keep=keep_p, tm=_row_tile(S, 512))
    rn = LANES
    tabs, tabs_q = _rope_tables(jnp.arange(max(S, P + rn), dtype=jnp.int32))
    tmq = _row_tile(S, 512)
    q = _q_up(cqn, *q_args, *tabs_q, tm=tmq)
    k, vt = _kv_up(ckv, kpe_pad, *kv_args, *tabs, tm=_row_tile(S, 1024))
    tqa = _row_tile(S, 1024)
    flash = lambda bounded: functools.partial(_mla_flash, tq=tqa, td=min(tqa, 512), hg=2, bounded=bounded)
    late = (w_o[0], w_up[0], w_down[0])
    ride = [_can_ride(a, _flash_steps(q, tqa, 2)) for a in late]
    oa, *cast = lax.cond(fast_a, flash(True), flash(False), q, k, vt, *(a for a, r in zip(late, ride) if r))
    cast = iter(cast)
    w["o"], w["up"], w["down"] = (next(cast) if r else a.astype(BF16) for a, r in zip(late, ride))
    u_tab = _band_vectors(rel_bias[0], shift_b)
    ob = lax.cond(fast_b, functools.partial(_band, bounded=True),
                  functools.partial(_band, bounded=False), qb, kb, vb, u_tab)
    h = _merge(x_prompt, oa, ob, w["o"], tm=_row_tile(S, 512))
    y_prompt = _ffn(h, norm_ffn, w["up"], w["down"], tm=_row_tile(S, 512), tf=2048)

    n_s = Bd * T
    xs = x_sample.reshape(1, n_s, D)
    cqn_s, ckv_s, kpe_pad_s, kpe_s, qb_s, kb_s, vb_s, kb_s32, vb_s32 = _proj(
        xs, norm_mix, w, *gains, keep=n_s, tm=_row_tile(n_s, 256))
    q_s = _q_up(cqn_s.reshape(Bd, T, Q_LORA), *q_args, *tabs_q, tm=T, pos0=P)
    oa_s = _mla_sample(
        q_s, cache_mla_ckv[0], cache_mla_kpe[0],
        _pad_rows(ckv_s.reshape(Bd, T, KV_LORA), rn), _pad_rows(kpe_pad_s.reshape(Bd, T, LANES), rn),
        *kv_args, tabs, t_new=T)
    w_c, w_n = _band_sample_vectors(rel_bias[0], Lb)
    ob_s = _band_sample(
        qb_s.reshape(Bd, T, hw), cache_band_k.reshape(Bd, Lb * hb, D_BAND), cache_band_v.reshape(Bd, Lb * hb, D_BAND),
        kb_s.reshape(Bd, T * hb, D_BAND), vb_s.reshape(Bd, T * hb, D_BAND), w_c, w_n, n_heads=hb)
    h_s = _merge(xs, oa_s.reshape(1, n_s, ha * V_MLA), ob_s.reshape(1, n_s, hw), w["o"], tm=n_s)
    y_sample = _ffn(h_s, norm_ffn, w["up"], w["down"], tm=n_s, tf=2048).reshape(Bd, T, D)

    return (y_prompt, y_sample,
            ckv[None], kpe[None],
            kb_tail.reshape(B, keep_p, hb, D_BAND)[None], vb_tail.reshape(B, keep_p, hb, D_BAND)[None],
            ckv_s.reshape(Bd, T, KV_LORA)[None], kpe_s.reshape(Bd, T, ROPE)[None],
            kb_s32.reshape(Bd, T, hb, D_BAND)[None], vb_s32.reshape(Bd, T, hb, D_BAND)[None])
```

```python
import functools

import jax
import jax.numpy as jnp
import numpy as np
from jax import lax
from jax.experimental import pallas as pl
from jax.experimental.pallas import tpu as pltpu

CHUNK = 64
EPS = 1e-6
NOPE = 128
ROPE = 64
HALF_ROPE = ROPE // 2
QK_MLA = NOPE + ROPE
V_MLA = 128
D_BAND = 128
BAND_CHUNKS = 8
BAND_PAST = BAND_CHUNKS * CHUNK
MAX_REL = 128
ROPE_BASE = 10000.0
Q_LORA = 512
KV_LORA = 256
NEG = -1e30
LOG2E = 1.4426950408889634

BOUND_MARGIN = 1.02
FAST_LIMIT = 60.0

LANES = 128
BF16_ROWS = 16
QK_PAD = 2 * LANES
AUG_LANE = ROPE
VMEM_LIMIT = 56 * 1024 * 1024

BF16 = jnp.bfloat16
F32 = jnp.float32

NT_DIMS = (((1,), (1,)), ((), ()))


def _params(*sem):
    return pltpu.CompilerParams(dimension_semantics=sem, vmem_limit_bytes=VMEM_LIMIT)


def _resident(a):
    return pl.BlockSpec(a.shape, lambda *_: (0,) * a.ndim, pipeline_mode=pl.Buffered(1))


def _rms(x, g):
    ms = jnp.mean(x * x, axis=-1, keepdims=True)
    return x * lax.rsqrt(ms + EPS) * g


def _dot(a, b):
    return jnp.dot(a, b, preferred_element_type=F32)


def _dot_nt(a, b):
    return lax.dot_general(a, b, NT_DIMS, preferred_element_type=F32)


def _proj_kernel(x_ref, nm_ref, wcq_ref, wckv_ref, wkpe_ref, wqb_ref, wkb_ref, wvb_ref,
                 gcq_ref, gckv_ref, gqb_ref, gkb_ref,
                 cqn_ref, ckv_ref, kpe_ref, kpe_out_ref, qb_ref, kb_ref, vb_ref, kbt_ref, vbt_ref,
                 *, n_heads, tail_start):
    t = pl.program_id(1)
    xn = _rms(x_ref[0], nm_ref[...]).astype(BF16)
    cqn_ref[0] = _rms(_dot(xn, wcq_ref[...]), gcq_ref[...]).astype(BF16)
    ckv_ref[0] = _rms(_dot(xn, wckv_ref[...]), gckv_ref[...])
    kpe = _dot(xn, wkpe_ref[...])
    kpe_ref[0] = kpe
    kpe_out_ref[0] = kpe[:, :ROPE]
    qb = _dot(xn, wqb_ref[...])
    kb = _dot(xn, wkb_ref[...])
    vb = _dot(xn, wvb_ref[...])
    vb_ref[0] = vb.astype(BF16)
    kbn = []
    for h in range(n_heads):
        sl = slice(h * D_BAND, (h + 1) * D_BAND)
        qb_ref[0, :, sl] = _rms(qb[:, sl], gqb_ref[...]).astype(BF16)
        kbn.append(_rms(kb[:, sl], gkb_ref[...]))
    kbn = jnp.concatenate(kbn, axis=-1)
    kb_ref[0] = kbn.astype(BF16)

    @pl.when(t >= tail_start)
    def _():
        kbt_ref[0] = kbn
        vbt_ref[0] = vb


def _proj(x, norm_mix, w, g_cq, g_ckv, g_qb, g_kb, *, keep, tm):
    B, S, D = x.shape
    n_heads = w["qb"].shape[1] // D_BAND
    hw = n_heads * D_BAND
    nt = S // tm
    tail_start = nt - keep // tm
    row = lambda b, t: (b, t, 0)
    tail = lambda b, t: (b, jnp.maximum(t - tail_start, 0), 0)
    out_shapes = (
        jax.ShapeDtypeStruct((B, S, Q_LORA), BF16),
        jax.ShapeDtypeStruct((B, S, KV_LORA), F32),
        jax.ShapeDtypeStruct((B, S, LANES), F32),
        jax.ShapeDtypeStruct((B, S, ROPE), F32),
        jax.ShapeDtypeStruct((B, S, hw), BF16),
        jax.ShapeDtypeStruct((B, S, hw), BF16),
        jax.ShapeDtypeStruct((B, S, hw), BF16),
        jax.ShapeDtypeStruct((B, keep, hw), F32),
        jax.ShapeDtypeStruct((B, keep, hw), F32),
    )
    out_specs = (
        pl.BlockSpec((1, tm, Q_LORA), row),
        pl.BlockSpec((1, tm, KV_LORA), row),
        pl.BlockSpec((1, tm, LANES), row),
        pl.BlockSpec((1, tm, ROPE), row),
        pl.BlockSpec((1, tm, hw), row),
        pl.BlockSpec((1, tm, hw), row),
        pl.BlockSpec((1, tm, hw), row),
        pl.BlockSpec((1, tm, hw), tail),
        pl.BlockSpec((1, tm, hw), tail),
    )
    weights = (w["cq"], w["ckv"], w["kpe"], w["qb"], w["kb"], w["vb"])
    gains = (g_cq, g_ckv, g_qb, g_kb)
    return pl.pallas_call(
        functools.partial(_proj_kernel, n_heads=n_heads, tail_start=tail_start),
        grid=(B, nt),
        in_specs=[pl.BlockSpec((1, tm, D), row), _resident(norm_mix)]
        + [_resident(a) for a in weights] + [_resident(a) for a in gains],
        out_specs=out_specs,
        out_shape=out_shapes,
        compiler_params=_params("arbitrary", "arbitrary"),
        name="proj",
    )(x, norm_mix, *weights, *gains)


def _q_up_kernel(cqn_ref, w_ref, gn_ref, gr_ref, grs_ref, aug_ref, cos_ref, sin_ref, q_ref,
                 *, n_heads):
    hw = n_heads * LANES
    hp = hw // 2
    y = _dot(cqn_ref[0], w_ref[...])
    low = lax.broadcasted_iota(jnp.int32, (1, LANES), 1) < ROPE
    for j in range(n_heads // 2):
        r = y[:, hw + j * LANES: hw + (j + 1) * LANES]
        rs = y[:, hw + hp + j * LANES: hw + hp + (j + 1) * LANES]
        r2 = r * r
        r2_of = (jnp.where(low, r2, 0.0), jnp.where(low, 0.0, r2))
        rinv = []
        for e in range(2):
            h = 2 * j + e
            n = y[:, h * LANES:(h + 1) * LANES]
            ss = jnp.sum(n * n + r2_of[e], axis=-1, keepdims=True)
            rinv.append(lax.rsqrt(ss * (1.0 / QK_MLA) + EPS))
            q_ref[0, h, :, 0:LANES] = (n * rinv[e] * gn_ref[...]).astype(BF16)
        rot = ((r * gr_ref[...]) * cos_ref[...] + (rs * grs_ref[...]) * sin_ref[...]) \
            * jnp.where(low, rinv[0], rinv[1])
        q_ref[0, 2 * j, :, LANES:QK_PAD] = (jnp.where(low, rot, 0.0) + aug_ref[...]).astype(BF16)
        q_ref[0, 2 * j + 1, :, LANES:QK_PAD] = (
            jnp.where(low, pltpu.roll(rot, ROPE, 1), 0.0) + aug_ref[...]).astype(BF16)


def _q_up(cqn, w, g_n, g_r, g_rs, aug, cos, sin, *, tm, pos0=0):
    B, S, _ = cqn.shape
    n_heads = w.shape[1] // (2 * LANES)
    assert pos0 % tm == 0
    row = lambda b, t: (b, t, 0)
    pos = lambda b, t: (pos0 // tm + t, 0)
    return pl.pallas_call(
        functools.partial(_q_up_kernel, n_heads=n_heads),
        grid=(B, S // tm),
        in_specs=[pl.BlockSpec((1, tm, Q_LORA), row), _resident(w),
                  _resident(g_n), _resident(g_r), _resident(g_rs), _resident(aug),
                  pl.BlockSpec((tm, LANES), pos), pl.BlockSpec((tm, LANES), pos)],
        out_specs=pl.BlockSpec((1, n_heads, tm, QK_PAD), lambda b, t: (b, 0, t, 0)),
        out_shape=jax.ShapeDtypeStruct((B, n_heads, S, QK_PAD), BF16),
        compiler_params=_params("arbitrary", "arbitrary"),
        name="q_up",
    )(cqn, w, g_n, g_r, g_rs, aug, cos, sin)


def _expand_kv(c, kpe, wuk, wuvt, g_n, g_r, k_aug, cos, sin, n_heads):
    cb = c.astype(BF16)
    kn = _dot(cb, wuk)
    vt = _dot_nt(wuvt, cb)
    a = kpe * g_r
    sspe = jnp.sum(kpe * kpe, axis=-1, keepdims=True)
    a_sw = pltpu.roll(a, HALF_ROPE, 1) + pltpu.roll(a, LANES - HALF_ROPE, 1)
    low = lax.broadcasted_iota(jnp.int32, (1, LANES), 1) < ROPE
    rot = jnp.where(low, a * cos + a_sw * sin, 0.0)
    keys = []
    for h in range(n_heads):
        n = kn[:, h * NOPE:(h + 1) * NOPE]
        ss = jnp.sum(n * n, axis=-1, keepdims=True) + sspe
        rinv = lax.rsqrt(ss * (1.0 / QK_MLA) + EPS)
        keys.append(jnp.concatenate([(n * rinv * g_n).astype(BF16),
                                     (rot * rinv + k_aug).astype(BF16)], axis=-1))
    return keys, vt


def _kv_up_kernel(ckv_ref, kpe_ref, wuk_ref, wuvt_ref, gn_ref, gr_ref, aug_ref, cos_ref, sin_ref,
                  k_ref, vt_ref, *, n_heads):
    keys, vt = _expand_kv(ckv_ref[0], kpe_ref[0], wuk_ref[...], wuvt_ref[...], gn_ref[...],
                          gr_ref[...], aug_ref[...], cos_ref[...], sin_ref[...], n_heads)
    for h in range(n_heads):
        k_ref[0, h] = keys[h]
        vt_ref[0, h] = vt[h * V_MLA:(h + 1) * V_MLA, :].astype(BF16)


def _kv_up(ckv, kpe_pad, w_uk, w_uvt, g_n, g_r, k_aug, cos, sin, *, tm):
    B, S, _ = ckv.shape
    n_heads = w_uk.shape[1] // NOPE
    row = lambda b, t: (b, t, 0)
    pos = lambda b, t: (t, 0)
    return pl.pallas_call(
        functools.partial(_kv_up_kernel, n_heads=n_heads),
        grid=(B, S // tm),
        in_specs=[pl.BlockSpec((1, tm, KV_LORA), row), pl.BlockSpec((1, tm, LANES), row),
                  _resident(w_uk), _resident(w_uvt),
                  _resident(g_n), _resident(g_r), _resident(k_aug),
                  pl.BlockSpec((tm, LANES), pos), pl.BlockSpec((tm, LANES), pos)],
        out_specs=(pl.BlockSpec((1, n_heads, tm, QK_PAD), lambda b, t: (b, 0, t, 0)),
                   pl.BlockSpec((1, n_heads, V_MLA, tm), lambda b, t: (b, 0, 0, t))),
        out_shape=(jax.ShapeDtypeStruct((B, n_heads, S, QK_PAD), BF16),
                   jax.ShapeDtypeStruct((B, n_heads, V_MLA, S), BF16)),
        compiler_params=_params("arbitrary", "arbitrary"),
        name="kv_up",
    )(ckv, kpe_pad, w_uk, w_uvt, g_n, g_r, k_aug, cos, sin)


def _mla_flash_kernel(*refs, tq, td, hg, bounded, n_riders):
    q_ref, k_ref, vt_ref = refs[:3]
    o_ref = refs[3 + n_riders]
    m_sc, l_sc, acc_sc = refs[4 + 2 * n_riders:]
    for src, dst in zip(refs[3:3 + n_riders], refs[4 + n_riders:4 + 2 * n_riders]):
        dst[...] = src[...].astype(BF16)
    i = pl.program_id(2)
    if not bounded:
        m_sc[...] = jnp.full(m_sc.shape, -jnp.inf, F32)
    l_sc[...] = jnp.zeros(l_sc.shape, F32)
    acc_sc[...] = jnp.zeros(acc_sc.shape, F32)

    def step(tiles):
        sts = [(hh, t, _dot_nt(k_ref[0, hh, pl.ds(t[0], t[1]), :], q_ref[0, hh, t[2]:, :]))
               for t in tiles for hh in range(hg)]
        for hh, (k0, kn, q0, mask), st in sts:
            if mask is not None:
                st = jnp.where(mask, st, NEG)
            vt = vt_ref[0, hh, :, pl.ds(k0, kn)]
            if bounded:
                p = jnp.exp2(st)
                l_sc[hh, :, q0:] += jnp.sum(p, axis=0, keepdims=True)
                acc_sc[hh, :, q0:] += _dot(vt, p.astype(BF16))
            else:
                m_prev = m_sc[hh, :, q0:]
                m_new = jnp.maximum(m_prev, jnp.max(st, axis=0, keepdims=True))
                alpha = jnp.exp2(m_prev - m_new)
                p = jnp.exp2(st - m_new)
                l_sc[hh, :, q0:] = alpha * l_sc[hh, :, q0:] + jnp.sum(p, axis=0, keepdims=True)
                acc_sc[hh, :, q0:] = acc_sc[hh, :, q0:] * alpha + _dot(vt, p.astype(BF16))
                m_sc[hh, :, q0:] = m_new

    def body(j, carry):
        step([(pl.multiple_of(j * tq, tq), tq, 0, None)])
        return carry

    lax.fori_loop(0, i, body, 0)
    diag = []
    for d in range(tq // td):
        nq = tq - d * td
        q_chunk = lax.broadcasted_iota(jnp.int32, (td, nq), 1) // CHUNK
        k_chunk = lax.broadcasted_iota(jnp.int32, (td, nq), 0) // CHUNK
        diag.append((pl.multiple_of(i * tq + d * td, td), td, d * td, k_chunk <= q_chunk))
    step(diag)
    for hh in range(hg):
        o_ref[0, :, hh * V_MLA:(hh + 1) * V_MLA] = (acc_sc[hh] / l_sc[hh]).T.astype(o_ref.dtype)


def _flash_steps(q, tq, hg):
    B, H, S, _ = q.shape
    return B * (H // hg) * (S // tq)


def _can_ride(a, n_steps):
    return a.shape[0] % n_steps == 0 and (a.shape[0] // n_steps) % BF16_ROWS == 0


def _mla_flash(q, k, vt, *riders, tq, td, hg, bounded):
    B, H, S, _ = q.shape
    nh, nq = H // hg, S // tq
    step = lambda b, h, i: ((b * nh + h) * nq + i, 0)
    rider_spec = lambda a: pl.BlockSpec((a.shape[0] // (B * nh * nq), a.shape[1]), step)
    return pl.pallas_call(
        functools.partial(_mla_flash_kernel, tq=tq, td=td, hg=hg, bounded=bounded, n_riders=len(riders)),
        grid=(B, nh, nq),
        in_specs=[pl.BlockSpec((1, hg, tq, QK_PAD), lambda b, h, i: (b, h, i, 0)),
                  pl.BlockSpec((1, hg, S, QK_PAD), lambda b, h, i: (b, h, 0, 0)),
                  pl.BlockSpec((1, hg, V_MLA, S), lambda b, h, i: (b, h, 0, 0))]
        + [rider_spec(a) for a in riders],
        out_specs=[pl.BlockSpec((1, tq, hg * V_MLA), lambda b, h, i: (b, i, h))]
        + [rider_spec(a) for a in riders],
        out_shape=[jax.ShapeDtypeStruct((B, S, H * V_MLA), BF16)]
        + [jax.ShapeDtypeStruct(a.shape, BF16) for a in riders],
        scratch_shapes=[pltpu.VMEM((hg, 1, tq), F32), pltpu.VMEM((hg, 1, tq), F32),
                        pltpu.VMEM((hg, V_MLA, tq), F32)],
        compiler_params=_params("arbitrary", "arbitrary", "arbitrary"),
        name="mla_flash_bounded" if bounded else "mla_flash",
    )(q, k, vt, *riders)


def _toeplitz(u_row, rows, cols, stride=1):
    u = jnp.broadcast_to(u_row, (rows, u_row.shape[-1]))
    return pltpu.roll(u, 0, 1, stride=stride, stride_axis=0)[:, :cols]


def _softmax_pv(parts, bounded):
    m = None
    if not bounded:
        for s, _ in parts:
            mx = jnp.max(s, axis=-1, keepdims=True)
            m = mx if m is None else jnp.maximum(m, mx)
    l = None
    acc = None
    for s, v in parts:
        p = jnp.exp2(s if bounded else s - m)
        ps = jnp.sum(p, axis=-1, keepdims=True)
        pv = _dot(p.astype(BF16), v)
        l = ps if l is None else l + ps
        acc = pv if acc is None else acc + pv
    return acc / l


def _band_kernel(q_ref, kc_ref, kp_ref, vc_ref, vp_ref, u_ref, o_ref, bc_sc, bp_sc,
                 *, n_heads, splits, bounded):
    b, g = pl.program_id(0), pl.program_id(1)
    tg = bc_sc.shape[1]

    @pl.when((b == 0) & (g == 0))
    def _():
        qc = lax.broadcasted_iota(jnp.int32, (tg, tg), 0) // CHUNK
        kc = lax.broadcasted_iota(jnp.int32, (tg, tg), 1) // CHUNK
        for h in range(n_heads):
            bc_sc[h] = jnp.where(kc <= qc, _toeplitz(u_ref[h, 0:1, :], tg, tg), NEG)
            bp_sc[h] = jnp.where(kc >= qc, _toeplitz(u_ref[h, 1:2, :], tg, tg), NEG)

    no_prev = jnp.where(g == 0, NEG, 0.0)
    tr = tg // splits
    for h0 in range(0, n_heads, 2):
        scores = []
        for h in (h0, h0 + 1):
            sl = slice(h * D_BAND, (h + 1) * D_BAND)
            for a in range(splits):
                rows, old, new = slice(a * tr, (a + 1) * tr), slice(a * tr, tg), slice(0, (a + 1) * tr)
                q = q_ref[0, rows, sl]
                scores.append((rows, old, new, sl,
                               _dot_nt(q, kp_ref[0, old, sl]) + (bp_sc[h, rows, old] + no_prev),
                               _dot_nt(q, kc_ref[0, new, sl]) + bc_sc[h, rows, new]))
        for rows, old, new, sl, s_prev, s_cur in scores:
            parts = [(s_prev, vp_ref[0, old, sl]), (s_cur, vc_ref[0, new, sl])]
            o_ref[0, rows, sl] = _softmax_pv(parts, bounded).astype(o_ref.dtype)


def _band(qb, kb, vb, u_tab, *, bounded):
    B, S, hw = qb.shape
    H = hw // D_BAND
    tg = BAND_PAST
    cur = lambda b, g: (b, g, 0)
    prev = lambda b, g: (b, jnp.maximum(g - 1, 0), 0)
    blk = (1, tg, hw)
    return pl.pallas_call(
        functools.partial(_band_kernel, n_heads=H, splits=4, bounded=bounded),
        grid=(B, S // tg),
        in_specs=[pl.BlockSpec(blk, cur), pl.BlockSpec(blk, cur), pl.BlockSpec(blk, prev),
                  pl.BlockSpec(blk, cur), pl.BlockSpec(blk, prev),
                  _resident(u_tab)],
        out_specs=pl.BlockSpec(blk, cur),
        out_shape=jax.ShapeDtypeStruct((B, S, hw), BF16),
        scratch_shapes=[pltpu.VMEM((H, tg, tg), F32), pltpu.VMEM((H, tg, tg), F32)],
        compiler_params=_params("arbitrary", "arbitrary"),
        name="band_bounded" if bounded else "band",
    )(qb, kb, kb, vb, vb, u_tab)


def _mla_sample_kernel(q_ref, cc_ref, pc_ref, cn_ref, pn_ref, wuk_ref, wuvt_ref, gn_ref, gr_ref,
                       aug_ref, cosc_ref, sinc_ref, cosn_ref, sinn_ref, o_ref, *, n_heads, t_new):
    args = (wuk_ref[...], wuvt_ref[...], gn_ref[...], gr_ref[...], aug_ref[...])
    pc = jnp.concatenate([pc_ref[0], jnp.zeros((pc_ref.shape[1], LANES - ROPE), F32)], axis=-1)
    kc, vtc = _expand_kv(cc_ref[0], pc, *args, cosc_ref[...], sinc_ref[...], n_heads)
    kn, vtn = _expand_kv(cn_ref[0], pn_ref[0], *args, cosn_ref[...], sinn_ref[...], n_heads)
    rows_n = cn_ref.shape[1]
    new_ok = lax.broadcasted_iota(jnp.int32, (q_ref.shape[2], rows_n), 1) < t_new
    scores = [(_dot_nt(q_ref[0, h], kc[h]), jnp.where(new_ok, _dot_nt(q_ref[0, h], kn[h]), NEG))
              for h in range(n_heads)]
    for h, (s_c, s_n) in enumerate(scores):
        m = jnp.maximum(jnp.max(s_c, axis=-1, keepdims=True), jnp.max(s_n, axis=-1, keepdims=True))
        p_c = jnp.exp2(s_c - m)
        p_n = jnp.exp2(s_n - m)
        l = jnp.sum(p_c, axis=-1, keepdims=True) + jnp.sum(p_n, axis=-1, keepdims=True)
        sl = slice(h * V_MLA, (h + 1) * V_MLA)
        pv = (_dot_nt(p_c.astype(BF16), vtc[sl].astype(BF16))
              + _dot_nt(p_n.astype(BF16), vtn[sl].astype(BF16)))
        o_ref[0, :, sl] = (pv / l).astype(o_ref.dtype)


def _mla_sample(q, ckv_c, kpe_c, ckv_n, kpe_n, w_uk, w_uvt, g_n, g_r, k_aug, tabs, *, t_new):
    Bd, H, T, _ = q.shape
    P = ckv_c.shape[1]
    rn = ckv_n.shape[1]
    assert P % rn == 0
    old = lambda: pl.BlockSpec((P, LANES), lambda b: (0, 0), pipeline_mode=pl.Buffered(1))
    new = lambda: pl.BlockSpec((rn, LANES), lambda b: (P // rn, 0), pipeline_mode=pl.Buffered(1))
    bat = lambda b: (b, 0, 0)
    return pl.pallas_call(
        functools.partial(_mla_sample_kernel, n_heads=H, t_new=t_new),
        grid=(Bd,),
        in_specs=[pl.BlockSpec((1, H, T, QK_PAD), lambda b: (b, 0, 0, 0)),
                  pl.BlockSpec((1, P, KV_LORA), bat), pl.BlockSpec((1, P, ROPE), bat),
                  pl.BlockSpec((1, rn, KV_LORA), bat), pl.BlockSpec((1, rn, LANES), bat),
                  _resident(w_uk), _resident(w_uvt), _resident(g_n), _resident(g_r), _resident(k_aug),
                  old(), old(), new(), new()],
        out_specs=pl.BlockSpec((1, T, H * V_MLA), bat),
        out_shape=jax.ShapeDtypeStruct((Bd, T, H * V_MLA), BF16),
        compiler_params=_params("arbitrary"),
        name="mla_sample",
    )(q, ckv_c, kpe_c, ckv_n, kpe_n, w_uk, w_uvt, g_n, g_r, k_aug, *tabs, *tabs)


def _band_sample_kernel(q_ref, kc_ref, vc_ref, kn_ref, vn_ref, wc_ref, wn_ref, o_ref, *, n_heads):
    T = q_ref.shape[1]
    heads = [slice(h * D_BAND, (h + 1) * D_BAND) for h in range(n_heads)]
    q = jnp.concatenate([q_ref[0, :, sl] for sl in heads], axis=0)
    tab = lambda w_ref, cols: jnp.concatenate(
        [_toeplitz(w_ref[h:h + 1, :], T, cols, stride=n_heads) for h in range(n_heads)], axis=0)
    s_c = _dot_nt(q, kc_ref[0].astype(BF16)) + tab(wc_ref, kc_ref.shape[1])
    s_n = _dot_nt(q, kn_ref[0]) + tab(wn_ref, kn_ref.shape[1])
    parts = [(s_c, vc_ref[0].astype(BF16)), (s_n, vn_ref[0])]
    o = _softmax_pv(parts, False).astype(o_ref.dtype)
    for h, sl in enumerate(heads):
        o_ref[0, :, sl] = o[h * T:(h + 1) * T]


def _band_sample(q, k_c, v_c, k_n, v_n, w_c, w_n, *, n_heads):
    Bd = q.shape[0]
    bat = lambda b: (b, 0, 0)
    blk = lambda a: pl.BlockSpec((1,) + a.shape[1:], bat)
    return pl.pallas_call(
        functools.partial(_band_sample_kernel, n_heads=n_heads),
        grid=(Bd,),
        in_specs=[blk(q), blk(k_c), blk(v_c), blk(k_n), blk(v_n), _resident(w_c), _resident(w_n)],
        out_specs=blk(q),
        out_shape=jax.ShapeDtypeStruct(q.shape, BF16),
        compiler_params=_params("arbitrary"),
        name="band_sample",
    )(q, k_c, v_c, k_n, v_n, w_c, w_n)


def _merge_kernel(x_ref, oa_ref, ob_ref, wo_ref, h_ref):
    half = oa_ref.shape[-1]
    h_ref[0] = x_ref[0] + _dot(oa_ref[0], wo_ref[:half, :]) + _dot(ob_ref[0], wo_ref[half:, :])


def _merge(x, oa, ob, w_o, *, tm):
    B, S, D = x.shape
    row = lambda b, t: (b, t, 0)
    return pl.pallas_call(
        _merge_kernel,
        grid=(B, S // tm),
        in_specs=[pl.BlockSpec((1, tm, D), row), pl.BlockSpec((1, tm, oa.shape[-1]), row),
                  pl.BlockSpec((1, tm, ob.shape[-1]), row),
                  _resident(w_o)],
        out_specs=pl.BlockSpec((1, tm, D), row),
        out_shape=jax.ShapeDtypeStruct((B, S, D), F32),
        compiler_params=_params("arbitrary", "arbitrary"),
        name="merge",
    )(x, oa, ob, w_o)


def _ffn_kernel(h_ref, g_ref, wup_ref, wdn_ref, y_ref, hn_sc):
    j = pl.program_id(2)

    @pl.when(j == 0)
    def _():
        h = h_ref[0]
        hn_sc[...] = _rms(h, g_ref[...]).astype(BF16)
        y_ref[0] = h

    u = jnp.maximum(_dot(hn_sc[...], wup_ref[...]), 0.0)
    y_ref[0] += _dot((u * u).astype(BF16), wdn_ref[...])


def _ffn(h, g, w_up, w_down, *, tm, tf):
    B, S, D = h.shape
    F = w_up.shape[1]
    row = lambda b, t, j: (b, t, 0)
    return pl.pallas_call(
        _ffn_kernel,
        grid=(B, S // tm, F // tf),
        in_specs=[pl.BlockSpec((1, tm, D), row), _resident(g),
                  pl.BlockSpec((D, tf), lambda b, t, j: (0, j)),
                  pl.BlockSpec((tf, D), lambda b, t, j: (j, 0))],
        out_specs=pl.BlockSpec((1, tm, D), row),
        out_shape=jax.ShapeDtypeStruct((B, S, D), F32),
        scratch_shapes=[pltpu.VMEM((tm, D), BF16)],
        compiler_params=_params("arbitrary", "arbitrary", "arbitrary"),
        name="ffn",
    )(h, g, w_up, w_down)


def _rope_tables(n_pos):
    inv = 1.0 / (ROPE_BASE ** (jnp.arange(0, ROPE, 2, dtype=F32) / ROPE))
    n_hi = -(-n_pos // CHUNK)
    hi = (jnp.arange(n_hi, dtype=F32) * CHUNK)[:, None, None] * inv
    lo = jnp.arange(CHUNK, dtype=F32)[None, :, None] * inv
    ch, sh, cl, sl = jnp.cos(hi), jnp.sin(hi), jnp.cos(lo), jnp.sin(lo)
    c = (ch * cl - sh * sl).reshape(n_hi * CHUNK, HALF_ROPE)[:n_pos]
    s = (sh * cl + ch * sl).reshape(n_hi * CHUNK, HALF_ROPE)[:n_pos]
    return jnp.concatenate([c, c, c, c], axis=1), jnp.concatenate([-s, s, -s, s], axis=1)


def _pad_lanes(a, width=LANES):
    return jnp.pad(a, [(0, 0)] * (a.ndim - 1) + [(0, width - a.shape[-1])])


def _pad_rows(a, rows):
    return jnp.pad(a, [(0, 0), (0, rows - a.shape[1]), (0, 0)])


def _swap_halves(a):
    return jnp.concatenate([a[..., HALF_ROPE:], a[..., :HALF_ROPE]], axis=-1)


def _prep_weights(w_in, w_uq, w_uk, w_uv):
    hb3 = (w_in.shape[1] - Q_LORA - KV_LORA - ROPE) // 3
    o = [0, Q_LORA, Q_LORA + KV_LORA, Q_LORA + KV_LORA + ROPE]
    o += [o[3] + hb3, o[3] + 2 * hb3, o[3] + 3 * hb3]
    names = ("cq", "ckv", "kpe", "qb", "kb", "vb")
    w = {n: w_in[:, o[i]:o[i + 1]].astype(BF16) for i, n in enumerate(names)}
    w["kpe"] = _pad_lanes(w["kpe"])
    ha = w_uq.shape[1] // QK_MLA
    uq = w_uq.astype(BF16).reshape(Q_LORA, ha, QK_MLA)
    nope = uq[:, :, :NOPE].reshape(Q_LORA, ha * NOPE)
    rope = uq[:, :, NOPE:]
    w["uq"] = jnp.concatenate(
        [nope, rope.reshape(Q_LORA, ha * ROPE), _swap_halves(rope).reshape(Q_LORA, ha * ROPE)], axis=1)
    w["uk"] = w_uk.astype(BF16)
    w["uvt"] = w_uv.astype(BF16).T
    return w


def _softmax_setup(g_qa, g_ka, g_qb, g_kb, rel_bias):
    c_a = QK_MLA ** -0.5 * LOG2E
    c_b = D_BAND ** -0.5 * LOG2E
    amax = lambda a: jnp.max(jnp.abs(a))
    bound_a = c_a * QK_MLA * amax(g_qa) * amax(g_ka) * BOUND_MARGIN
    bound_b = c_b * D_BAND * amax(g_qb) * amax(g_kb) * BOUND_MARGIN + LOG2E * amax(rel_bias)
    fast_a = bound_a <= FAST_LIMIT
    fast_b = bound_b <= FAST_LIMIT
    shift_a = jnp.where(fast_a, bound_a, 0.0)
    shift_b = jnp.where(fast_b, bound_b, 0.0)
    lane = jnp.arange(LANES)[None, :]
    twice = lambda a: jnp.concatenate([a, a], axis=-1)
    g = {"qa_n": g_qa[None, :NOPE] * c_a, "qa_r": twice(g_qa[None, NOPE:]) * c_a,
         "qa_rs": twice(_swap_halves(g_qa[None, NOPE:])) * c_a,
         "q_aug": (lane == AUG_LANE).astype(F32),
         "ka_n": g_ka[None, :NOPE], "ka_r": _pad_lanes(g_ka[None, NOPE:]),
         "k_aug": jnp.where(lane == AUG_LANE, -shift_a, 0.0).astype(F32),
         "qb": g_qb * c_b}
    return g, fast_a, fast_b, shift_b


def _band_vectors(rel_bias, shift):
    t = BAND_PAST
    e = np.arange(2 * t)
    e = np.where(e < t, e, e - 2 * t)
    idx = np.stack([np.clip(-e, -MAX_REL, MAX_REL), np.clip(t - e, -MAX_REL, MAX_REL)]) + MAX_REL
    return (rel_bias[:, idx] * LOG2E - shift).astype(F32)


def _band_sample_vectors(rel_bias, lb):
    H = rel_bias.shape[0]

    def interleave(dist, n_pos, n):
        e = np.arange(n)
        e = np.where(e < n_pos, e, e - n)
        u = rel_bias[:, np.clip(dist - e, -MAX_REL, MAX_REL) + MAX_REL] * LOG2E
        own = np.arange(H)[:, None, None] == np.arange(H)[None, None, :]
        return jnp.where(own, u[:, :, None], NEG).reshape(H, n * H).astype(F32)

    return interleave(lb, lb, lb + LANES), interleave(0, CHUNK, 2 * CHUNK)


def _row_tile(n, pref):
    return pref if n % pref == 0 else n


def kernel(x_prompt, x_sample, cache_mla_ckv, cache_mla_kpe, cache_band_k, cache_band_v,
           norm_mix, w_in, g_cq, w_uq, g_ckv, w_uk, w_uv, g_qa, g_ka, g_qb, g_kb, rel_bias,
           w_o, norm_ffn, w_up, w_down):
    depth = w_in.shape[0]
    assert depth == 1, "single-layer step"
    B, S, D = x_prompt.shape
    Bd, T, _ = x_sample.shape
    P = cache_mla_ckv.shape[2]
    Lb = cache_band_k.shape[2]
    keep_p = min(BAND_PAST, S)
    assert S % BAND_PAST == 0 and T <= CHUNK

    w = _prep_weights(w_in[0], w_uq[0], w_uk[0], w_uv[0])
    g, fast_a, fast_b, shift_b = _softmax_setup(g_qa[0], g_ka[0], g_qb, g_kb, rel_bias[0])
    ha = w["uk"].shape[1] // NOPE
    hb = rel_bias.shape[1]
    hw = hb * D_BAND
    gains = (g_cq, g_ckv, g["qb"], g_kb)
    q_args = (w["uq"], g["qa_n"], g["qa_r"], g["qa_rs"], g["q_aug"])
    kv_args = (w["uk"], w["uvt"], g["ka_n"], g["ka_r"], g["k_aug"])

    cqn, ckv, kpe_pad, kpe, qb, kb, vb, kb_tail, vb_tail = _proj(
        x_prompt, norm_mix, w, *gains, keep=keep_p, tm=_row_tile(S, 512))
    rn = LANES
    tabs = _rope_tables(max(S, P + rn))
    tmq = _row_tile(S, 512)
    q = _q_up(cqn, *q_args, *tabs, tm=tmq)
    k, vt = _kv_up(ckv, kpe_pad, *kv_args, *tabs, tm=_row_tile(S, 1024))
    tqa = _row_tile(S, 1024)
    flash = lambda bounded: functools.partial(_mla_flash, tq=tqa, td=min(tqa, 256), hg=2, bounded=bounded)
    late = (w_o[0], w_up[0], w_down[0])
    ride = [_can_ride(a, _flash_steps(q, tqa, 2)) for a in late]
    oa, *cast = lax.cond(fast_a, flash(True), flash(False), q, k, vt, *(a for a, r in zip(late, ride) if r))
    cast = iter(cast)
    w["o"], w["up"], w["down"] = (next(cast) if r else a.astype(BF16) for a, r in zip(late, ride))
    u_tab = _band_vectors(rel_bias[0], shift_b)
    ob = lax.cond(fast_b, functools.partial(_band, bounded=True),
                  functools.partial(_band, bounded=False), qb, kb, vb, u_tab)
    h = _merge(x_prompt, oa, ob, w["o"], tm=_row_tile(S, 512))
    y_prompt = _ffn(h, norm_ffn, w["up"], w["down"], tm=_row_tile(S, 512), tf=2048)

    n_s = Bd * T
    xs = x_sample.reshape(1, n_s, D)
    cqn_s, ckv_s, kpe_pad_s, kpe_s, qb_s, kb_s, vb_s, kb_s32, vb_s32 = _proj(
        xs, norm_mix, w, *gains, keep=n_s, tm=_row_tile(n_s, 256))
    q_s = _q_up(cqn_s.reshape(Bd, T, Q_LORA), *q_args, *tabs, tm=T, pos0=P)
    oa_s = _mla_sample(
        q_s, cache_mla_ckv[0], cache_mla_kpe[0],
        _pad_rows(ckv_s.reshape(Bd, T, KV_LORA), rn), _pad_rows(kpe_pad_s.reshape(Bd, T, LANES), rn),
        *kv_args, tabs, t_new=T)
    w_c, w_n = _band_sample_vectors(rel_bias[0], Lb)
    ob_s = _band_sample(
        qb_s.reshape(Bd, T, hw), cache_band_k.reshape(Bd, Lb * hb, D_BAND), cache_band_v.reshape(Bd, Lb * hb, D_BAND),
        kb_s.reshape(Bd, T * hb, D_BAND), vb_s.reshape(Bd, T * hb, D_BAND), w_c, w_n, n_heads=hb)
    h_s = _merge(xs, oa_s.reshape(1, n_s, ha * V_MLA), ob_s.reshape(1, n_s, hw), w["o"], tm=n_s)
    y_sample = _ffn(h_s, norm_ffn, w["up"], w["down"], tm=n_s, tf=2048).reshape(Bd, T, D)

    return (y_prompt, y_sample,
            ckv[None], kpe[None],
            kb_tail.reshape(B, keep_p, hb, D_BAND)[None], vb_tail.reshape(B, keep_p, hb, D_BAND)[None],
            ckv_s.reshape(Bd, T, KV_LORA)[None], kpe_s.reshape(Bd, T, ROPE)[None],
            kb_s32.reshape(Bd, T, hb, D_BAND)[None], vb_s32.reshape(Bd, T, hb, D_BAND)[None])
```

```python
import functools

import jax
import jax.numpy as jnp
import numpy as np
from jax import lax
from jax.experimental import pallas as pl
from jax.experimental.pallas import tpu as pltpu

CHUNK = 64
EPS = 1e-6
NOPE = 128
ROPE = 64
HALF_ROPE = ROPE // 2
QK_MLA = NOPE + ROPE
V_MLA = 128
D_BAND = 128
BAND_CHUNKS = 8
BAND_PAST = BAND_CHUNKS * CHUNK
MAX_REL = 128
ROPE_BASE = 10000.0
Q_LORA = 512
KV_LORA = 256
NEG = -1e30
LOG2E = 1.4426950408889634

BOUND_MARGIN = 1.02
FAST_LIMIT = 60.0

LANES = 128
BF16_ROWS = 16
QK_PAD = 2 * LANES
AUG_LANE = ROPE
VMEM_LIMIT = 56 * 1024 * 1024

BF16 = jnp.bfloat16
F32 = jnp.float32

NT_DIMS = (((1,), (1,)), ((), ()))


def _params(*sem):
    return pltpu.CompilerParams(dimension_semantics=sem, vmem_limit_bytes=VMEM_LIMIT)


def _resident(a):
    return pl.BlockSpec(a.shape, lambda *_: (0,) * a.ndim, pipeline_mode=pl.Buffered(1))


def _rms(x, g):
    ms = jnp.mean(x * x, axis=-1, keepdims=True)
    return x * lax.rsqrt(ms + EPS) * g


def _dot(a, b):
    return jnp.dot(a, b, preferred_element_type=F32)


def _dot_nt(a, b):
    return lax.dot_general(a, b, NT_DIMS, preferred_element_type=F32)


N_PROJ_IN, N_PROJ_OUT, N_KV_IN = 12, 9, 7


def _proj_kernel(*refs, n_heads, n_mla_heads, tail_start):
    (x_ref, nm_ref, wcq_ref, wckv_ref, wkpe_ref, wqb_ref, wkb_ref, wvb_ref,
     gcq_ref, gckv_ref, gqb_ref, gkb_ref) = refs[:N_PROJ_IN]
    n_kv = N_KV_IN if n_mla_heads else 0
    outs = refs[N_PROJ_IN + n_kv:]
    cqn_ref, ckv_ref, kpe_ref, kpe_out_ref, qb_ref, kb_ref, vb_ref, kbt_ref, vbt_ref = outs[:N_PROJ_OUT]
    t = pl.program_id(1)
    xn = _rms(x_ref[0], nm_ref[...]).astype(BF16)
    cqn_ref[0] = _rms(_dot(xn, wcq_ref[...]), gcq_ref[...]).astype(BF16)
    ckv = _rms(_dot(xn, wckv_ref[...]), gckv_ref[...])
    ckv_ref[0] = ckv
    kpe = _dot(xn, wkpe_ref[...])
    kpe_ref[0] = kpe
    kpe_out_ref[0] = kpe[:, :ROPE]
    if n_mla_heads:
        wuk_ref, wuvt_ref, gn_ref, gr_ref, aug_ref, cos_ref, sin_ref = refs[N_PROJ_IN:N_PROJ_IN + n_kv]
        k_ref, vt_ref = outs[N_PROJ_OUT:]
        keys, vt = _expand_kv(ckv, kpe, wuk_ref[...], wuvt_ref[...], gn_ref[...], gr_ref[...],
                              aug_ref[...], cos_ref[...], sin_ref[...], n_mla_heads)
        for h in range(n_mla_heads):
            k_ref[0, h] = keys[h]
            vt_ref[0, h] = vt[h * V_MLA:(h + 1) * V_MLA, :].astype(BF16)
    qb = _dot(xn, wqb_ref[...])
    kb = _dot(xn, wkb_ref[...])
    vb = _dot(xn, wvb_ref[...])
    vb_ref[0] = vb.astype(BF16)
    kbn = []
    for h in range(n_heads):
        sl = slice(h * D_BAND, (h + 1) * D_BAND)
        qb_ref[0, :, sl] = _rms(qb[:, sl], gqb_ref[...]).astype(BF16)
        kbn.append(_rms(kb[:, sl], gkb_ref[...]))
    kbn = jnp.concatenate(kbn, axis=-1)
    kb_ref[0] = kbn.astype(BF16)

    @pl.when(t >= tail_start)
    def _():
        kbt_ref[0] = kbn
        vbt_ref[0] = vb


def _proj(x, norm_mix, w, g_cq, g_ckv, g_qb, g_kb, *, keep, tm, kv=None):
    B, S, D = x.shape
    n_heads = w["qb"].shape[1] // D_BAND
    hw = n_heads * D_BAND
    nt = S // tm
    tail_start = nt - keep // tm
    row = lambda b, t: (b, t, 0)
    tail = lambda b, t: (b, jnp.maximum(t - tail_start, 0), 0)
    out_shapes = (
        jax.ShapeDtypeStruct((B, S, Q_LORA), BF16),
        jax.ShapeDtypeStruct((B, S, KV_LORA), F32),
        jax.ShapeDtypeStruct((B, S, LANES), F32),
        jax.ShapeDtypeStruct((B, S, ROPE), F32),
        jax.ShapeDtypeStruct((B, S, hw), BF16),
        jax.ShapeDtypeStruct((B, S, hw), BF16),
        jax.ShapeDtypeStruct((B, S, hw), BF16),
        jax.ShapeDtypeStruct((B, keep, hw), F32),
        jax.ShapeDtypeStruct((B, keep, hw), F32),
    )
    out_specs = (
        pl.BlockSpec((1, tm, Q_LORA), row),
        pl.BlockSpec((1, tm, KV_LORA), row),
        pl.BlockSpec((1, tm, LANES), row),
        pl.BlockSpec((1, tm, ROPE), row),
        pl.BlockSpec((1, tm, hw), row),
        pl.BlockSpec((1, tm, hw), row),
        pl.BlockSpec((1, tm, hw), row),
        pl.BlockSpec((1, tm, hw), tail),
        pl.BlockSpec((1, tm, hw), tail),
    )
    weights = (w["cq"], w["ckv"], w["kpe"], w["qb"], w["kb"], w["vb"])
    gains = (g_cq, g_ckv, g_qb, g_kb)
    kv_specs, kv_args, ha = [], (), 0
    if kv is not None:
        ha = kv[0].shape[1] // NOPE
        pos = lambda b, t: (t, 0)
        kv_specs = [_resident(a) for a in kv[:5]] + [pl.BlockSpec((tm, LANES), pos)] * 2
        kv_args = tuple(kv)
        out_shapes += (jax.ShapeDtypeStruct((B, ha, S, QK_PAD), BF16),
                       jax.ShapeDtypeStruct((B, ha, V_MLA, S), BF16))
        out_specs += (pl.BlockSpec((1, ha, tm, QK_PAD), lambda b, t: (b, 0, t, 0)),
                      pl.BlockSpec((1, ha, V_MLA, tm), lambda b, t: (b, 0, 0, t)))
    return pl.pallas_call(
        functools.partial(_proj_kernel, n_heads=n_heads, n_mla_heads=ha, tail_start=tail_start),
        grid=(B, nt),
        in_specs=[pl.BlockSpec((1, tm, D), row), _resident(norm_mix)]
        + [_resident(a) for a in weights] + [_resident(a) for a in gains] + kv_specs,
        out_specs=out_specs,
        out_shape=out_shapes,
        compiler_params=_params("arbitrary", "arbitrary"),
        name="proj",
    )(x, norm_mix, *weights, *gains, *kv_args)


def _q_up_kernel(cqn_ref, w_ref, gn_ref, gr_ref, grs_ref, aug_ref, cos_ref, sin_ref, q_ref,
                 *, n_heads):
    hw = n_heads * LANES
    hp = hw // 2
    y = _dot(cqn_ref[0], w_ref[...])
    low = lax.broadcasted_iota(jnp.int32, (1, LANES), 1) < ROPE
    for j in range(n_heads // 2):
        r = y[:, hw + j * LANES: hw + (j + 1) * LANES]
        rs = y[:, hw + hp + j * LANES: hw + hp + (j + 1) * LANES]
        r2 = r * r
        r2_of = (jnp.where(low, r2, 0.0), jnp.where(low, 0.0, r2))
        rinv = []
        for e in range(2):
            h = 2 * j + e
            n = y[:, h * LANES:(h + 1) * LANES]
            ss = jnp.sum(n * n + r2_of[e], axis=-1, keepdims=True)
            rinv.append(lax.rsqrt(ss * (1.0 / QK_MLA) + EPS))
            q_ref[0, h, :, 0:LANES] = (n * rinv[e] * gn_ref[...]).astype(BF16)
        rot = ((r * gr_ref[...]) * cos_ref[...] + (rs * grs_ref[...]) * sin_ref[...]) \
            * jnp.where(low, rinv[0], rinv[1])
        q_ref[0, 2 * j, :, LANES:QK_PAD] = (jnp.where(low, rot, 0.0) + aug_ref[...]).astype(BF16)
        q_ref[0, 2 * j + 1, :, LANES:QK_PAD] = (
            jnp.where(low, pltpu.roll(rot, ROPE, 1), 0.0) + aug_ref[...]).astype(BF16)


def _q_up(cqn, w, g_n, g_r, g_rs, aug, cos, sin, *, tm, pos0=0):
    B, S, _ = cqn.shape
    n_heads = w.shape[1] // (2 * LANES)
    assert pos0 % tm == 0
    row = lambda b, t: (b, t, 0)
    pos = lambda b, t: (pos0 // tm + t, 0)
    return pl.pallas_call(
        functools.partial(_q_up_kernel, n_heads=n_heads),
        grid=(B, S // tm),
        in_specs=[pl.BlockSpec((1, tm, Q_LORA), row), _resident(w),
                  _resident(g_n), _resident(g_r), _resident(g_rs), _resident(aug),
                  pl.BlockSpec((tm, LANES), pos), pl.BlockSpec((tm, LANES), pos)],
        out_specs=pl.BlockSpec((1, n_heads, tm, QK_PAD), lambda b, t: (b, 0, t, 0)),
        out_shape=jax.ShapeDtypeStruct((B, n_heads, S, QK_PAD), BF16),
        compiler_params=_params("arbitrary", "arbitrary"),
        name="q_up",
    )(cqn, w, g_n, g_r, g_rs, aug, cos, sin)


def _expand_kv(c, kpe, wuk, wuvt, g_n, g_r, k_aug, cos, sin, n_heads):
    cb = c.astype(BF16)
    kn = _dot(cb, wuk)
    vt = _dot_nt(wuvt, cb)
    a = kpe * g_r
    sspe = jnp.sum(kpe * kpe, axis=-1, keepdims=True)
    a_sw = pltpu.roll(a, HALF_ROPE, 1) + pltpu.roll(a, LANES - HALF_ROPE, 1)
    low = lax.broadcasted_iota(jnp.int32, (1, LANES), 1) < ROPE
    rot = jnp.where(low, a * cos + a_sw * sin, 0.0)
    keys = []
    for h in range(n_heads):
        n = kn[:, h * NOPE:(h + 1) * NOPE]
        ss = jnp.sum(n * n, axis=-1, keepdims=True) + sspe
        rinv = lax.rsqrt(ss * (1.0 / QK_MLA) + EPS)
        keys.append(jnp.concatenate([(n * rinv * g_n).astype(BF16),
                                     (rot * rinv + k_aug).astype(BF16)], axis=-1))
    return keys, vt


N_FLASH_IN = 3


def _mla_flash_kernel(*refs, tq, td, hg, bounded, n_riders):
    q_ref, k_ref, vt_ref = refs[:N_FLASH_IN]
    o_ref = refs[N_FLASH_IN + n_riders]
    m_sc, l_sc, acc_sc = refs[N_FLASH_IN + 1 + 2 * n_riders:]
    for src, dst in zip(refs[N_FLASH_IN:N_FLASH_IN + n_riders],
                        refs[N_FLASH_IN + 1 + n_riders:N_FLASH_IN + 1 + 2 * n_riders]):
        dst[...] = src[...].astype(BF16)
    i = pl.program_id(2)
    if not bounded:
        m_sc[...] = jnp.full(m_sc.shape, -jnp.inf, F32)
    l_sc[...] = jnp.zeros(l_sc.shape, F32)
    acc_sc[...] = jnp.zeros(acc_sc.shape, F32)

    def step(tiles):
        sts = [(hh, t, _dot_nt(k_ref[0, hh, pl.ds(t[0], t[1]), :], q_ref[0, hh, t[2]:, :]))
               for t in tiles for hh in range(hg)]
        for hh, (k0, kn, q0, mask), st in sts:
            if mask is not None:
                st = jnp.where(mask, st, NEG)
            vt = vt_ref[0, hh, :, pl.ds(k0, kn)]
            if bounded:
                p = jnp.exp2(st)
                l_sc[hh, :, q0:] += jnp.sum(p, axis=0, keepdims=True)
                acc_sc[hh, :, q0:] += _dot(vt, p.astype(BF16))
            else:
                m_prev = m_sc[hh, :, q0:]
                m_new = jnp.maximum(m_prev, jnp.max(st, axis=0, keepdims=True))
                alpha = jnp.exp2(m_prev - m_new)
                p = jnp.exp2(st - m_new)
                l_sc[hh, :, q0:] = alpha * l_sc[hh, :, q0:] + jnp.sum(p, axis=0, keepdims=True)
                acc_sc[hh, :, q0:] = acc_sc[hh, :, q0:] * alpha + _dot(vt, p.astype(BF16))
                m_sc[hh, :, q0:] = m_new

    def body(j, carry):
        step([(pl.multiple_of(j * tq, tq), tq, 0, None)])
        return carry

    lax.fori_loop(0, i, body, 0)
    diag = []
    for d in range(tq // td):
        nq = tq - d * td
        q_chunk = lax.broadcasted_iota(jnp.int32, (td, nq), 1) // CHUNK
        k_chunk = lax.broadcasted_iota(jnp.int32, (td, nq), 0) // CHUNK
        diag.append((pl.multiple_of(i * tq + d * td, td), td, d * td, k_chunk <= q_chunk))
    step(diag)
    for hh in range(hg):
        o_ref[0, :, hh * V_MLA:(hh + 1) * V_MLA] = (acc_sc[hh] / l_sc[hh]).T.astype(o_ref.dtype)


def _flash_steps(B, H, S, tq, hg):
    return B * (H // hg) * (S // tq)


def _can_ride(a, n_steps):
    return a.shape[0] % n_steps == 0 and (a.shape[0] // n_steps) % BF16_ROWS == 0


def _mla_flash(q, k, vt, *riders, tq, td, hg, bounded):
    B, H, S, _ = q.shape
    nh, nq = H // hg, S // tq
    step = lambda b, h, i: ((b * nh + h) * nq + i, 0)
    rider_spec = lambda a: pl.BlockSpec((a.shape[0] // (B * nh * nq), a.shape[1]), step)
    return pl.pallas_call(
        functools.partial(_mla_flash_kernel, tq=tq, td=td, hg=hg, bounded=bounded, n_riders=len(riders)),
        grid=(B, nh, nq),
        in_specs=[pl.BlockSpec((1, hg, tq, QK_PAD), lambda b, h, i: (b, h, i, 0)),
                  pl.BlockSpec((1, hg, S, QK_PAD), lambda b, h, i: (b, h, 0, 0)),
                  pl.BlockSpec((1, hg, V_MLA, S), lambda b, h, i: (b, h, 0, 0))]
        + [rider_spec(a) for a in riders],
        out_specs=[pl.BlockSpec((1, tq, hg * V_MLA), lambda b, h, i: (b, i, h))]
        + [rider_spec(a) for a in riders],
        out_shape=[jax.ShapeDtypeStruct((B, S, H * V_MLA), BF16)]
        + [jax.ShapeDtypeStruct(a.shape, BF16) for a in riders],
        scratch_shapes=[pltpu.VMEM((hg, 1, tq), F32), pltpu.VMEM((hg, 1, tq), F32),
                        pltpu.VMEM((hg, V_MLA, tq), F32)],
        compiler_params=_params("arbitrary", "arbitrary", "arbitrary"),
        name="mla_flash_bounded" if bounded else "mla_flash",
    )(q, k, vt, *riders)


def _toeplitz(u_row, rows, cols, stride=1):
    u = jnp.broadcast_to(u_row, (rows, u_row.shape[-1]))
    return pltpu.roll(u, 0, 1, stride=stride, stride_axis=0)[:, :cols]


def _softmax_pv(parts, bounded):
    m = None
    if not bounded:
        for s, _ in parts:
            mx = jnp.max(s, axis=-1, keepdims=True)
            m = mx if m is None else jnp.maximum(m, mx)
    l = None
    acc = None
    for s, v in parts:
        p = jnp.exp2(s if bounded else s - m)
        ps = jnp.sum(p, axis=-1, keepdims=True)
        pv = _dot(p.astype(BF16), v)
        l = ps if l is None else l + ps
        acc = pv if acc is None else acc + pv
    return acc / l


def _band_kernel(q_ref, kc_ref, kp_ref, vc_ref, vp_ref, u_ref, o_ref, bc_sc, bp_sc,
                 *, n_heads, splits, bounded):
    b, g = pl.program_id(0), pl.program_id(1)
    tg = bc_sc.shape[1]

    @pl.when((b == 0) & (g == 0))
    def _():
        qc = lax.broadcasted_iota(jnp.int32, (tg, tg), 0) // CHUNK
        kc = lax.broadcasted_iota(jnp.int32, (tg, tg), 1) // CHUNK
        for h in range(n_heads):
            bc_sc[h] = jnp.where(kc <= qc, _toeplitz(u_ref[h, 0:1, :], tg, tg), NEG)
            bp_sc[h] = jnp.where(kc >= qc, _toeplitz(u_ref[h, 1:2, :], tg, tg), NEG)

    no_prev = jnp.where(g == 0, NEG, 0.0)
    tr = tg // splits
    for h0 in range(0, n_heads, 2):
        scores = []
        for h in (h0, h0 + 1):
            sl = slice(h * D_BAND, (h + 1) * D_BAND)
            for a in range(splits):
                rows, old, new = slice(a * tr, (a + 1) * tr), slice(a * tr, tg), slice(0, (a + 1) * tr)
                q = q_ref[0, rows, sl]
                scores.append((rows, old, new, sl,
                               _dot_nt(q, kp_ref[0, old, sl]) + (bp_sc[h, rows, old] + no_prev),
                               _dot_nt(q, kc_ref[0, new, sl]) + bc_sc[h, rows, new]))
        for rows, old, new, sl, s_prev, s_cur in scores:
            parts = [(s_prev, vp_ref[0, old, sl]), (s_cur, vc_ref[0, new, sl])]
            o_ref[0, rows, sl] = _softmax_pv(parts, bounded).astype(o_ref.dtype)


def _band(qb, kb, vb, u_tab, *, bounded):
    B, S, hw = qb.shape
    H = hw // D_BAND
    tg = BAND_PAST
    cur = lambda b, g: (b, g, 0)
    prev = lambda b, g: (b, jnp.maximum(g - 1, 0), 0)
    blk = (1, tg, hw)
    return pl.pallas_call(
        functools.partial(_band_kernel, n_heads=H, splits=4, bounded=bounded),
        grid=(B, S // tg),
        in_specs=[pl.BlockSpec(blk, cur), pl.BlockSpec(blk, cur), pl.BlockSpec(blk, prev),
                  pl.BlockSpec(blk, cur), pl.BlockSpec(blk, prev),
                  _resident(u_tab)],
        out_specs=pl.BlockSpec(blk, cur),
        out_shape=jax.ShapeDtypeStruct((B, S, hw), BF16),
        scratch_shapes=[pltpu.VMEM((H, tg, tg), F32), pltpu.VMEM((H, tg, tg), F32)],
        compiler_params=_params("arbitrary", "arbitrary"),
        name="band_bounded" if bounded else "band",
    )(qb, kb, kb, vb, vb, u_tab)


def _mla_sample_kernel(q_ref, cc_ref, pc_ref, cn_ref, pn_ref, wuk_ref, wuvt_ref, gn_ref, gr_ref,
                       aug_ref, cosc_ref, sinc_ref, cosn_ref, sinn_ref, o_ref, *, n_heads, t_new):
    args = (wuk_ref[...], wuvt_ref[...], gn_ref[...], gr_ref[...], aug_ref[...])
    pc = jnp.concatenate([pc_ref[0], jnp.zeros((pc_ref.shape[1], LANES - ROPE), F32)], axis=-1)
    kc, vtc = _expand_kv(cc_ref[0], pc, *args, cosc_ref[...], sinc_ref[...], n_heads)
    kn, vtn = _expand_kv(cn_ref[0], pn_ref[0], *args, cosn_ref[...], sinn_ref[...], n_heads)
    rows_n = cn_ref.shape[1]
    new_ok = lax.broadcasted_iota(jnp.int32, (q_ref.shape[2], rows_n), 1) < t_new
    scores = [(_dot_nt(q_ref[0, h], kc[h]), jnp.where(new_ok, _dot_nt(q_ref[0, h], kn[h]), NEG))
              for h in range(n_heads)]
    for h, (s_c, s_n) in enumerate(scores):
        m = jnp.maximum(jnp.max(s_c, axis=-1, keepdims=True), jnp.max(s_n, axis=-1, keepdims=True))
        p_c = jnp.exp2(s_c - m)
        p_n = jnp.exp2(s_n - m)
        l = jnp.sum(p_c, axis=-1, keepdims=True) + jnp.sum(p_n, axis=-1, keepdims=True)
        sl = slice(h * V_MLA, (h + 1) * V_MLA)
        pv = (_dot_nt(p_c.astype(BF16), vtc[sl].astype(BF16))
              + _dot_nt(p_n.astype(BF16), vtn[sl].astype(BF16)))
        o_ref[0, :, sl] = (pv / l).astype(o_ref.dtype)


def _mla_sample(q, ckv_c, kpe_c, ckv_n, kpe_n, w_uk, w_uvt, g_n, g_r, k_aug, tabs, *, t_new):
    Bd, H, T, _ = q.shape
    P = ckv_c.shape[1]
    rn = ckv_n.shape[1]
    assert P % rn == 0
    old = lambda: pl.BlockSpec((P, LANES), lambda b: (0, 0), pipeline_mode=pl.Buffered(1))
    new = lambda: pl.BlockSpec((rn, LANES), lambda b: (P // rn, 0), pipeline_mode=pl.Buffered(1))
    bat = lambda b: (b, 0, 0)
    return pl.pallas_call(
        functools.partial(_mla_sample_kernel, n_heads=H, t_new=t_new),
        grid=(Bd,),
        in_specs=[pl.BlockSpec((1, H, T, QK_PAD), lambda b: (b, 0, 0, 0)),
                  pl.BlockSpec((1, P, KV_LORA), bat), pl.BlockSpec((1, P, ROPE), bat),
                  pl.BlockSpec((1, rn, KV_LORA), bat), pl.BlockSpec((1, rn, LANES), bat),
                  _resident(w_uk), _resident(w_uvt), _resident(g_n), _resident(g_r), _resident(k_aug),
                  old(), old(), new(), new()],
        out_specs=pl.BlockSpec((1, T, H * V_MLA), bat),
        out_shape=jax.ShapeDtypeStruct((Bd, T, H * V_MLA), BF16),
        compiler_params=_params("arbitrary"),
        name="mla_sample",
    )(q, ckv_c, kpe_c, ckv_n, kpe_n, w_uk, w_uvt, g_n, g_r, k_aug, *tabs, *tabs)


def _band_sample_kernel(q_ref, kc_ref, vc_ref, kn_ref, vn_ref, wc_ref, wn_ref, o_ref, *, n_heads):
    T = q_ref.shape[1]
    heads = [slice(h * D_BAND, (h + 1) * D_BAND) for h in range(n_heads)]
    q = jnp.concatenate([q_ref[0, :, sl] for sl in heads], axis=0)
    tab = lambda w_ref, cols: jnp.concatenate(
        [_toeplitz(w_ref[h:h + 1, :], T, cols, stride=n_heads) for h in range(n_heads)], axis=0)
    s_c = _dot_nt(q, kc_ref[0].astype(BF16)) + tab(wc_ref, kc_ref.shape[1])
    s_n = _dot_nt(q, kn_ref[0]) + tab(wn_ref, kn_ref.shape[1])
    parts = [(s_c, vc_ref[0].astype(BF16)), (s_n, vn_ref[0])]
    o = _softmax_pv(parts, False).astype(o_ref.dtype)
    for h, sl in enumerate(heads):
        o_ref[0, :, sl] = o[h * T:(h + 1) * T]


def _band_sample(q, k_c, v_c, k_n, v_n, w_c, w_n, *, n_heads):
    Bd = q.shape[0]
    bat = lambda b: (b, 0, 0)
    blk = lambda a: pl.BlockSpec((1,) + a.shape[1:], bat)
    return pl.pallas_call(
        functools.partial(_band_sample_kernel, n_heads=n_heads),
        grid=(Bd,),
        in_specs=[blk(q), blk(k_c), blk(v_c), blk(k_n), blk(v_n), _resident(w_c), _resident(w_n)],
        out_specs=blk(q),
        out_shape=jax.ShapeDtypeStruct(q.shape, BF16),
        compiler_params=_params("arbitrary"),
        name="band_sample",
    )(q, k_c, v_c, k_n, v_n, w_c, w_n)


def _merge_kernel(x_ref, oa_ref, ob_ref, wo_ref, h_ref):
    half = oa_ref.shape[-1]
    h_ref[0] = x_ref[0] + _dot(oa_ref[0], wo_ref[:half, :]) + _dot(ob_ref[0], wo_ref[half:, :])


def _merge(x, oa, ob, w_o, *, tm):
    B, S, D = x.shape
    row = lambda b, t: (b, t, 0)
    return pl.pallas_call(
        _merge_kernel,
        grid=(B, S // tm),
        in_specs=[pl.BlockSpec((1, tm, D), row), pl.BlockSpec((1, tm, oa.shape[-1]), row),
                  pl.BlockSpec((1, tm, ob.shape[-1]), row),
                  _resident(w_o)],
        out_specs=pl.BlockSpec((1, tm, D), row),
        out_shape=jax.ShapeDtypeStruct((B, S, D), F32),
        compiler_params=_params("arbitrary", "arbitrary"),
        name="merge",
    )(x, oa, ob, w_o)


def _ffn_kernel(h_ref, g_ref, wup_ref, wdn_ref, y_ref, hn_sc):
    j = pl.program_id(2)

    @pl.when(j == 0)
    def _():
        h = h_ref[0]
        hn_sc[...] = _rms(h, g_ref[...]).astype(BF16)
        y_ref[0] = h

    u = jnp.maximum(_dot(hn_sc[...], wup_ref[...]), 0.0)
    y_ref[0] += _dot((u * u).astype(BF16), wdn_ref[...])


def _ffn(h, g, w_up, w_down, *, tm, tf):
    B, S, D = h.shape
    F = w_up.shape[1]
    row = lambda b, t, j: (b, t, 0)
    return pl.pallas_call(
        _ffn_kernel,
        grid=(B, S // tm, F // tf),
        in_specs=[pl.BlockSpec((1, tm, D), row), _resident(g),
                  pl.BlockSpec((D, tf), lambda b, t, j: (0, j)),
                  pl.BlockSpec((tf, D), lambda b, t, j: (j, 0))],
        out_specs=pl.BlockSpec((1, tm, D), row),
        out_shape=jax.ShapeDtypeStruct((B, S, D), F32),
        scratch_shapes=[pltpu.VMEM((tm, D), BF16)],
        compiler_params=_params("arbitrary", "arbitrary", "arbitrary"),
        name="ffn",
    )(h, g, w_up, w_down)


def _rope_tables(n_pos):
    inv = 1.0 / (ROPE_BASE ** (jnp.arange(0, ROPE, 2, dtype=F32) / ROPE))
    n_hi = -(-n_pos // CHUNK)
    hi = (jnp.arange(n_hi, dtype=F32) * CHUNK)[:, None, None] * inv
    lo = jnp.arange(CHUNK, dtype=F32)[None, :, None] * inv
    ch, sh, cl, sl = jnp.cos(hi), jnp.sin(hi), jnp.cos(lo), jnp.sin(lo)
    c = (ch * cl - sh * sl).reshape(n_hi * CHUNK, HALF_ROPE)[:n_pos]
    s = (sh * cl + ch * sl).reshape(n_hi * CHUNK, HALF_ROPE)[:n_pos]
    return jnp.concatenate([c, c, c, c], axis=1), jnp.concatenate([-s, s, -s, s], axis=1)


def _pad_lanes(a, width=LANES):
    return jnp.pad(a, [(0, 0)] * (a.ndim - 1) + [(0, width - a.shape[-1])])


def _pad_rows(a, rows):
    return jnp.pad(a, [(0, 0), (0, rows - a.shape[1]), (0, 0)])


def _swap_halves(a):
    return jnp.concatenate([a[..., HALF_ROPE:], a[..., :HALF_ROPE]], axis=-1)


def _prep_weights(w_in, w_uq, w_uk, w_uv):
    hb3 = (w_in.shape[1] - Q_LORA - KV_LORA - ROPE) // 3
    o = [0, Q_LORA, Q_LORA + KV_LORA, Q_LORA + KV_LORA + ROPE]
    o += [o[3] + hb3, o[3] + 2 * hb3, o[3] + 3 * hb3]
    names = ("cq", "ckv", "kpe", "qb", "kb", "vb")
    w = {n: w_in[:, o[i]:o[i + 1]].astype(BF16) for i, n in enumerate(names)}
    w["kpe"] = _pad_lanes(w["kpe"])
    ha = w_uq.shape[1] // QK_MLA
    uq = w_uq.astype(BF16).reshape(Q_LORA, ha, QK_MLA)
    nope = uq[:, :, :NOPE].reshape(Q_LORA, ha * NOPE)
    rope = uq[:, :, NOPE:]
    w["uq"] = jnp.concatenate(
        [nope, rope.reshape(Q_LORA, ha * ROPE), _swap_halves(rope).reshape(Q_LORA, ha * ROPE)], axis=1)
    w["uk"] = w_uk.astype(BF16)
    w["uvt"] = w_uv.astype(BF16).T
    return w


def _softmax_setup(g_qa, g_ka, g_qb, g_kb, rel_bias):
    c_a = QK_MLA ** -0.5 * LOG2E
    c_b = D_BAND ** -0.5 * LOG2E
    amax = lambda a: jnp.max(jnp.abs(a))
    bound_a = c_a * QK_MLA * amax(g_qa) * amax(g_ka) * BOUND_MARGIN
    bound_b = c_b * D_BAND * amax(g_qb) * amax(g_kb) * BOUND_MARGIN + LOG2E * amax(rel_bias)
    fast_a = bound_a <= FAST_LIMIT
    fast_b = bound_b <= FAST_LIMIT
    shift_a = jnp.where(fast_a, bound_a, 0.0)
    shift_b = jnp.where(fast_b, bound_b, 0.0)
    lane = jnp.arange(LANES)[None, :]
    twice = lambda a: jnp.concatenate([a, a], axis=-1)
    g = {"qa_n": g_qa[None, :NOPE] * c_a, "qa_r": twice(g_qa[None, NOPE:]) * c_a,
         "qa_rs": twice(_swap_halves(g_qa[None, NOPE:])) * c_a,
         "q_aug": (lane == AUG_LANE).astype(F32),
         "ka_n": g_ka[None, :NOPE], "ka_r": _pad_lanes(g_ka[None, NOPE:]),
         "k_aug": jnp.where(lane == AUG_LANE, -shift_a, 0.0).astype(F32),
         "qb": g_qb * c_b}
    return g, fast_a, fast_b, shift_b


def _band_vectors(rel_bias, shift):
    t = BAND_PAST
    e = np.arange(2 * t)
    e = np.where(e < t, e, e - 2 * t)
    idx = np.stack([np.clip(-e, -MAX_REL, MAX_REL), np.clip(t - e, -MAX_REL, MAX_REL)]) + MAX_REL
    return (rel_bias[:, idx] * LOG2E - shift).astype(F32)


def _band_sample_vectors(rel_bias, lb):
    H = rel_bias.shape[0]

    def interleave(dist, n_pos, n):
        e = np.arange(n)
        e = np.where(e < n_pos, e, e - n)
        u = rel_bias[:, np.clip(dist - e, -MAX_REL, MAX_REL) + MAX_REL] * LOG2E
        own = np.arange(H)[:, None, None] == np.arange(H)[None, None, :]
        return jnp.where(own, u[:, :, None], NEG).reshape(H, n * H).astype(F32)

    return interleave(lb, lb, lb + LANES), interleave(0, CHUNK, 2 * CHUNK)


def _row_tile(n, pref):
    return pref if n % pref == 0 else n


def kernel(x_prompt, x_sample, cache_mla_ckv, cache_mla_kpe, cache_band_k, cache_band_v,
           norm_mix, w_in, g_cq, w_uq, g_ckv, w_uk, w_uv, g_qa, g_ka, g_qb, g_kb, rel_bias,
           w_o, norm_ffn, w_up, w_down):
    depth = w_in.shape[0]
    assert depth == 1, "single-layer step"
    B, S, D = x_prompt.shape
    Bd, T, _ = x_sample.shape
    P = cache_mla_ckv.shape[2]
    Lb = cache_band_k.shape[2]
    keep_p = min(BAND_PAST, S)
    assert S % BAND_PAST == 0 and T <= CHUNK

    w = _prep_weights(w_in[0], w_uq[0], w_uk[0], w_uv[0])
    g, fast_a, fast_b, shift_b = _softmax_setup(g_qa[0], g_ka[0], g_qb, g_kb, rel_bias[0])
    ha = w["uk"].shape[1] // NOPE
    hb = rel_bias.shape[1]
    hw = hb * D_BAND
    gains = (g_cq, g_ckv, g["qb"], g_kb)
    q_args = (w["uq"], g["qa_n"], g["qa_r"], g["qa_rs"], g["q_aug"])
    kv_args = (w["uk"], w["uvt"], g["ka_n"], g["ka_r"], g["k_aug"])

    rn = LANES
    tabs = _rope_tables(max(S, P + rn))
    cqn, ckv, kpe_pad, kpe, qb, kb, vb, kb_tail, vb_tail, k, vt = _proj(
        x_prompt, norm_mix, w, *gains, keep=keep_p, tm=_row_tile(S, 512), kv=(*kv_args, *tabs))
    q = _q_up(cqn, *q_args, *tabs, tm=_row_tile(S, 512))
    tqa = _row_tile(S, 1024)
    flash = lambda bounded: functools.partial(_mla_flash, tq=tqa, td=min(tqa, 256), hg=2, bounded=bounded)
    late = (w_o[0], w_up[0], w_down[0])
    ride = [_can_ride(a, _flash_steps(B, ha, S, tqa, 2)) for a in late]
    oa, *cast = lax.cond(fast_a, flash(True), flash(False), q, k, vt, *(a for a, r in zip(late, ride) if r))
    cast = iter(cast)
    w["o"], w["up"], w["down"] = (next(cast) if r else a.astype(BF16) for a, r in zip(late, ride))
    u_tab = _band_vectors(rel_bias[0], shift_b)
    ob = lax.cond(fast_b, functools.partial(_band, bounded=True),
                  functools.partial(_band, bounded=False), qb, kb, vb, u_tab)
    h = _merge(x_prompt, oa, ob, w["o"], tm=_row_tile(S, 512))
    y_prompt = _ffn(h, norm_ffn, w["up"], w["down"], tm=_row_tile(S, 512), tf=2048)

    n_s = Bd * T
    xs = x_sample.reshape(1, n_s, D)
    cqn_s, ckv_s, kpe_pad_s, kpe_s, qb_s, kb_s, vb_s, kb_s32, vb_s32 = _proj(
        xs, norm_mix, w, *gains, keep=n_s, tm=_row_tile(n_s, 256))
    q_s = _q_up(cqn_s.reshape(Bd, T, Q_LORA), *q_args, *tabs, tm=T, pos0=P)
    oa_s = _mla_sample(
        q_s, cache_mla_ckv[0], cache_mla_kpe[0],
        _pad_rows(ckv_s.reshape(Bd, T, KV_LORA), rn), _pad_rows(kpe_pad_s.reshape(Bd, T, LANES), rn),
        *kv_args, tabs, t_new=T)
    w_c, w_n = _band_sample_vectors(rel_bias[0], Lb)
    ob_s = _band_sample(
        qb_s.reshape(Bd, T, hw), cache_band_k.reshape(Bd, Lb * hb, D_BAND), cache_band_v.reshape(Bd, Lb * hb, D_BAND),
        kb_s.reshape(Bd, T * hb, D_BAND), vb_s.reshape(Bd, T * hb, D_BAND), w_c, w_n, n_heads=hb)
    h_s = _merge(xs, oa_s.reshape(1, n_s, ha * V_MLA), ob_s.reshape(1, n_s, hw), w["o"], tm=n_s)
    y_sample = _ffn(h_s, norm_ffn, w["up"], w["down"], tm=n_s, tf=2048).reshape(Bd, T, D)

    return (y_prompt, y_sample,
            ckv[None], kpe[None],
            kb_tail.reshape(B, keep_p, hb, D_BAND)[None], vb_tail.reshape(B, keep_p, hb, D_BAND)[None],
            ckv_s.reshape(Bd, T, KV_LORA)[None], kpe_s.reshape(Bd, T, ROPE)[None],
            kb_s32.reshape(Bd, T, hb, D_BAND)[None], vb_s32.reshape(Bd, T, hb, D_BAND)[None])
```

```python
import functools

import jax
import jax.numpy as jnp
import numpy as np
from jax import lax
from jax.experimental import pallas as pl
from jax.experimental.pallas import tpu as pltpu

CHUNK = 64
EPS = 1e-6
NOPE = 128
ROPE = 64
HALF_ROPE = ROPE // 2
QK_MLA = NOPE + ROPE
V_MLA = 128
D_BAND = 128
BAND_CHUNKS = 8
BAND_PAST = BAND_CHUNKS * CHUNK
MAX_REL = 128
ROPE_BASE = 10000.0
Q_LORA = 512
KV_LORA = 256
NEG = -1e30
LOG2E = 1.4426950408889634

BOUND_MARGIN = 1.02
FAST_LIMIT = 60.0

LANES = 128
BF16_ROWS = 16
QK_PAD = 2 * LANES
AUG_LANE = ROPE
VMEM_LIMIT = 56 * 1024 * 1024

BF16 = jnp.bfloat16
F32 = jnp.float32

NT_DIMS = (((1,), (1,)), ((), ()))


def _params(*sem):
    return pltpu.CompilerParams(dimension_semantics=sem, vmem_limit_bytes=VMEM_LIMIT)


def _resident(a):
    return pl.BlockSpec(a.shape, lambda *_: (0,) * a.ndim, pipeline_mode=pl.Buffered(1))


def _rms(x, g):
    ms = jnp.mean(x * x, axis=-1, keepdims=True)
    return x * lax.rsqrt(ms + EPS) * g


def _dot(a, b):
    return jnp.dot(a, b, preferred_element_type=F32)


def _dot_nt(a, b):
    return lax.dot_general(a, b, NT_DIMS, preferred_element_type=F32)


N_PROJ_IN, N_PROJ_OUT, N_KV_IN = 12, 9, 7


def _proj_kernel(*refs, n_heads, n_mla_heads, tail_start):
    (x_ref, nm_ref, wcq_ref, wckv_ref, wkpe_ref, wqb_ref, wkb_ref, wvb_ref,
     gcq_ref, gckv_ref, gqb_ref, gkb_ref) = refs[:N_PROJ_IN]
    n_kv = N_KV_IN if n_mla_heads else 0
    outs = refs[N_PROJ_IN + n_kv:]
    cqn_ref, ckv_ref, kpe_ref, kpe_out_ref, qb_ref, kb_ref, vb_ref, kbt_ref, vbt_ref = outs[:N_PROJ_OUT]
    t = pl.program_id(1)
    xn = _rms(x_ref[0], nm_ref[...]).astype(BF16)
    cqn_ref[0] = _rms(_dot(xn, wcq_ref[...]), gcq_ref[...]).astype(BF16)
    ckv = _rms(_dot(xn, wckv_ref[...]), gckv_ref[...])
    ckv_ref[0] = ckv
    kpe = _dot(xn, wkpe_ref[...])
    kpe_ref[0] = kpe
    kpe_out_ref[0] = kpe[:, :ROPE]
    if n_mla_heads:
        wuk_ref, wuvt_ref, gn_ref, gr_ref, aug_ref, cos_ref, sin_ref = refs[N_PROJ_IN:N_PROJ_IN + n_kv]
        k_ref, vt_ref = outs[N_PROJ_OUT:]
        keys, vt = _expand_kv(ckv, kpe, wuk_ref[...], wuvt_ref[...], gn_ref[...], gr_ref[...],
                              aug_ref[...], cos_ref[...], sin_ref[...], n_mla_heads)
        for h in range(n_mla_heads):
            k_ref[0, h] = keys[h]
            vt_ref[0, h] = vt[h * V_MLA:(h + 1) * V_MLA, :].astype(BF16)
    qb = _dot(xn, wqb_ref[...])
    kb = _dot(xn, wkb_ref[...])
    vb = _dot(xn, wvb_ref[...])
    vb_ref[0] = vb.astype(BF16)
    kbn = []
    for h in range(n_heads):
        sl = slice(h * D_BAND, (h + 1) * D_BAND)
        qb_ref[0, :, sl] = _rms(qb[:, sl], gqb_ref[...]).astype(BF16)
        kbn.append(_rms(kb[:, sl], gkb_ref[...]))
    kbn = jnp.concatenate(kbn, axis=-1)
    kb_ref[0] = kbn.astype(BF16)

    @pl.when(t >= tail_start)
    def _():
        kbt_ref[0] = kbn
        vbt_ref[0] = vb


def _proj(x, norm_mix, w, g_cq, g_ckv, g_qb, g_kb, *, keep, tm, kv=None):
    B, S, D = x.shape
    n_heads = w["qb"].shape[1] // D_BAND
    hw = n_heads * D_BAND
    nt = S // tm
    tail_start = nt - keep // tm
    row = lambda b, t: (b, t, 0)
    tail = lambda b, t: (b, jnp.maximum(t - tail_start, 0), 0)
    out_shapes = (
        jax.ShapeDtypeStruct((B, S, Q_LORA), BF16),
        jax.ShapeDtypeStruct((B, S, KV_LORA), F32),
        jax.ShapeDtypeStruct((B, S, LANES), F32),
        jax.ShapeDtypeStruct((B, S, ROPE), F32),
        jax.ShapeDtypeStruct((B, S, hw), BF16),
        jax.ShapeDtypeStruct((B, S, hw), BF16),
        jax.ShapeDtypeStruct((B, S, hw), BF16),
        jax.ShapeDtypeStruct((B, keep, hw), F32),
        jax.ShapeDtypeStruct((B, keep, hw), F32),
    )
    out_specs = (
        pl.BlockSpec((1, tm, Q_LORA), row),
        pl.BlockSpec((1, tm, KV_LORA), row),
        pl.BlockSpec((1, tm, LANES), row),
        pl.BlockSpec((1, tm, ROPE), row),
        pl.BlockSpec((1, tm, hw), row),
        pl.BlockSpec((1, tm, hw), row),
        pl.BlockSpec((1, tm, hw), row),
        pl.BlockSpec((1, tm, hw), tail),
        pl.BlockSpec((1, tm, hw), tail),
    )
    weights = (w["cq"], w["ckv"], w["kpe"], w["qb"], w["kb"], w["vb"])
    gains = (g_cq, g_ckv, g_qb, g_kb)
    kv_specs, kv_args, ha = [], (), 0
    if kv is not None:
        ha = kv[0].shape[1] // NOPE
        pos = lambda b, t: (t, 0)
        kv_specs = [_resident(a) for a in kv[:5]] + [pl.BlockSpec((tm, LANES), pos)] * 2
        kv_args = tuple(kv)
        out_shapes += (jax.ShapeDtypeStruct((B, ha, S, QK_PAD), BF16),
                       jax.ShapeDtypeStruct((B, ha, V_MLA, S), BF16))
        out_specs += (pl.BlockSpec((1, ha, tm, QK_PAD), lambda b, t: (b, 0, t, 0)),
                      pl.BlockSpec((1, ha, V_MLA, tm), lambda b, t: (b, 0, 0, t)))
    return pl.pallas_call(
        functools.partial(_proj_kernel, n_heads=n_heads, n_mla_heads=ha, tail_start=tail_start),
        grid=(B, nt),
        in_specs=[pl.BlockSpec((1, tm, D), row), _resident(norm_mix)]
        + [_resident(a) for a in weights] + [_resident(a) for a in gains] + kv_specs,
        out_specs=out_specs,
        out_shape=out_shapes,
        compiler_params=_params("arbitrary", "arbitrary"),
        name="proj",
    )(x, norm_mix, *weights, *gains, *kv_args)


def _q_up_kernel(cqn_ref, w_ref, gn_ref, gr_ref, grs_ref, aug_ref, cos_ref, sin_ref, q_ref,
                 *, n_heads):
    hw = n_heads * LANES
    hp = hw // 2
    y = _dot(cqn_ref[0], w_ref[...])
    low = lax.broadcasted_iota(jnp.int32, (1, LANES), 1) < ROPE
    for j in range(n_heads // 2):
        r = y[:, hw + j * LANES: hw + (j + 1) * LANES]
        rs = y[:, hw + hp + j * LANES: hw + hp + (j + 1) * LANES]
        r2 = r * r
        r2_of = (jnp.where(low, r2, 0.0), jnp.where(low, 0.0, r2))
        rinv = []
        for e in range(2):
            h = 2 * j + e
            n = y[:, h * LANES:(h + 1) * LANES]
            ss = jnp.sum(n * n + r2_of[e], axis=-1, keepdims=True)
            rinv.append(lax.rsqrt(ss * (1.0 / QK_MLA) + EPS))
            q_ref[0, h, :, 0:LANES] = (n * rinv[e] * gn_ref[...]).astype(BF16)
        rot = ((r * gr_ref[...]) * cos_ref[...] + (rs * grs_ref[...]) * sin_ref[...]) \
            * jnp.where(low, rinv[0], rinv[1])
        q_ref[0, 2 * j, :, LANES:QK_PAD] = (jnp.where(low, rot, 0.0) + aug_ref[...]).astype(BF16)
        q_ref[0, 2 * j + 1, :, LANES:QK_PAD] = (
            jnp.where(low, pltpu.roll(rot, ROPE, 1), 0.0) + aug_ref[...]).astype(BF16)


def _q_up(cqn, w, g_n, g_r, g_rs, aug, cos, sin, *, tm):
    B, S, _ = cqn.shape
    n_heads = w.shape[1] // (2 * LANES)
    row = lambda b, t: (b, t, 0)
    pos = lambda b, t: (t, 0)
    return pl.pallas_call(
        functools.partial(_q_up_kernel, n_heads=n_heads),
        grid=(B, S // tm),
        in_specs=[pl.BlockSpec((1, tm, Q_LORA), row), _resident(w),
                  _resident(g_n), _resident(g_r), _resident(g_rs), _resident(aug),
                  pl.BlockSpec((tm, LANES), pos), pl.BlockSpec((tm, LANES), pos)],
        out_specs=pl.BlockSpec((1, n_heads, tm, QK_PAD), lambda b, t: (b, 0, t, 0)),
        out_shape=jax.ShapeDtypeStruct((B, n_heads, S, QK_PAD), BF16),
        compiler_params=_params("arbitrary", "arbitrary"),
        name="q_up",
    )(cqn, w, g_n, g_r, g_rs, aug, cos, sin)


def _expand_kv(c, kpe, wuk, wuvt, g_n, g_r, k_aug, cos, sin, n_heads):
    cb = c.astype(BF16)
    kn = _dot(cb, wuk)
    vt = _dot_nt(wuvt, cb)
    a = kpe * g_r
    sspe = jnp.sum(kpe * kpe, axis=-1, keepdims=True)
    a_sw = pltpu.roll(a, HALF_ROPE, 1) + pltpu.roll(a, LANES - HALF_ROPE, 1)
    low = lax.broadcasted_iota(jnp.int32, (1, LANES), 1) < ROPE
    rot = jnp.where(low, a * cos + a_sw * sin, 0.0)
    keys = []
    for h in range(n_heads):
        n = kn[:, h * NOPE:(h + 1) * NOPE]
        ss = jnp.sum(n * n, axis=-1, keepdims=True) + sspe
        rinv = lax.rsqrt(ss * (1.0 / QK_MLA) + EPS)
        keys.append(jnp.concatenate([(n * rinv * g_n).astype(BF16),
                                     (rot * rinv + k_aug).astype(BF16)], axis=-1))
    return keys, vt


N_FLASH_IN = 3


def _mla_flash_kernel(*refs, tq, td, hg, bounded, n_riders):
    q_ref, k_ref, vt_ref = refs[:N_FLASH_IN]
    o_ref = refs[N_FLASH_IN + n_riders]
    m_sc, l_sc, acc_sc = refs[N_FLASH_IN + 1 + 2 * n_riders:]
    for src, dst in zip(refs[N_FLASH_IN:N_FLASH_IN + n_riders],
                        refs[N_FLASH_IN + 1 + n_riders:N_FLASH_IN + 1 + 2 * n_riders]):
        dst[...] = src[...].astype(BF16)
    i = pl.program_id(2)
    if not bounded:
        m_sc[...] = jnp.full(m_sc.shape, -jnp.inf, F32)
    l_sc[...] = jnp.zeros(l_sc.shape, F32)
    acc_sc[...] = jnp.zeros(acc_sc.shape, F32)

    def step(tiles):
        sts = [(hh, t, _dot_nt(k_ref[0, hh, pl.ds(t[0], t[1]), :], q_ref[0, hh, t[2]:, :]))
               for t in tiles for hh in range(hg)]
        for hh, (k0, kn, q0, mask), st in sts:
            if mask is not None:
                st = jnp.where(mask, st, NEG)
            vt = vt_ref[0, hh, :, pl.ds(k0, kn)]
            if bounded:
                p = jnp.exp2(st)
                l_sc[hh, :, q0:] += jnp.sum(p, axis=0, keepdims=True)
                acc_sc[hh, :, q0:] += _dot(vt, p.astype(BF16))
            else:
                m_prev = m_sc[hh, :, q0:]
                m_new = jnp.maximum(m_prev, jnp.max(st, axis=0, keepdims=True))
                alpha = jnp.exp2(m_prev - m_new)
                p = jnp.exp2(st - m_new)
                l_sc[hh, :, q0:] = alpha * l_sc[hh, :, q0:] + jnp.sum(p, axis=0, keepdims=True)
                acc_sc[hh, :, q0:] = acc_sc[hh, :, q0:] * alpha + _dot(vt, p.astype(BF16))
                m_sc[hh, :, q0:] = m_new

    def body(j, carry):
        step([(pl.multiple_of(j * tq, tq), tq, 0, None)])
        return carry

    lax.fori_loop(0, i, body, 0)
    diag = []
    for d in range(tq // td):
        nq = tq - d * td
        q_chunk = lax.broadcasted_iota(jnp.int32, (td, nq), 1) // CHUNK
        k_chunk = lax.broadcasted_iota(jnp.int32, (td, nq), 0) // CHUNK
        diag.append((pl.multiple_of(i * tq + d * td, td), td, d * td, k_chunk <= q_chunk))
    step(diag)
    for hh in range(hg):
        o_ref[0, :, hh * V_MLA:(hh + 1) * V_MLA] = (acc_sc[hh] / l_sc[hh]).T.astype(o_ref.dtype)


def _flash_steps(B, H, S, tq, hg):
    return B * (H // hg) * (S // tq)


def _can_ride(a, n_steps):
    return a.shape[0] % n_steps == 0 and (a.shape[0] // n_steps) % BF16_ROWS == 0


def _mla_flash(q, k, vt, *riders, tq, td, hg, bounded):
    B, H, S, _ = q.shape
    nh, nq = H // hg, S // tq
    step = lambda b, h, i: ((b * nh + h) * nq + i, 0)
    rider_spec = lambda a: pl.BlockSpec((a.shape[0] // (B * nh * nq), a.shape[1]), step)
    return pl.pallas_call(
        functools.partial(_mla_flash_kernel, tq=tq, td=td, hg=hg, bounded=bounded, n_riders=len(riders)),
        grid=(B, nh, nq),
        in_specs=[pl.BlockSpec((1, hg, tq, QK_PAD), lambda b, h, i: (b, h, i, 0)),
                  pl.BlockSpec((1, hg, S, QK_PAD), lambda b, h, i: (b, h, 0, 0)),
                  pl.BlockSpec((1, hg, V_MLA, S), lambda b, h, i: (b, h, 0, 0))]
        + [rider_spec(a) for a in riders],
        out_specs=[pl.BlockSpec((1, tq, hg * V_MLA), lambda b, h, i: (b, i, h))]
        + [rider_spec(a) for a in riders],
        out_shape=[jax.ShapeDtypeStruct((B, S, H * V_MLA), BF16)]
        + [jax.ShapeDtypeStruct(a.shape, BF16) for a in riders],
        scratch_shapes=[pltpu.VMEM((hg, 1, tq), F32), pltpu.VMEM((hg, 1, tq), F32),
                        pltpu.VMEM((hg, V_MLA, tq), F32)],
        compiler_params=_params("arbitrary", "arbitrary", "arbitrary"),
        name="mla_flash_bounded" if bounded else "mla_flash",
    )(q, k, vt, *riders)


def _toeplitz(u_row, rows, cols, stride=1):
    u = jnp.broadcast_to(u_row, (rows, u_row.shape[-1]))
    return pltpu.roll(u, 0, 1, stride=stride, stride_axis=0)[:, :cols]


def _softmax_pv(parts, bounded):
    m = None
    if not bounded:
        for s, _ in parts:
            mx = jnp.max(s, axis=-1, keepdims=True)
            m = mx if m is None else jnp.maximum(m, mx)
    l = None
    acc = None
    for s, v in parts:
        p = jnp.exp2(s if bounded else s - m)
        ps = jnp.sum(p, axis=-1, keepdims=True)
        pv = _dot(p.astype(BF16), v)
        l = ps if l is None else l + ps
        acc = pv if acc is None else acc + pv
    return acc / l


def _band_kernel(q_ref, kc_ref, kp_ref, vc_ref, vp_ref, u_ref, o_ref, bc_sc, bp_sc,
                 *, n_heads, splits, bounded):
    b, g = pl.program_id(0), pl.program_id(1)
    tg = bc_sc.shape[1]

    @pl.when((b == 0) & (g == 0))
    def _():
        qc = lax.broadcasted_iota(jnp.int32, (tg, tg), 0) // CHUNK
        kc = lax.broadcasted_iota(jnp.int32, (tg, tg), 1) // CHUNK
        for h in range(n_heads):
            bc_sc[h] = jnp.where(kc <= qc, _toeplitz(u_ref[h, 0:1, :], tg, tg), NEG)
            bp_sc[h] = jnp.where(kc >= qc, _toeplitz(u_ref[h, 1:2, :], tg, tg), NEG)

    no_prev = jnp.where(g == 0, NEG, 0.0)
    tr = tg // splits
    for h0 in range(0, n_heads, 2):
        scores = []
        for h in (h0, h0 + 1):
            sl = slice(h * D_BAND, (h + 1) * D_BAND)
            for a in range(splits):
                rows, old, new = slice(a * tr, (a + 1) * tr), slice(a * tr, tg), slice(0, (a + 1) * tr)
                q = q_ref[0, rows, sl]
                scores.append((rows, old, new, sl,
                               _dot_nt(q, kp_ref[0, old, sl]) + (bp_sc[h, rows, old] + no_prev),
                               _dot_nt(q, kc_ref[0, new, sl]) + bc_sc[h, rows, new]))
        for rows, old, new, sl, s_prev, s_cur in scores:
            parts = [(s_prev, vp_ref[0, old, sl]), (s_cur, vc_ref[0, new, sl])]
            o_ref[0, rows, sl] = _softmax_pv(parts, bounded).astype(o_ref.dtype)


def _band(qb, kb, vb, u_tab, *, bounded):
    B, S, hw = qb.shape
    H = hw // D_BAND
    tg = BAND_PAST
    cur = lambda b, g: (b, g, 0)
    prev = lambda b, g: (b, jnp.maximum(g - 1, 0), 0)
    blk = (1, tg, hw)
    return pl.pallas_call(
        functools.partial(_band_kernel, n_heads=H, splits=4, bounded=bounded),
        grid=(B, S // tg),
        in_specs=[pl.BlockSpec(blk, cur), pl.BlockSpec(blk, cur), pl.BlockSpec(blk, prev),
                  pl.BlockSpec(blk, cur), pl.BlockSpec(blk, prev),
                  _resident(u_tab)],
        out_specs=pl.BlockSpec(blk, cur),
        out_shape=jax.ShapeDtypeStruct((B, S, hw), BF16),
        scratch_shapes=[pltpu.VMEM((H, tg, tg), F32), pltpu.VMEM((H, tg, tg), F32)],
        compiler_params=_params("arbitrary", "arbitrary"),
        name="band_bounded" if bounded else "band",
    )(qb, kb, kb, vb, vb, u_tab)


def _mla_sample_kernel(q_ref, cc_ref, pc_ref, cn_ref, pn_ref, wuk_ref, wuvt_ref, gn_ref, gr_ref,
                       aug_ref, cosc_ref, sinc_ref, cosn_ref, sinn_ref, o_ref, *, n_heads, t_new):
    args = (wuk_ref[...], wuvt_ref[...], gn_ref[...], gr_ref[...], aug_ref[...])
    pc = jnp.concatenate([pc_ref[0], jnp.zeros((pc_ref.shape[1], LANES - ROPE), F32)], axis=-1)
    kc, vtc = _expand_kv(cc_ref[0], pc, *args, cosc_ref[...], sinc_ref[...], n_heads)
    kn, vtn = _expand_kv(cn_ref[0], pn_ref[0], *args, cosn_ref[...], sinn_ref[...], n_heads)
    rows_n = cn_ref.shape[1]
    new_ok = lax.broadcasted_iota(jnp.int32, (q_ref.shape[2], rows_n), 1) < t_new
    scores = [(_dot_nt(q_ref[0, h], kc[h]), jnp.where(new_ok, _dot_nt(q_ref[0, h], kn[h]), NEG))
              for h in range(n_heads)]
    for h, (s_c, s_n) in enumerate(scores):
        m = jnp.maximum(jnp.max(s_c, axis=-1, keepdims=True), jnp.max(s_n, axis=-1, keepdims=True))
        p_c = jnp.exp2(s_c - m)
        p_n = jnp.exp2(s_n - m)
        l = jnp.sum(p_c, axis=-1, keepdims=True) + jnp.sum(p_n, axis=-1, keepdims=True)
        sl = slice(h * V_MLA, (h + 1) * V_MLA)
        pv = (_dot_nt(p_c.astype(BF16), vtc[sl].astype(BF16))
              + _dot_nt(p_n.astype(BF16), vtn[sl].astype(BF16)))
        o_ref[0, :, sl] = (pv / l).astype(o_ref.dtype)


def _mla_sample(q, ckv_c, kpe_c, ckv_n, kpe_n, w_uk, w_uvt, g_n, g_r, k_aug, tabs, *, t_new):
    Bd = ckv_c.shape[0]
    H, T = q.shape[1], q.shape[2] // Bd
    P = ckv_c.shape[1]
    rn = ckv_n.shape[1]
    assert P % rn == 0
    old = lambda: pl.BlockSpec((P, LANES), lambda b: (0, 0), pipeline_mode=pl.Buffered(1))
    new = lambda: pl.BlockSpec((rn, LANES), lambda b: (P // rn, 0), pipeline_mode=pl.Buffered(1))
    bat = lambda b: (b, 0, 0)
    return pl.pallas_call(
        functools.partial(_mla_sample_kernel, n_heads=H, t_new=t_new),
        grid=(Bd,),
        in_specs=[pl.BlockSpec((1, H, T, QK_PAD), lambda b: (0, 0, b, 0)),
                  pl.BlockSpec((1, P, KV_LORA), bat), pl.BlockSpec((1, P, ROPE), bat),
                  pl.BlockSpec((1, rn, KV_LORA), bat), pl.BlockSpec((1, rn, LANES), bat),
                  _resident(w_uk), _resident(w_uvt), _resident(g_n), _resident(g_r), _resident(k_aug),
                  old(), old(), new(), new()],
        out_specs=pl.BlockSpec((1, T, H * V_MLA), bat),
        out_shape=jax.ShapeDtypeStruct((Bd, T, H * V_MLA), BF16),
        compiler_params=_params("arbitrary"),
        name="mla_sample",
    )(q, ckv_c, kpe_c, ckv_n, kpe_n, w_uk, w_uvt, g_n, g_r, k_aug, *tabs, *tabs)


def _band_sample_kernel(q_ref, kc_ref, vc_ref, kn_ref, vn_ref, wc_ref, wn_ref, o_ref, *, n_heads):
    T = q_ref.shape[1]
    heads = [slice(h * D_BAND, (h + 1) * D_BAND) for h in range(n_heads)]
    q = jnp.concatenate([q_ref[0, :, sl] for sl in heads], axis=0)
    tab = lambda w_ref, cols: jnp.concatenate(
        [_toeplitz(w_ref[h:h + 1, :], T, cols, stride=n_heads) for h in range(n_heads)], axis=0)
    s_c = _dot_nt(q, kc_ref[0].astype(BF16)) + tab(wc_ref, kc_ref.shape[1])
    s_n = _dot_nt(q, kn_ref[0]) + tab(wn_ref, kn_ref.shape[1])
    parts = [(s_c, vc_ref[0].astype(BF16)), (s_n, vn_ref[0])]
    o = _softmax_pv(parts, False).astype(o_ref.dtype)
    for h, sl in enumerate(heads):
        o_ref[0, :, sl] = o[h * T:(h + 1) * T]


def _band_sample(q, k_c, v_c, k_n, v_n, w_c, w_n, *, n_heads):
    Bd = q.shape[0]
    bat = lambda b: (b, 0, 0)
    blk = lambda a: pl.BlockSpec((1,) + a.shape[1:], bat)
    return pl.pallas_call(
        functools.partial(_band_sample_kernel, n_heads=n_heads),
        grid=(Bd,),
        in_specs=[blk(q), blk(k_c), blk(v_c), blk(k_n), blk(v_n), _resident(w_c), _resident(w_n)],
        out_specs=blk(q),
        out_shape=jax.ShapeDtypeStruct(q.shape, BF16),
        compiler_params=_params("arbitrary"),
        name="band_sample",
    )(q, k_c, v_c, k_n, v_n, w_c, w_n)


def _merge_kernel(x_ref, oa_ref, ob_ref, wo_ref, h_ref):
    half = oa_ref.shape[-1]
    h_ref[0] = x_ref[0] + _dot(oa_ref[0], wo_ref[:half, :]) + _dot(ob_ref[0], wo_ref[half:, :])


def _merge(x, oa, ob, w_o, *, tm):
    B, S, D = x.shape
    row = lambda b, t: (b, t, 0)
    return pl.pallas_call(
        _merge_kernel,
        grid=(B, S // tm),
        in_specs=[pl.BlockSpec((1, tm, D), row), pl.BlockSpec((1, tm, oa.shape[-1]), row),
                  pl.BlockSpec((1, tm, ob.shape[-1]), row),
                  _resident(w_o)],
        out_specs=pl.BlockSpec((1, tm, D), row),
        out_shape=jax.ShapeDtypeStruct((B, S, D), F32),
        compiler_params=_params("arbitrary", "arbitrary"),
        name="merge",
    )(x, oa, ob, w_o)


def _ffn_kernel(h_ref, g_ref, wup_ref, wdn_ref, y_ref, hn_sc):
    j = pl.program_id(2)

    @pl.when(j == 0)
    def _():
        h = h_ref[0]
        hn_sc[...] = _rms(h, g_ref[...]).astype(BF16)
        y_ref[0] = h

    u = jnp.maximum(_dot(hn_sc[...], wup_ref[...]), 0.0)
    y_ref[0] += _dot((u * u).astype(BF16), wdn_ref[...])


def _ffn(h, g, w_up, w_down, *, tm, tf):
    B, S, D = h.shape
    F = w_up.shape[1]
    row = lambda b, t, j: (b, t, 0)
    return pl.pallas_call(
        _ffn_kernel,
        grid=(B, S // tm, F // tf),
        in_specs=[pl.BlockSpec((1, tm, D), row), _resident(g),
                  pl.BlockSpec((D, tf), lambda b, t, j: (0, j)),
                  pl.BlockSpec((tf, D), lambda b, t, j: (j, 0))],
        out_specs=pl.BlockSpec((1, tm, D), row),
        out_shape=jax.ShapeDtypeStruct((B, S, D), F32),
        scratch_shapes=[pltpu.VMEM((tm, D), BF16)],
        compiler_params=_params("arbitrary", "arbitrary", "arbitrary"),
        name="ffn",
    )(h, g, w_up, w_down)


def _rope_tables(n_pos):
    inv = 1.0 / (ROPE_BASE ** (jnp.arange(0, ROPE, 2, dtype=F32) / ROPE))
    n_hi = -(-n_pos // CHUNK)
    hi = (jnp.arange(n_hi, dtype=F32) * CHUNK)[:, None, None] * inv
    lo = jnp.arange(CHUNK, dtype=F32)[None, :, None] * inv
    ch, sh, cl, sl = jnp.cos(hi), jnp.sin(hi), jnp.cos(lo), jnp.sin(lo)
    c = (ch * cl - sh * sl).reshape(n_hi * CHUNK, HALF_ROPE)[:n_pos]
    s = (sh * cl + ch * sl).reshape(n_hi * CHUNK, HALF_ROPE)[:n_pos]
    return jnp.concatenate([c, c, c, c], axis=1), jnp.concatenate([-s, s, -s, s], axis=1)


def _pad_lanes(a, width=LANES):
    return jnp.pad(a, [(0, 0)] * (a.ndim - 1) + [(0, width - a.shape[-1])])


def _pad_rows(a, rows):
    return jnp.pad(a, [(0, 0), (0, rows - a.shape[1]), (0, 0)])


def _swap_halves(a):
    return jnp.concatenate([a[..., HALF_ROPE:], a[..., :HALF_ROPE]], axis=-1)


def _prep_weights(w_in, w_uq, w_uk, w_uv):
    hb3 = (w_in.shape[1] - Q_LORA - KV_LORA - ROPE) // 3
    o = [0, Q_LORA, Q_LORA + KV_LORA, Q_LORA + KV_LORA + ROPE]
    o += [o[3] + hb3, o[3] + 2 * hb3, o[3] + 3 * hb3]
    names = ("cq", "ckv", "kpe", "qb", "kb", "vb")
    w = {n: w_in[:, o[i]:o[i + 1]].astype(BF16) for i, n in enumerate(names)}
    w["kpe"] = _pad_lanes(w["kpe"])
    ha = w_uq.shape[1] // QK_MLA
    uq = w_uq.astype(BF16).reshape(Q_LORA, ha, QK_MLA)
    nope = uq[:, :, :NOPE].reshape(Q_LORA, ha * NOPE)
    rope = uq[:, :, NOPE:]
    w["uq"] = jnp.concatenate(
        [nope, rope.reshape(Q_LORA, ha * ROPE), _swap_halves(rope).reshape(Q_LORA, ha * ROPE)], axis=1)
    w["uk"] = w_uk.astype(BF16)
    w["uvt"] = w_uv.astype(BF16).T
    return w


def _softmax_setup(g_qa, g_ka, g_qb, g_kb, rel_bias):
    c_a = QK_MLA ** -0.5 * LOG2E
    c_b = D_BAND ** -0.5 * LOG2E
    amax = lambda a: jnp.max(jnp.abs(a))
    bound_a = c_a * QK_MLA * amax(g_qa) * amax(g_ka) * BOUND_MARGIN
    bound_b = c_b * D_BAND * amax(g_qb) * amax(g_kb) * BOUND_MARGIN + LOG2E * amax(rel_bias)
    fast_a = bound_a <= FAST_LIMIT
    fast_b = bound_b <= FAST_LIMIT
    shift_a = jnp.where(fast_a, bound_a, 0.0)
    shift_b = jnp.where(fast_b, bound_b, 0.0)
    lane = jnp.arange(LANES)[None, :]
    twice = lambda a: jnp.concatenate([a, a], axis=-1)
    g = {"qa_n": g_qa[None, :NOPE] * c_a, "qa_r": twice(g_qa[None, NOPE:]) * c_a,
         "qa_rs": twice(_swap_halves(g_qa[None, NOPE:])) * c_a,
         "q_aug": (lane == AUG_LANE).astype(F32),
         "ka_n": g_ka[None, :NOPE], "ka_r": _pad_lanes(g_ka[None, NOPE:]),
         "k_aug": jnp.where(lane == AUG_LANE, -shift_a, 0.0).astype(F32),
         "qb": g_qb * c_b}
    return g, fast_a, fast_b, shift_b


def _band_vectors(rel_bias, shift):
    t = BAND_PAST
    e = np.arange(2 * t)
    e = np.where(e < t, e, e - 2 * t)
    idx = np.stack([np.clip(-e, -MAX_REL, MAX_REL), np.clip(t - e, -MAX_REL, MAX_REL)]) + MAX_REL
    return (rel_bias[:, idx] * LOG2E - shift).astype(F32)


def _band_sample_vectors(rel_bias, lb):
    H = rel_bias.shape[0]

    def interleave(dist, n_pos, n):
        e = np.arange(n)
        e = np.where(e < n_pos, e, e - n)
        u = rel_bias[:, np.clip(dist - e, -MAX_REL, MAX_REL) + MAX_REL] * LOG2E
        own = np.arange(H)[:, None, None] == np.arange(H)[None, None, :]
        return jnp.where(own, u[:, :, None], NEG).reshape(H, n * H).astype(F32)

    return interleave(lb, lb, lb + LANES), interleave(0, CHUNK, 2 * CHUNK)


def _row_tile(n, pref):
    return pref if n % pref == 0 else n


def kernel(x_prompt, x_sample, cache_mla_ckv, cache_mla_kpe, cache_band_k, cache_band_v,
           norm_mix, w_in, g_cq, w_uq, g_ckv, w_uk, w_uv, g_qa, g_ka, g_qb, g_kb, rel_bias,
           w_o, norm_ffn, w_up, w_down):
    depth = w_in.shape[0]
    assert depth == 1, "single-layer step"
    B, S, D = x_prompt.shape
    Bd, T, _ = x_sample.shape
    P = cache_mla_ckv.shape[2]
    Lb = cache_band_k.shape[2]
    keep_p = min(BAND_PAST, S)
    assert S % BAND_PAST == 0 and T <= CHUNK

    w = _prep_weights(w_in[0], w_uq[0], w_uk[0], w_uv[0])
    g, fast_a, fast_b, shift_b = _softmax_setup(g_qa[0], g_ka[0], g_qb, g_kb, rel_bias[0])
    ha = w["uk"].shape[1] // NOPE
    hb = rel_bias.shape[1]
    hw = hb * D_BAND
    gains = (g_cq, g_ckv, g["qb"], g_kb)
    q_args = (w["uq"], g["qa_n"], g["qa_r"], g["qa_rs"], g["q_aug"])
    kv_args = (w["uk"], w["uvt"], g["ka_n"], g["ka_r"], g["k_aug"])

    rn = LANES
    tabs = _rope_tables(max(S, P + rn))
    cqn, ckv, kpe_pad, kpe, qb, kb, vb, kb_tail, vb_tail, k, vt = _proj(
        x_prompt, norm_mix, w, *gains, keep=keep_p, tm=_row_tile(S, 512), kv=(*kv_args, *tabs))
    q = _q_up(cqn, *q_args, *tabs, tm=_row_tile(S, 512))
    tqa = _row_tile(S, 1024)
    flash = lambda bounded: functools.partial(_mla_flash, tq=tqa, td=min(tqa, 256), hg=2, bounded=bounded)
    late = (w_o[0], w_up[0], w_down[0])
    ride = [_can_ride(a, _flash_steps(B, ha, S, tqa, 2)) for a in late]
    oa, *cast = lax.cond(fast_a, flash(True), flash(False), q, k, vt, *(a for a, r in zip(late, ride) if r))
    cast = iter(cast)
    w["o"], w["up"], w["down"] = (next(cast) if r else a.astype(BF16) for a, r in zip(late, ride))
    u_tab = _band_vectors(rel_bias[0], shift_b)
    ob = lax.cond(fast_b, functools.partial(_band, bounded=True),
                  functools.partial(_band, bounded=False), qb, kb, vb, u_tab)
    h = _merge(x_prompt, oa, ob, w["o"], tm=_row_tile(S, 512))
    y_prompt = _ffn(h, norm_ffn, w["up"], w["down"], tm=_row_tile(S, 512), tf=2048)

    n_s = Bd * T
    xs = x_sample.reshape(1, n_s, D)
    cqn_s, ckv_s, kpe_pad_s, kpe_s, qb_s, kb_s, vb_s, kb_s32, vb_s32 = _proj(
        xs, norm_mix, w, *gains, keep=n_s, tm=_row_tile(n_s, 256))
    tabs_s = tuple(jnp.tile(t[P:P + T], (Bd, 1)) for t in tabs)
    q_s = _q_up(cqn_s, *q_args, *tabs_s, tm=n_s)
    oa_s = _mla_sample(
        q_s, cache_mla_ckv[0], cache_mla_kpe[0],
        _pad_rows(ckv_s.reshape(Bd, T, KV_LORA), rn), _pad_rows(kpe_pad_s.reshape(Bd, T, LANES), rn),
        *kv_args, tabs, t_new=T)
    w_c, w_n = _band_sample_vectors(rel_bias[0], Lb)
    ob_s = _band_sample(
        qb_s.reshape(Bd, T, hw), cache_band_k.reshape(Bd, Lb * hb, D_BAND), cache_band_v.reshape(Bd, Lb * hb, D_BAND),
        kb_s.reshape(Bd, T * hb, D_BAND), vb_s.reshape(Bd, T * hb, D_BAND), w_c, w_n, n_heads=hb)
    h_s = _merge(xs, oa_s.reshape(1, n_s, ha * V_MLA), ob_s.reshape(1, n_s, hw), w["o"], tm=n_s)
    y_sample = _ffn(h_s, norm_ffn, w["up"], w["down"], tm=n_s, tf=2048).reshape(Bd, T, D)

    return (y_prompt, y_sample,
            ckv[None], kpe[None],
            kb_tail.reshape(B, keep_p, hb, D_BAND)[None], vb_tail.reshape(B, keep_p, hb, D_BAND)[None],
            ckv_s.reshape(Bd, T, KV_LORA)[None], kpe_s.reshape(Bd, T, ROPE)[None],
            kb_s32.reshape(Bd, T, hb, D_BAND)[None], vb_s32.reshape(Bd, T, hb, D_BAND)[None])
```

```python
import functools

import jax
import jax.numpy as jnp
import numpy as np
from jax import lax
from jax.experimental import pallas as pl
from jax.experimental.pallas import tpu as pltpu

CHUNK = 64
EPS = 1e-6
NOPE = 128
ROPE = 64
HALF_ROPE = ROPE // 2
QK_MLA = NOPE + ROPE
V_MLA = 128
D_BAND = 128
BAND_CHUNKS = 8
BAND_PAST = BAND_CHUNKS * CHUNK
MAX_REL = 128
ROPE_BASE = 10000.0
Q_LORA = 512
KV_LORA = 256
NEG = -1e30
LOG2E = 1.4426950408889634

BOUND_MARGIN = 1.02
FAST_LIMIT = 60.0

LANES = 128
BF16_ROWS = 16
QK_PAD = 2 * LANES
AUG_LANE = ROPE
VMEM_LIMIT = 56 * 1024 * 1024

BF16 = jnp.bfloat16
F32 = jnp.float32

NT_DIMS = (((1,), (1,)), ((), ()))


def _params(*sem):
    return pltpu.CompilerParams(dimension_semantics=sem, vmem_limit_bytes=VMEM_LIMIT)


def _resident(a):
    return pl.BlockSpec(a.shape, lambda *_: (0,) * a.ndim, pipeline_mode=pl.Buffered(1))


def _rms(x, g):
    ms = jnp.mean(x * x, axis=-1, keepdims=True)
    return x * lax.rsqrt(ms + EPS) * g


def _dot(a, b):
    return jnp.dot(a, b, preferred_element_type=F32)


def _dot_nt(a, b):
    return lax.dot_general(a, b, NT_DIMS, preferred_element_type=F32)


N_PROJ_IN, N_PROJ_OUT, N_KV_IN = 12, 9, 7


def _proj_kernel(*refs, n_heads, n_mla_heads, tail_start):
    (x_ref, nm_ref, wcq_ref, wckv_ref, wkpe_ref, wqb_ref, wkb_ref, wvb_ref,
     gcq_ref, gckv_ref, gqb_ref, gkb_ref) = refs[:N_PROJ_IN]
    n_kv = N_KV_IN if n_mla_heads else 0
    outs = refs[N_PROJ_IN + n_kv:]
    cqn_ref, ckv_ref, kpe_ref, kpe_out_ref, qb_ref, kb_ref, vb_ref, kbt_ref, vbt_ref = outs[:N_PROJ_OUT]
    t = pl.program_id(1)
    xn = _rms(x_ref[0], nm_ref[...]).astype(BF16)
    cqn_ref[0] = _rms(_dot(xn, wcq_ref[...]), gcq_ref[...]).astype(BF16)
    ckv = _rms(_dot(xn, wckv_ref[...]), gckv_ref[...])
    ckv_ref[0] = ckv
    kpe = _dot(xn, wkpe_ref[...])
    kpe_ref[0] = kpe
    kpe_out_ref[0] = kpe[:, :ROPE]
    if n_mla_heads:
        wuk_ref, wuvt_ref, gn_ref, gr_ref, aug_ref, cos_ref, sin_ref = refs[N_PROJ_IN:N_PROJ_IN + n_kv]
        k_ref, vt_ref = outs[N_PROJ_OUT:]
        keys, vt = _expand_kv(ckv, kpe, wuk_ref[...], wuvt_ref[...], gn_ref[...], gr_ref[...],
                              aug_ref[...], cos_ref[...], sin_ref[...], n_mla_heads)
        for h in range(n_mla_heads):
            k_ref[0, h] = keys[h]
            vt_ref[0, h] = vt[h * V_MLA:(h + 1) * V_MLA, :].astype(BF16)
    qb = _dot(xn, wqb_ref[...])
    kb = _dot(xn, wkb_ref[...])
    vb = _dot(xn, wvb_ref[...])
    vb_ref[0] = vb.astype(BF16)
    kbn = []
    for h in range(n_heads):
        sl = slice(h * D_BAND, (h + 1) * D_BAND)
        qb_ref[0, :, sl] = _rms(qb[:, sl], gqb_ref[...]).astype(BF16)
        kbn.append(_rms(kb[:, sl], gkb_ref[...]))
    kbn = jnp.concatenate(kbn, axis=-1)
    kb_ref[0] = kbn.astype(BF16)

    @pl.when(t >= tail_start)
    def _():
        kbt_ref[0] = kbn
        vbt_ref[0] = vb


def _proj(x, norm_mix, w, g_cq, g_ckv, g_qb, g_kb, *, keep, tm, kv=None):
    B, S, D = x.shape
    hw = w["cols"]["qb"][1]
    n_heads = hw // D_BAND
    nt = S // tm
    tail_start = nt - keep // tm
    row = lambda b, t: (b, t, 0)
    tail = lambda b, t: (b, jnp.maximum(t - tail_start, 0), 0)
    out_shapes = (
        jax.ShapeDtypeStruct((B, S, Q_LORA), BF16),
        jax.ShapeDtypeStruct((B, S, KV_LORA), F32),
        jax.ShapeDtypeStruct((B, S, LANES), F32),
        jax.ShapeDtypeStruct((B, S, ROPE), F32),
        jax.ShapeDtypeStruct((B, S, hw), BF16),
        jax.ShapeDtypeStruct((B, S, hw), BF16),
        jax.ShapeDtypeStruct((B, S, hw), BF16),
        jax.ShapeDtypeStruct((B, keep, hw), F32),
        jax.ShapeDtypeStruct((B, keep, hw), F32),
    )
    out_specs = (
        pl.BlockSpec((1, tm, Q_LORA), row),
        pl.BlockSpec((1, tm, KV_LORA), row),
        pl.BlockSpec((1, tm, LANES), row),
        pl.BlockSpec((1, tm, ROPE), row),
        pl.BlockSpec((1, tm, hw), row),
        pl.BlockSpec((1, tm, hw), row),
        pl.BlockSpec((1, tm, hw), row),
        pl.BlockSpec((1, tm, hw), tail),
        pl.BlockSpec((1, tm, hw), tail),
    )
    w_specs = [pl.BlockSpec((D, width), functools.partial(lambda blk, b, t: (0, blk), blk),
                            pipeline_mode=pl.Buffered(1))
               for blk, width in (w["cols"][n] for n in ("cq", "ckv", "kpe", "qb", "kb", "vb"))]
    gains = (g_cq, g_ckv, g_qb, g_kb)
    kv_specs, kv_args, ha = [], (), 0
    if kv is not None:
        ha = kv[0].shape[1] // NOPE
        pos = lambda b, t: (t, 0)
        kv_specs = [_resident(a) for a in kv[:5]] + [pl.BlockSpec((tm, LANES), pos)] * 2
        kv_args = tuple(kv)
        out_shapes += (jax.ShapeDtypeStruct((B, ha, S, QK_PAD), BF16),
                       jax.ShapeDtypeStruct((B, ha, V_MLA, S), BF16))
        out_specs += (pl.BlockSpec((1, ha, tm, QK_PAD), lambda b, t: (b, 0, t, 0)),
                      pl.BlockSpec((1, ha, V_MLA, tm), lambda b, t: (b, 0, 0, t)))
    return pl.pallas_call(
        functools.partial(_proj_kernel, n_heads=n_heads, n_mla_heads=ha, tail_start=tail_start),
        grid=(B, nt),
        in_specs=[pl.BlockSpec((1, tm, D), row), _resident(norm_mix)]
        + w_specs + [_resident(a) for a in gains] + kv_specs,
        out_specs=out_specs,
        out_shape=out_shapes,
        compiler_params=_params("arbitrary", "arbitrary"),
        name="proj",
    )(x, norm_mix, *([w["in"]] * len(w_specs)), *gains, *kv_args)


def _q_up_kernel(cqn_ref, w_ref, gn_ref, gr_ref, grs_ref, aug_ref, cos_ref, sin_ref, q_ref,
                 *, n_heads):
    hw = n_heads * LANES
    hp = hw // 2
    y = _dot(cqn_ref[0], w_ref[...])
    low = lax.broadcasted_iota(jnp.int32, (1, LANES), 1) < ROPE
    for j in range(n_heads // 2):
        r = y[:, hw + j * LANES: hw + (j + 1) * LANES]
        rs = y[:, hw + hp + j * LANES: hw + hp + (j + 1) * LANES]
        r2 = r * r
        r2_of = (jnp.where(low, r2, 0.0), jnp.where(low, 0.0, r2))
        rinv = []
        for e in range(2):
            h = 2 * j + e
            n = y[:, h * LANES:(h + 1) * LANES]
            ss = jnp.sum(n * n + r2_of[e], axis=-1, keepdims=True)
            rinv.append(lax.rsqrt(ss * (1.0 / QK_MLA) + EPS))
            q_ref[0, h, :, 0:LANES] = (n * rinv[e] * gn_ref[...]).astype(BF16)
        rot = ((r * gr_ref[...]) * cos_ref[...] + (rs * grs_ref[...]) * sin_ref[...]) \
            * jnp.where(low, rinv[0], rinv[1])
        q_ref[0, 2 * j, :, LANES:QK_PAD] = (jnp.where(low, rot, 0.0) + aug_ref[...]).astype(BF16)
        q_ref[0, 2 * j + 1, :, LANES:QK_PAD] = (
            jnp.where(low, pltpu.roll(rot, ROPE, 1), 0.0) + aug_ref[...]).astype(BF16)


def _q_up(cqn, w, g_n, g_r, g_rs, aug, cos, sin, *, tm):
    B, S, _ = cqn.shape
    n_heads = w.shape[1] // (2 * LANES)
    row = lambda b, t: (b, t, 0)
    pos = lambda b, t: (t, 0)
    return pl.pallas_call(
        functools.partial(_q_up_kernel, n_heads=n_heads),
        grid=(B, S // tm),
        in_specs=[pl.BlockSpec((1, tm, Q_LORA), row), _resident(w),
                  _resident(g_n), _resident(g_r), _resident(g_rs), _resident(aug),
                  pl.BlockSpec((tm, LANES), pos), pl.BlockSpec((tm, LANES), pos)],
        out_specs=pl.BlockSpec((1, n_heads, tm, QK_PAD), lambda b, t: (b, 0, t, 0)),
        out_shape=jax.ShapeDtypeStruct((B, n_heads, S, QK_PAD), BF16),
        compiler_params=_params("arbitrary", "arbitrary"),
        name="q_up",
    )(cqn, w, g_n, g_r, g_rs, aug, cos, sin)


def _expand_kv(c, kpe, wuk, wuvt, g_n, g_r, k_aug, cos, sin, n_heads):
    cb = c.astype(BF16)
    kn = _dot(cb, wuk)
    vt = _dot_nt(wuvt, cb)
    a = kpe * g_r
    sspe = jnp.sum(kpe * kpe, axis=-1, keepdims=True)
    a_sw = pltpu.roll(a, HALF_ROPE, 1) + pltpu.roll(a, LANES - HALF_ROPE, 1)
    low = lax.broadcasted_iota(jnp.int32, (1, LANES), 1) < ROPE
    rot = jnp.where(low, a * cos + a_sw * sin, 0.0)
    keys = []
    for h in range(n_heads):
        n = kn[:, h * NOPE:(h + 1) * NOPE]
        ss = jnp.sum(n * n, axis=-1, keepdims=True) + sspe
        rinv = lax.rsqrt(ss * (1.0 / QK_MLA) + EPS)
        keys.append(jnp.concatenate([(n * rinv * g_n).astype(BF16),
                                     (rot * rinv + k_aug).astype(BF16)], axis=-1))
    return keys, vt


N_FLASH_IN = 3


def _mla_flash_kernel(*refs, tq, td, hg, bounded, n_riders):
    q_ref, k_ref, vt_ref = refs[:N_FLASH_IN]
    o_ref = refs[N_FLASH_IN + n_riders]
    m_sc, l_sc, acc_sc = refs[N_FLASH_IN + 1 + 2 * n_riders:]
    for src, dst in zip(refs[N_FLASH_IN:N_FLASH_IN + n_riders],
                        refs[N_FLASH_IN + 1 + n_riders:N_FLASH_IN + 1 + 2 * n_riders]):
        dst[...] = src[...].astype(BF16)
    i = pl.program_id(2)
    if not bounded:
        m_sc[...] = jnp.full(m_sc.shape, -jnp.inf, F32)
    l_sc[...] = jnp.zeros(l_sc.shape, F32)
    acc_sc[...] = jnp.zeros(acc_sc.shape, F32)

    def step(tiles):
        sts = [(hh, t, _dot_nt(k_ref[0, hh, pl.ds(t[0], t[1]), :], q_ref[0, hh, t[2]:, :]))
               for t in tiles for hh in range(hg)]
        for hh, (k0, kn, q0, mask), st in sts:
            if mask is not None:
                st = jnp.where(mask, st, NEG)
            vt = vt_ref[0, hh, :, pl.ds(k0, kn)]
            if bounded:
                p = jnp.exp2(st)
                l_sc[hh, :, q0:] += jnp.sum(p, axis=0, keepdims=True)
                acc_sc[hh, :, q0:] += _dot(vt, p.astype(BF16))
            else:
                m_prev = m_sc[hh, :, q0:]
                m_new = jnp.maximum(m_prev, jnp.max(st, axis=0, keepdims=True))
                alpha = jnp.exp2(m_prev - m_new)
                p = jnp.exp2(st - m_new)
                l_sc[hh, :, q0:] = alpha * l_sc[hh, :, q0:] + jnp.sum(p, axis=0, keepdims=True)
                acc_sc[hh, :, q0:] = acc_sc[hh, :, q0:] * alpha + _dot(vt, p.astype(BF16))
                m_sc[hh, :, q0:] = m_new

    def body(j, carry):
        step([(pl.multiple_of(j * tq, tq), tq, 0, None)])
        return carry

    lax.fori_loop(0, i, body, 0)
    diag = []
    for d in range(tq // td):
        nq = tq - d * td
        q_chunk = lax.broadcasted_iota(jnp.int32, (td, nq), 1) // CHUNK
        k_chunk = lax.broadcasted_iota(jnp.int32, (td, nq), 0) // CHUNK
        diag.append((pl.multiple_of(i * tq + d * td, td), td, d * td, k_chunk <= q_chunk))
    step(diag)
    for hh in range(hg):
        o_ref[0, :, hh * V_MLA:(hh + 1) * V_MLA] = (acc_sc[hh] / l_sc[hh]).T.astype(o_ref.dtype)


def _flash_steps(B, H, S, tq, hg):
    return B * (H // hg) * (S // tq)


def _can_ride(a, n_steps):
    return a.shape[0] % n_steps == 0 and (a.shape[0] // n_steps) % BF16_ROWS == 0


def _mla_flash(q, k, vt, *riders, tq, td, hg, bounded):
    B, H, S, _ = q.shape
    nh, nq = H // hg, S // tq
    step = lambda b, h, i: ((b * nh + h) * nq + i, 0)
    rider_spec = lambda a: pl.BlockSpec((a.shape[0] // (B * nh * nq), a.shape[1]), step)
    return pl.pallas_call(
        functools.partial(_mla_flash_kernel, tq=tq, td=td, hg=hg, bounded=bounded, n_riders=len(riders)),
        grid=(B, nh, nq),
        in_specs=[pl.BlockSpec((1, hg, tq, QK_PAD), lambda b, h, i: (b, h, i, 0)),
                  pl.BlockSpec((1, hg, S, QK_PAD), lambda b, h, i: (b, h, 0, 0)),
                  pl.BlockSpec((1, hg, V_MLA, S), lambda b, h, i: (b, h, 0, 0))]
        + [rider_spec(a) for a in riders],
        out_specs=[pl.BlockSpec((1, tq, hg * V_MLA), lambda b, h, i: (b, i, h))]
        + [rider_spec(a) for a in riders],
        out_shape=[jax.ShapeDtypeStruct((B, S, H * V_MLA), BF16)]
        + [jax.ShapeDtypeStruct(a.shape, BF16) for a in riders],
        scratch_shapes=[pltpu.VMEM((hg, 1, tq), F32), pltpu.VMEM((hg, 1, tq), F32),
                        pltpu.VMEM((hg, V_MLA, tq), F32)],
        compiler_params=_params("arbitrary", "arbitrary", "arbitrary"),
        name="mla_flash_bounded" if bounded else "mla_flash",
    )(q, k, vt, *riders)


def _toeplitz(u_row, rows, cols, stride=1):
    u = jnp.broadcast_to(u_row, (rows, u_row.shape[-1]))
    return pltpu.roll(u, 0, 1, stride=stride, stride_axis=0)[:, :cols]


def _softmax_pv(parts, bounded):
    m = None
    if not bounded:
        for s, _ in parts:
            mx = jnp.max(s, axis=-1, keepdims=True)
            m = mx if m is None else jnp.maximum(m, mx)
    l = None
    acc = None
    for s, v in parts:
        p = jnp.exp2(s if bounded else s - m)
        ps = jnp.sum(p, axis=-1, keepdims=True)
        pv = _dot(p.astype(BF16), v)
        l = ps if l is None else l + ps
        acc = pv if acc is None else acc + pv
    return acc / l


def _band_kernel(q_ref, kc_ref, kp_ref, vc_ref, vp_ref, u_ref, o_ref, bc_sc, bp_sc,
                 *, n_heads, splits, bounded):
    b, g = pl.program_id(0), pl.program_id(1)
    tg = bc_sc.shape[1]

    @pl.when((b == 0) & (g == 0))
    def _():
        qc = lax.broadcasted_iota(jnp.int32, (tg, tg), 0) // CHUNK
        kc = lax.broadcasted_iota(jnp.int32, (tg, tg), 1) // CHUNK
        for h in range(n_heads):
            bc_sc[h] = jnp.where(kc <= qc, _toeplitz(u_ref[h, 0:1, :], tg, tg), NEG)
            bp_sc[h] = jnp.where(kc >= qc, _toeplitz(u_ref[h, 1:2, :], tg, tg), NEG)

    no_prev = jnp.where(g == 0, NEG, 0.0)
    tr = tg // splits
    for h0 in range(0, n_heads, 2):
        scores = []
        for h in (h0, h0 + 1):
            sl = slice(h * D_BAND, (h + 1) * D_BAND)
            for a in range(splits):
                rows, old, new = slice(a * tr, (a + 1) * tr), slice(a * tr, tg), slice(0, (a + 1) * tr)
                q = q_ref[0, rows, sl]
                scores.append((rows, old, new, sl,
                               _dot_nt(q, kp_ref[0, old, sl]) + (bp_sc[h, rows, old] + no_prev),
                               _dot_nt(q, kc_ref[0, new, sl]) + bc_sc[h, rows, new]))
        for rows, old, new, sl, s_prev, s_cur in scores:
            parts = [(s_prev, vp_ref[0, old, sl]), (s_cur, vc_ref[0, new, sl])]
            o_ref[0, rows, sl] = _softmax_pv(parts, bounded).astype(o_ref.dtype)


def _band(qb, kb, vb, u_tab, *, bounded):
    B, S, hw = qb.shape
    H = hw // D_BAND
    tg = BAND_PAST
    cur = lambda b, g: (b, g, 0)
    prev = lambda b, g: (b, jnp.maximum(g - 1, 0), 0)
    blk = (1, tg, hw)
    return pl.pallas_call(
        functools.partial(_band_kernel, n_heads=H, splits=4, bounded=bounded),
        grid=(B, S // tg),
        in_specs=[pl.BlockSpec(blk, cur), pl.BlockSpec(blk, cur), pl.BlockSpec(blk, prev),
                  pl.BlockSpec(blk, cur), pl.BlockSpec(blk, prev),
                  _resident(u_tab)],
        out_specs=pl.BlockSpec(blk, cur),
        out_shape=jax.ShapeDtypeStruct((B, S, hw), BF16),
        scratch_shapes=[pltpu.VMEM((H, tg, tg), F32), pltpu.VMEM((H, tg, tg), F32)],
        compiler_params=_params("arbitrary", "arbitrary"),
        name="band_bounded" if bounded else "band",
    )(qb, kb, kb, vb, vb, u_tab)


def _mla_sample_kernel(q_ref, cc_ref, pc_ref, cn_ref, pn_ref, wuk_ref, wuvt_ref, gn_ref, gr_ref,
                       aug_ref, cosc_ref, sinc_ref, cosn_ref, sinn_ref, o_ref, *, n_heads, t_new):
    args = (wuk_ref[...], wuvt_ref[...], gn_ref[...], gr_ref[...], aug_ref[...])
    pc = jnp.concatenate([pc_ref[0], jnp.zeros((pc_ref.shape[1], LANES - ROPE), F32)], axis=-1)
    kc, vtc = _expand_kv(cc_ref[0], pc, *args, cosc_ref[...], sinc_ref[...], n_heads)
    kn, vtn = _expand_kv(cn_ref[0], pn_ref[0], *args, cosn_ref[...], sinn_ref[...], n_heads)
    rows_n = cn_ref.shape[1]
    new_ok = lax.broadcasted_iota(jnp.int32, (q_ref.shape[2], rows_n), 1) < t_new
    scores = [(_dot_nt(q_ref[0, h], kc[h]), jnp.where(new_ok, _dot_nt(q_ref[0, h], kn[h]), NEG))
              for h in range(n_heads)]
    for h, (s_c, s_n) in enumerate(scores):
        m = jnp.maximum(jnp.max(s_c, axis=-1, keepdims=True), jnp.max(s_n, axis=-1, keepdims=True))
        p_c = jnp.exp2(s_c - m)
        p_n = jnp.exp2(s_n - m)
        l = jnp.sum(p_c, axis=-1, keepdims=True) + jnp.sum(p_n, axis=-1, keepdims=True)
        sl = slice(h * V_MLA, (h + 1) * V_MLA)
        pv = (_dot_nt(p_c.astype(BF16), vtc[sl].astype(BF16))
              + _dot_nt(p_n.astype(BF16), vtn[sl].astype(BF16)))
        o_ref[0, :, sl] = (pv / l).astype(o_ref.dtype)


def _mla_sample(q, ckv_c, kpe_c, ckv_n, kpe_n, w_uk, w_uvt, g_n, g_r, k_aug, tabs, *, t_new):
    Bd = ckv_c.shape[0]
    H, T = q.shape[1], q.shape[2] // Bd
    P = ckv_c.shape[1]
    rn = ckv_n.shape[1]
    assert P % rn == 0
    old = lambda: pl.BlockSpec((P, LANES), lambda b: (0, 0), pipeline_mode=pl.Buffered(1))
    new = lambda: pl.BlockSpec((rn, LANES), lambda b: (P // rn, 0), pipeline_mode=pl.Buffered(1))
    bat = lambda b: (b, 0, 0)
    return pl.pallas_call(
        functools.partial(_mla_sample_kernel, n_heads=H, t_new=t_new),
        grid=(Bd,),
        in_specs=[pl.BlockSpec((1, H, T, QK_PAD), lambda b: (0, 0, b, 0)),
                  pl.BlockSpec((1, P, KV_LORA), bat), pl.BlockSpec((1, P, ROPE), bat),
                  pl.BlockSpec((1, rn, KV_LORA), bat), pl.BlockSpec((1, rn, LANES), bat),
                  _resident(w_uk), _resident(w_uvt), _resident(g_n), _resident(g_r), _resident(k_aug),
                  old(), old(), new(), new()],
        out_specs=pl.BlockSpec((1, T, H * V_MLA), bat),
        out_shape=jax.ShapeDtypeStruct((Bd, T, H * V_MLA), BF16),
        compiler_params=_params("arbitrary"),
        name="mla_sample",
    )(q, ckv_c, kpe_c, ckv_n, kpe_n, w_uk, w_uvt, g_n, g_r, k_aug, *tabs, *tabs)


def _band_sample_kernel(q_ref, kc_ref, vc_ref, kn_ref, vn_ref, wc_ref, wn_ref, o_ref, *, n_heads):
    T = q_ref.shape[1]
    heads = [slice(h * D_BAND, (h + 1) * D_BAND) for h in range(n_heads)]
    q = jnp.concatenate([q_ref[0, :, sl] for sl in heads], axis=0)
    tab = lambda w_ref, cols: jnp.concatenate(
        [_toeplitz(w_ref[h:h + 1, :], T, cols, stride=n_heads) for h in range(n_heads)], axis=0)
    s_c = _dot_nt(q, kc_ref[0].astype(BF16)) + tab(wc_ref, kc_ref.shape[1])
    s_n = _dot_nt(q, kn_ref[0]) + tab(wn_ref, kn_ref.shape[1])
    parts = [(s_c, vc_ref[0].astype(BF16)), (s_n, vn_ref[0])]
    o = _softmax_pv(parts, False).astype(o_ref.dtype)
    for h, sl in enumerate(heads):
        o_ref[0, :, sl] = o[h * T:(h + 1) * T]


def _band_sample(q, k_c, v_c, k_n, v_n, w_c, w_n, *, n_heads):
    Bd = q.shape[0]
    bat = lambda b: (b, 0, 0)
    blk = lambda a: pl.BlockSpec((1,) + a.shape[1:], bat)
    return pl.pallas_call(
        functools.partial(_band_sample_kernel, n_heads=n_heads),
        grid=(Bd,),
        in_specs=[blk(q), blk(k_c), blk(v_c), blk(k_n), blk(v_n), _resident(w_c), _resident(w_n)],
        out_specs=blk(q),
        out_shape=jax.ShapeDtypeStruct(q.shape, BF16),
        compiler_params=_params("arbitrary"),
        name="band_sample",
    )(q, k_c, v_c, k_n, v_n, w_c, w_n)


def _merge_kernel(x_ref, oa_ref, ob_ref, wo_ref, h_ref):
    half = oa_ref.shape[-1]
    h_ref[0] = x_ref[0] + _dot(oa_ref[0], wo_ref[:half, :]) + _dot(ob_ref[0], wo_ref[half:, :])


def _merge(x, oa, ob, w_o, *, tm):
    B, S, D = x.shape
    row = lambda b, t: (b, t, 0)
    return pl.pallas_call(
        _merge_kernel,
        grid=(B, S // tm),
        in_specs=[pl.BlockSpec((1, tm, D), row), pl.BlockSpec((1, tm, oa.shape[-1]), row),
                  pl.BlockSpec((1, tm, ob.shape[-1]), row),
                  _resident(w_o)],
        out_specs=pl.BlockSpec((1, tm, D), row),
        out_shape=jax.ShapeDtypeStruct((B, S, D), F32),
        compiler_params=_params("arbitrary", "arbitrary"),
        name="merge",
    )(x, oa, ob, w_o)


def _ffn_kernel(h_ref, g_ref, wup_ref, wdn_ref, y_ref, hn_sc):
    j = pl.program_id(2)

    @pl.when(j == 0)
    def _():
        h = h_ref[0]
        hn_sc[...] = _rms(h, g_ref[...]).astype(BF16)
        y_ref[0] = h

    u = jnp.maximum(_dot(hn_sc[...], wup_ref[...]), 0.0)
    y_ref[0] += _dot((u * u).astype(BF16), wdn_ref[...])


def _ffn(h, g, w_up, w_down, *, tm, tf):
    B, S, D = h.shape
    F = w_up.shape[1]
    row = lambda b, t, j: (b, t, 0)
    return pl.pallas_call(
        _ffn_kernel,
        grid=(B, S // tm, F // tf),
        in_specs=[pl.BlockSpec((1, tm, D), row), _resident(g),
                  pl.BlockSpec((D, tf), lambda b, t, j: (0, j)),
                  pl.BlockSpec((tf, D), lambda b, t, j: (j, 0))],
        out_specs=pl.BlockSpec((1, tm, D), row),
        out_shape=jax.ShapeDtypeStruct((B, S, D), F32),
        scratch_shapes=[pltpu.VMEM((tm, D), BF16)],
        compiler_params=_params("arbitrary", "arbitrary", "arbitrary"),
        name="ffn",
    )(h, g, w_up, w_down)


def _rope_tables(n_pos):
    inv = 1.0 / (ROPE_BASE ** (jnp.arange(0, ROPE, 2, dtype=F32) / ROPE))
    n_hi = -(-n_pos // CHUNK)
    hi = (jnp.arange(n_hi, dtype=F32) * CHUNK)[:, None, None] * inv
    lo = jnp.arange(CHUNK, dtype=F32)[None, :, None] * inv
    ch, sh, cl, sl = jnp.cos(hi), jnp.sin(hi), jnp.cos(lo), jnp.sin(lo)
    c = (ch * cl - sh * sl).reshape(n_hi * CHUNK, HALF_ROPE)[:n_pos]
    s = (sh * cl + ch * sl).reshape(n_hi * CHUNK, HALF_ROPE)[:n_pos]
    return jnp.concatenate([c, c, c, c], axis=1), jnp.concatenate([-s, s, -s, s], axis=1)


def _pad_lanes(a, width=LANES):
    return jnp.pad(a, [(0, 0)] * (a.ndim - 1) + [(0, width - a.shape[-1])])


def _pad_rows(a, rows):
    return jnp.pad(a, [(0, 0), (0, rows - a.shape[1]), (0, 0)])


def _swap_halves(a):
    return jnp.concatenate([a[..., HALF_ROPE:], a[..., :HALF_ROPE]], axis=-1)


def _prep_weights(w_in, w_uq, w_uk, w_uv):
    low = Q_LORA + KV_LORA + ROPE
    hb3 = (w_in.shape[1] - low) // 3
    w = {"in": jnp.concatenate([w_in[:, low:], w_in[:, :low],
                                jnp.zeros((w_in.shape[0], LANES - ROPE), w_in.dtype)], axis=1).astype(BF16)}
    widths = (("qb", hb3), ("kb", hb3), ("vb", hb3), ("cq", Q_LORA), ("ckv", KV_LORA), ("kpe", LANES))
    w["cols"], start = {}, 0
    for name, width in widths:
        assert start % width == 0
        w["cols"][name] = (start // width, width)
        start += width
    ha = w_uq.shape[1] // QK_MLA
    uq = w_uq.astype(BF16).reshape(Q_LORA, ha, QK_MLA)
    nope = uq[:, :, :NOPE].reshape(Q_LORA, ha * NOPE)
    rope = uq[:, :, NOPE:]
    w["uq"] = jnp.concatenate(
        [nope, rope.reshape(Q_LORA, ha * ROPE), _swap_halves(rope).reshape(Q_LORA, ha * ROPE)], axis=1)
    w["uk"] = w_uk.astype(BF16)
    w["uvt"] = w_uv.astype(BF16).T
    return w


def _softmax_setup(g_qa, g_ka, g_qb, g_kb, rel_bias):
    c_a = QK_MLA ** -0.5 * LOG2E
    c_b = D_BAND ** -0.5 * LOG2E
    amax = lambda a: jnp.max(jnp.abs(a))
    bound_a = c_a * QK_MLA * amax(g_qa) * amax(g_ka) * BOUND_MARGIN
    bound_b = c_b * D_BAND * amax(g_qb) * amax(g_kb) * BOUND_MARGIN + LOG2E * amax(rel_bias)
    fast_a = bound_a <= FAST_LIMIT
    fast_b = bound_b <= FAST_LIMIT
    shift_a = jnp.where(fast_a, bound_a, 0.0)
    shift_b = jnp.where(fast_b, bound_b, 0.0)
    lane = jnp.arange(LANES)[None, :]
    twice = lambda a: jnp.concatenate([a, a], axis=-1)
    g = {"qa_n": g_qa[None, :NOPE] * c_a, "qa_r": twice(g_qa[None, NOPE:]) * c_a,
         "qa_rs": twice(_swap_halves(g_qa[None, NOPE:])) * c_a,
         "q_aug": (lane == AUG_LANE).astype(F32),
         "ka_n": g_ka[None, :NOPE], "ka_r": _pad_lanes(g_ka[None, NOPE:]),
         "k_aug": jnp.where(lane == AUG_LANE, -shift_a, 0.0).astype(F32),
         "qb": g_qb * c_b}
    return g, fast_a, fast_b, shift_b


def _band_vectors(rel_bias, shift):
    t = BAND_PAST
    e = np.arange(2 * t)
    e = np.where(e < t, e, e - 2 * t)
    idx = np.stack([np.clip(-e, -MAX_REL, MAX_REL), np.clip(t - e, -MAX_REL, MAX_REL)]) + MAX_REL
    return (rel_bias[:, idx] * LOG2E - shift).astype(F32)


def _band_sample_vectors(rel_bias, lb):
    H = rel_bias.shape[0]

    def interleave(dist, n_pos, n):
        e = np.arange(n)
        e = np.where(e < n_pos, e, e - n)
        u = rel_bias[:, np.clip(dist - e, -MAX_REL, MAX_REL) + MAX_REL] * LOG2E
        own = np.arange(H)[:, None, None] == np.arange(H)[None, None, :]
        return jnp.where(own, u[:, :, None], NEG).reshape(H, n * H).astype(F32)

    return interleave(lb, lb, lb + LANES), interleave(0, CHUNK, 2 * CHUNK)


def _row_tile(n, pref):
    return pref if n % pref == 0 else n


def kernel(x_prompt, x_sample, cache_mla_ckv, cache_mla_kpe, cache_band_k, cache_band_v,
           norm_mix, w_in, g_cq, w_uq, g_ckv, w_uk, w_uv, g_qa, g_ka, g_qb, g_kb, rel_bias,
           w_o, norm_ffn, w_up, w_down):
    depth = w_in.shape[0]
    assert depth == 1, "single-layer step"
    B, S, D = x_prompt.shape
    Bd, T, _ = x_sample.shape
    P = cache_mla_ckv.shape[2]
    Lb = cache_band_k.shape[2]
    keep_p = min(BAND_PAST, S)
    assert S % BAND_PAST == 0 and T <= CHUNK

    w = _prep_weights(w_in[0], w_uq[0], w_uk[0], w_uv[0])
    g, fast_a, fast_b, shift_b = _softmax_setup(g_qa[0], g_ka[0], g_qb, g_kb, rel_bias[0])
    ha = w["uk"].shape[1] // NOPE
    hb = rel_bias.shape[1]
    hw = hb * D_BAND
    gains = (g_cq, g_ckv, g["qb"], g_kb)
    q_args = (w["uq"], g["qa_n"], g["qa_r"], g["qa_rs"], g["q_aug"])
    kv_args = (w["uk"], w["uvt"], g["ka_n"], g["ka_r"], g["k_aug"])

    rn = LANES
    tabs = _rope_tables(max(S, P + rn))
    cqn, ckv, kpe_pad, kpe, qb, kb, vb, kb_tail, vb_tail, k, vt = _proj(
        x_prompt, norm_mix, w, *gains, keep=keep_p, tm=_row_tile(S, 512), kv=(*kv_args, *tabs))
    q = _q_up(cqn, *q_args, *tabs, tm=_row_tile(S, 512))
    tqa = _row_tile(S, 1024)
    flash = lambda bounded: functools.partial(_mla_flash, tq=tqa, td=min(tqa, 256), hg=2, bounded=bounded)
    late = (w_o[0], w_up[0], w_down[0])
    ride = [_can_ride(a, _flash_steps(B, ha, S, tqa, 2)) for a in late]
    oa, *cast = lax.cond(fast_a, flash(True), flash(False), q, k, vt, *(a for a, r in zip(late, ride) if r))
    cast = iter(cast)
    w["o"], w["up"], w["down"] = (next(cast) if r else a.astype(BF16) for a, r in zip(late, ride))
    u_tab = _band_vectors(rel_bias[0], shift_b)
    ob = lax.cond(fast_b, functools.partial(_band, bounded=True),
                  functools.partial(_band, bounded=False), qb, kb, vb, u_tab)
    h = _merge(x_prompt, oa, ob, w["o"], tm=_row_tile(S, 512))
    y_prompt = _ffn(h, norm_ffn, w["up"], w["down"], tm=_row_tile(S, 512), tf=2048)

    n_s = Bd * T
    xs = x_sample.reshape(1, n_s, D)
    cqn_s, ckv_s, kpe_pad_s, kpe_s, qb_s, kb_s, vb_s, kb_s32, vb_s32 = _proj(
        xs, norm_mix, w, *gains, keep=n_s, tm=_row_tile(n_s, 256))
    tabs_s = tuple(jnp.tile(t[P:P + T], (Bd, 1)) for t in tabs)
    q_s = _q_up(cqn_s, *q_args, *tabs_s, tm=n_s)
    oa_s = _mla_sample(
        q_s, cache_mla_ckv[0], cache_mla_kpe[0],
        _pad_rows(ckv_s.reshape(Bd, T, KV_LORA), rn), _pad_rows(kpe_pad_s.reshape(Bd, T, LANES), rn),
        *kv_args, tabs, t_new=T)
    w_c, w_n = _band_sample_vectors(rel_bias[0], Lb)
    ob_s = _band_sample(
        qb_s.reshape(Bd, T, hw), cache_band_k.reshape(Bd, Lb * hb, D_BAND), cache_band_v.reshape(Bd, Lb * hb, D_BAND),
        kb_s.reshape(Bd, T * hb, D_BAND), vb_s.reshape(Bd, T * hb, D_BAND), w_c, w_n, n_heads=hb)
    h_s = _merge(xs, oa_s.reshape(1, n_s, ha * V_MLA), ob_s.reshape(1, n_s, hw), w["o"], tm=n_s)
    y_sample = _ffn(h_s, norm_ffn, w["up"], w["down"], tm=n_s, tf=2048).reshape(Bd, T, D)

    return (y_prompt, y_sample,
            ckv[None], kpe[None],
            kb_tail.reshape(B, keep_p, hb, D_BAND)[None], vb_tail.reshape(B, keep_p, hb, D_BAND)[None],
            ckv_s.reshape(Bd, T, KV_LORA)[None], kpe_s.reshape(Bd, T, ROPE)[None],
            kb_s32.reshape(Bd, T, hb, D_BAND)[None], vb_s32.reshape(Bd, T, hb, D_BAND)[None])
```

```python
import functools

import jax
import jax.numpy as jnp
import numpy as np
from jax import lax
from jax.experimental import pallas as pl
from jax.experimental.pallas import tpu as pltpu

CHUNK = 64
EPS = 1e-6
NOPE = 128
ROPE = 64
HALF_ROPE = ROPE // 2
QK_MLA = NOPE + ROPE
V_MLA = 128
D_BAND = 128
BAND_CHUNKS = 8
BAND_PAST = BAND_CHUNKS * CHUNK
MAX_REL = 128
ROPE_BASE = 10000.0
Q_LORA = 512
KV_LORA = 256
NEG = -1e30
LOG2E = 1.4426950408889634

BOUND_MARGIN = 1.02
FAST_LIMIT = 60.0

LANES = 128
BF16_ROWS = 16
QK_PAD = 2 * LANES
AUG_LANE = ROPE
VMEM_LIMIT = 56 * 1024 * 1024
ROWS_PROJ = 512
ROWS_PROJ_SAMPLE = 256
ROWS_Q_UP = 512
ROWS_FLASH = 1024
ROWS_FLASH_DIAG = 256
ROWS_MERGE = 512
ROWS_FFN = 512
COLS_FFN = 2048

BF16 = jnp.bfloat16
F32 = jnp.float32

NT_DIMS = (((1,), (1,)), ((), ()))


def _params(*sem):
    return pltpu.CompilerParams(dimension_semantics=sem, vmem_limit_bytes=VMEM_LIMIT)


def _resident(a):
    return pl.BlockSpec(a.shape, lambda *_: (0,) * a.ndim, pipeline_mode=pl.Buffered(1))


def _rms(x, g):
    ms = jnp.mean(x * x, axis=-1, keepdims=True)
    return x * lax.rsqrt(ms + EPS) * g


def _dot(a, b):
    return jnp.dot(a, b, preferred_element_type=F32)


def _dot_nt(a, b):
    return lax.dot_general(a, b, NT_DIMS, preferred_element_type=F32)


N_PROJ_IN, N_PROJ_OUT, N_KV_IN = 12, 9, 7


def _proj_kernel(*refs, n_heads, n_mla_heads, tail_start):
    (x_ref, nm_ref, wcq_ref, wckv_ref, wkpe_ref, wqb_ref, wkb_ref, wvb_ref,
     gcq_ref, gckv_ref, gqb_ref, gkb_ref) = refs[:N_PROJ_IN]
    n_kv = N_KV_IN if n_mla_heads else 0
    outs = refs[N_PROJ_IN + n_kv:]
    cqn_ref, ckv_ref, kpe_ref, kpe_out_ref, qb_ref, kb_ref, vb_ref, kbt_ref, vbt_ref = outs[:N_PROJ_OUT]
    t = pl.program_id(1)
    xn = _rms(x_ref[0], nm_ref[...]).astype(BF16)
    cqn_ref[0] = _rms(_dot(xn, wcq_ref[...]), gcq_ref[...]).astype(BF16)
    ckv = _rms(_dot(xn, wckv_ref[...]), gckv_ref[...])
    ckv_ref[0] = ckv
    kpe = _dot(xn, wkpe_ref[...])
    kpe_ref[0] = kpe
    kpe_out_ref[0] = kpe[:, :ROPE]
    if n_mla_heads:
        wuk_ref, wuvt_ref, gn_ref, gr_ref, aug_ref, cos_ref, sin_ref = refs[N_PROJ_IN:N_PROJ_IN + n_kv]
        k_ref, vt_ref = outs[N_PROJ_OUT:]
        keys, vt = _expand_kv(ckv, kpe, wuk_ref[...], wuvt_ref[...], gn_ref[...], gr_ref[...],
                              aug_ref[...], cos_ref[...], sin_ref[...], n_mla_heads)
        for h in range(n_mla_heads):
            k_ref[0, h] = keys[h]
            vt_ref[0, h] = vt[h * V_MLA:(h + 1) * V_MLA, :].astype(BF16)
    qb = _dot(xn, wqb_ref[...])
    kb = _dot(xn, wkb_ref[...])
    vb = _dot(xn, wvb_ref[...])
    vb_ref[0] = vb.astype(BF16)
    kbn = []
    for h in range(n_heads):
        sl = slice(h * D_BAND, (h + 1) * D_BAND)
        qb_ref[0, :, sl] = _rms(qb[:, sl], gqb_ref[...]).astype(BF16)
        kbn.append(_rms(kb[:, sl], gkb_ref[...]))
    kbn = jnp.concatenate(kbn, axis=-1)
    kb_ref[0] = kbn.astype(BF16)

    @pl.when(t >= tail_start)
    def _():
        kbt_ref[0] = kbn
        vbt_ref[0] = vb


def _proj(x, norm_mix, w, g_cq, g_ckv, g_qb, g_kb, *, keep, tm, kv=None):
    B, S, D = x.shape
    n_heads = w["qb"].shape[1] // D_BAND
    hw = n_heads * D_BAND
    nt = S // tm
    tail_start = nt - keep // tm
    row = lambda b, t: (b, t, 0)
    tail = lambda b, t: (b, jnp.maximum(t - tail_start, 0), 0)
    out_shapes = (
        jax.ShapeDtypeStruct((B, S, Q_LORA), BF16),
        jax.ShapeDtypeStruct((B, S, KV_LORA), F32),
        jax.ShapeDtypeStruct((B, S, LANES), F32),
        jax.ShapeDtypeStruct((B, S, ROPE), F32),
        jax.ShapeDtypeStruct((B, S, hw), BF16),
        jax.ShapeDtypeStruct((B, S, hw), BF16),
        jax.ShapeDtypeStruct((B, S, hw), BF16),
        jax.ShapeDtypeStruct((B, keep, hw), F32),
        jax.ShapeDtypeStruct((B, keep, hw), F32),
    )
    out_specs = (
        pl.BlockSpec((1, tm, Q_LORA), row),
        pl.BlockSpec((1, tm, KV_LORA), row),
        pl.BlockSpec((1, tm, LANES), row),
        pl.BlockSpec((1, tm, ROPE), row),
        pl.BlockSpec((1, tm, hw), row),
        pl.BlockSpec((1, tm, hw), row),
        pl.BlockSpec((1, tm, hw), row),
        pl.BlockSpec((1, tm, hw), tail),
        pl.BlockSpec((1, tm, hw), tail),
    )
    weights = (w["cq"], w["ckv"], w["kpe"], w["qb"], w["kb"], w["vb"])
    gains = (g_cq, g_ckv, g_qb, g_kb)
    kv_specs, kv_args, ha = [], (), 0
    if kv is not None:
        ha = kv[0].shape[1] // NOPE
        pos = lambda b, t: (t, 0)
        kv_specs = [_resident(a) for a in kv[:5]] + [pl.BlockSpec((tm, LANES), pos)] * 2
        kv_args = tuple(kv)
        out_shapes += (jax.ShapeDtypeStruct((B, ha, S, QK_PAD), BF16),
                       jax.ShapeDtypeStruct((B, ha, V_MLA, S), BF16))
        out_specs += (pl.BlockSpec((1, ha, tm, QK_PAD), lambda b, t: (b, 0, t, 0)),
                      pl.BlockSpec((1, ha, V_MLA, tm), lambda b, t: (b, 0, 0, t)))
    return pl.pallas_call(
        functools.partial(_proj_kernel, n_heads=n_heads, n_mla_heads=ha, tail_start=tail_start),
        grid=(B, nt),
        in_specs=[pl.BlockSpec((1, tm, D), row), _resident(norm_mix)]
        + [_resident(a) for a in weights] + [_resident(a) for a in gains] + kv_specs,
        out_specs=out_specs,
        out_shape=out_shapes,
        compiler_params=_params("arbitrary", "arbitrary"),
        name="proj",
    )(x, norm_mix, *weights, *gains, *kv_args)


def _q_up_kernel(cqn_ref, w_ref, gn_ref, gr_ref, grs_ref, aug_ref, cos_ref, sin_ref, q_ref,
                 *, n_heads):
    hw = n_heads * LANES
    hp = hw // 2
    y = _dot(cqn_ref[0], w_ref[...])
    low = lax.broadcasted_iota(jnp.int32, (1, LANES), 1) < ROPE
    for j in range(n_heads // 2):
        r = y[:, hw + j * LANES: hw + (j + 1) * LANES]
        rs = y[:, hw + hp + j * LANES: hw + hp + (j + 1) * LANES]
        r2 = r * r
        r2_of = (jnp.where(low, r2, 0.0), jnp.where(low, 0.0, r2))
        rinv = []
        for e in range(2):
            h = 2 * j + e
            n = y[:, h * LANES:(h + 1) * LANES]
            ss = jnp.sum(n * n + r2_of[e], axis=-1, keepdims=True)
            rinv.append(lax.rsqrt(ss * (1.0 / QK_MLA) + EPS))
            q_ref[0, h, :, 0:LANES] = (n * rinv[e] * gn_ref[...]).astype(BF16)
        rot = ((r * gr_ref[...]) * cos_ref[...] + (rs * grs_ref[...]) * sin_ref[...]) \
            * jnp.where(low, rinv[0], rinv[1])
        q_ref[0, 2 * j, :, LANES:QK_PAD] = (jnp.where(low, rot, 0.0) + aug_ref[...]).astype(BF16)
        q_ref[0, 2 * j + 1, :, LANES:QK_PAD] = (
            jnp.where(low, pltpu.roll(rot, ROPE, 1), 0.0) + aug_ref[...]).astype(BF16)


def _q_up(cqn, w, g_n, g_r, g_rs, aug, cos, sin, *, tm):
    B, S, _ = cqn.shape
    n_heads = w.shape[1] // (2 * LANES)
    row = lambda b, t: (b, t, 0)
    pos = lambda b, t: (t, 0)
    return pl.pallas_call(
        functools.partial(_q_up_kernel, n_heads=n_heads),
        grid=(B, S // tm),
        in_specs=[pl.BlockSpec((1, tm, Q_LORA), row), _resident(w),
                  _resident(g_n), _resident(g_r), _resident(g_rs), _resident(aug),
                  pl.BlockSpec((tm, LANES), pos), pl.BlockSpec((tm, LANES), pos)],
        out_specs=pl.BlockSpec((1, n_heads, tm, QK_PAD), lambda b, t: (b, 0, t, 0)),
        out_shape=jax.ShapeDtypeStruct((B, n_heads, S, QK_PAD), BF16),
        compiler_params=_params("arbitrary", "arbitrary"),
        name="q_up",
    )(cqn, w, g_n, g_r, g_rs, aug, cos, sin)


def _expand_kv(c, kpe, wuk, wuvt, g_n, g_r, k_aug, cos, sin, n_heads):
    cb = c.astype(BF16)
    kn = _dot(cb, wuk)
    vt = _dot_nt(wuvt, cb)
    a = kpe * g_r
    sspe = jnp.sum(kpe * kpe, axis=-1, keepdims=True)
    a_sw = pltpu.roll(a, HALF_ROPE, 1) + pltpu.roll(a, LANES - HALF_ROPE, 1)
    low = lax.broadcasted_iota(jnp.int32, (1, LANES), 1) < ROPE
    rot = jnp.where(low, a * cos + a_sw * sin, 0.0)
    keys = []
    for h in range(n_heads):
        n = kn[:, h * NOPE:(h + 1) * NOPE]
        ss = jnp.sum(n * n, axis=-1, keepdims=True) + sspe
        rinv = lax.rsqrt(ss * (1.0 / QK_MLA) + EPS)
        keys.append(jnp.concatenate([(n * rinv * g_n).astype(BF16),
                                     (rot * rinv + k_aug).astype(BF16)], axis=-1))
    return keys, vt


N_FLASH_IN = 3


def _mla_flash_kernel(*refs, tq, td, hg, bounded, n_riders):
    q_ref, k_ref, vt_ref = refs[:N_FLASH_IN]
    o_ref = refs[N_FLASH_IN + n_riders]
    m_sc, l_sc, acc_sc = refs[N_FLASH_IN + 1 + 2 * n_riders:]
    for src, dst in zip(refs[N_FLASH_IN:N_FLASH_IN + n_riders],
                        refs[N_FLASH_IN + 1 + n_riders:N_FLASH_IN + 1 + 2 * n_riders]):
        dst[...] = src[...].astype(BF16)
    i = pl.program_id(2)
    if not bounded:
        m_sc[...] = jnp.full(m_sc.shape, -jnp.inf, F32)
    l_sc[...] = jnp.zeros(l_sc.shape, F32)
    acc_sc[...] = jnp.zeros(acc_sc.shape, F32)

    def step(tiles):
        sts = [(hh, t, _dot_nt(k_ref[0, hh, pl.ds(t[0], t[1]), :], q_ref[0, hh, t[2]:, :]))
               for t in tiles for hh in range(hg)]
        for hh, (k0, kn, q0, mask), st in sts:
            if mask is not None:
                st = jnp.where(mask, st, NEG)
            vt = vt_ref[0, hh, :, pl.ds(k0, kn)]
            if bounded:
                p = jnp.exp2(st)
                l_sc[hh, :, q0:] += jnp.sum(p, axis=0, keepdims=True)
                acc_sc[hh, :, q0:] += _dot(vt, p.astype(BF16))
            else:
                m_prev = m_sc[hh, :, q0:]
                m_new = jnp.maximum(m_prev, jnp.max(st, axis=0, keepdims=True))
                alpha = jnp.exp2(m_prev - m_new)
                p = jnp.exp2(st - m_new)
                l_sc[hh, :, q0:] = alpha * l_sc[hh, :, q0:] + jnp.sum(p, axis=0, keepdims=True)
                acc_sc[hh, :, q0:] = acc_sc[hh, :, q0:] * alpha + _dot(vt, p.astype(BF16))
                m_sc[hh, :, q0:] = m_new

    def body(j, carry):
        step([(pl.multiple_of(j * tq, tq), tq, 0, None)])
        return carry

    lax.fori_loop(0, i, body, 0)
    diag = []
    for d in range(tq // td):
        nq = tq - d * td
        q_chunk = lax.broadcasted_iota(jnp.int32, (td, nq), 1) // CHUNK
        k_chunk = lax.broadcasted_iota(jnp.int32, (td, nq), 0) // CHUNK
        diag.append((pl.multiple_of(i * tq + d * td, td), td, d * td, k_chunk <= q_chunk))
    step(diag)
    for hh in range(hg):
        o_ref[0, :, hh * V_MLA:(hh + 1) * V_MLA] = (acc_sc[hh] / l_sc[hh]).T.astype(o_ref.dtype)


def _flash_steps(B, H, S, tq, hg):
    return B * (H // hg) * (S // tq)


def _can_ride(a, n_steps):
    return a.shape[0] % n_steps == 0 and (a.shape[0] // n_steps) % BF16_ROWS == 0


def _mla_flash(q, k, vt, *riders, tq, td, hg, bounded):
    B, H, S, _ = q.shape
    nh, nq = H // hg, S // tq
    step = lambda b, h, i: ((b * nh + h) * nq + i, 0)
    rider_spec = lambda a: pl.BlockSpec((a.shape[0] // (B * nh * nq), a.shape[1]), step)
    return pl.pallas_call(
        functools.partial(_mla_flash_kernel, tq=tq, td=td, hg=hg, bounded=bounded, n_riders=len(riders)),
        grid=(B, nh, nq),
        in_specs=[pl.BlockSpec((1, hg, tq, QK_PAD), lambda b, h, i: (b, h, i, 0)),
                  pl.BlockSpec((1, hg, S, QK_PAD), lambda b, h, i: (b, h, 0, 0)),
                  pl.BlockSpec((1, hg, V_MLA, S), lambda b, h, i: (b, h, 0, 0))]
        + [rider_spec(a) for a in riders],
        out_specs=[pl.BlockSpec((1, tq, hg * V_MLA), lambda b, h, i: (b, i, h))]
        + [rider_spec(a) for a in riders],
        out_shape=[jax.ShapeDtypeStruct((B, S, H * V_MLA), BF16)]
        + [jax.ShapeDtypeStruct(a.shape, BF16) for a in riders],
        scratch_shapes=[pltpu.VMEM((hg, 1, tq), F32), pltpu.VMEM((hg, 1, tq), F32),
                        pltpu.VMEM((hg, V_MLA, tq), F32)],
        compiler_params=_params("arbitrary", "arbitrary", "arbitrary"),
        name="mla_flash_bounded" if bounded else "mla_flash",
    )(q, k, vt, *riders)


def _toeplitz(u_row, rows, cols, stride=1):
    u = jnp.broadcast_to(u_row, (rows, u_row.shape[-1]))
    return pltpu.roll(u, 0, 1, stride=stride, stride_axis=0)[:, :cols]


def _softmax_pv(parts, bounded):
    m = None
    if not bounded:
        for s, _ in parts:
            mx = jnp.max(s, axis=-1, keepdims=True)
            m = mx if m is None else jnp.maximum(m, mx)
    l = None
    acc = None
    for s, v in parts:
        p = jnp.exp2(s if bounded else s - m)
        ps = jnp.sum(p, axis=-1, keepdims=True)
        pv = _dot(p.astype(BF16), v)
        l = ps if l is None else l + ps
        acc = pv if acc is None else acc + pv
    return acc / l


def _band_kernel(q_ref, kc_ref, kp_ref, vc_ref, vp_ref, u_ref, o_ref, bc_sc, bp_sc,
                 *, n_heads, splits, bounded):
    b, g = pl.program_id(0), pl.program_id(1)
    tg = bc_sc.shape[1]

    @pl.when((b == 0) & (g == 0))
    def _():
        qc = lax.broadcasted_iota(jnp.int32, (tg, tg), 0) // CHUNK
        kc = lax.broadcasted_iota(jnp.int32, (tg, tg), 1) // CHUNK
        for h in range(n_heads):
            bc_sc[h] = jnp.where(kc <= qc, _toeplitz(u_ref[h, 0:1, :], tg, tg), NEG)
            bp_sc[h] = jnp.where(kc >= qc, _toeplitz(u_ref[h, 1:2, :], tg, tg), NEG)

    no_prev = jnp.where(g == 0, NEG, 0.0)
    tr = tg // splits
    for h0 in range(0, n_heads, 2):
        scores = []
        for h in (h0, h0 + 1):
            sl = slice(h * D_BAND, (h + 1) * D_BAND)
            for a in range(splits):
                rows, old, new = slice(a * tr, (a + 1) * tr), slice(a * tr, tg), slice(0, (a + 1) * tr)
                q = q_ref[0, rows, sl]
                scores.append((rows, old, new, sl,
                               _dot_nt(q, kp_ref[0, old, sl]) + (bp_sc[h, rows, old] + no_prev),
                               _dot_nt(q, kc_ref[0, new, sl]) + bc_sc[h, rows, new]))
        for rows, old, new, sl, s_prev, s_cur in scores:
            parts = [(s_prev, vp_ref[0, old, sl]), (s_cur, vc_ref[0, new, sl])]
            o_ref[0, rows, sl] = _softmax_pv(parts, bounded).astype(o_ref.dtype)


def _band(qb, kb, vb, u_tab, *, bounded):
    B, S, hw = qb.shape
    H = hw // D_BAND
    tg = BAND_PAST
    cur = lambda b, g: (b, g, 0)
    prev = lambda b, g: (b, jnp.maximum(g - 1, 0), 0)
    blk = (1, tg, hw)
    return pl.pallas_call(
        functools.partial(_band_kernel, n_heads=H, splits=4, bounded=bounded),
        grid=(B, S // tg),
        in_specs=[pl.BlockSpec(blk, cur), pl.BlockSpec(blk, cur), pl.BlockSpec(blk, prev),
                  pl.BlockSpec(blk, cur), pl.BlockSpec(blk, prev),
                  _resident(u_tab)],
        out_specs=pl.BlockSpec(blk, cur),
        out_shape=jax.ShapeDtypeStruct((B, S, hw), BF16),
        scratch_shapes=[pltpu.VMEM((H, tg, tg), F32), pltpu.VMEM((H, tg, tg), F32)],
        compiler_params=_params("arbitrary", "arbitrary"),
        name="band_bounded" if bounded else "band",
    )(qb, kb, kb, vb, vb, u_tab)


def _mla_sample_kernel(q_ref, cc_ref, pc_ref, cn_ref, pn_ref, wuk_ref, wuvt_ref, gn_ref, gr_ref,
                       aug_ref, cosc_ref, sinc_ref, cosn_ref, sinn_ref, o_ref, *, n_heads, t_new):
    args = (wuk_ref[...], wuvt_ref[...], gn_ref[...], gr_ref[...], aug_ref[...])
    pc = jnp.concatenate([pc_ref[0], jnp.zeros((pc_ref.shape[1], LANES - ROPE), F32)], axis=-1)
    kc, vtc = _expand_kv(cc_ref[0], pc, *args, cosc_ref[...], sinc_ref[...], n_heads)
    kn, vtn = _expand_kv(cn_ref[0], pn_ref[0], *args, cosn_ref[...], sinn_ref[...], n_heads)
    rows_n = cn_ref.shape[1]
    new_ok = lax.broadcasted_iota(jnp.int32, (q_ref.shape[2], rows_n), 1) < t_new
    scores = [(_dot_nt(q_ref[0, h], kc[h]), jnp.where(new_ok, _dot_nt(q_ref[0, h], kn[h]), NEG))
              for h in range(n_heads)]
    for h, (s_c, s_n) in enumerate(scores):
        m = jnp.maximum(jnp.max(s_c, axis=-1, keepdims=True), jnp.max(s_n, axis=-1, keepdims=True))
        p_c = jnp.exp2(s_c - m)
        p_n = jnp.exp2(s_n - m)
        l = jnp.sum(p_c, axis=-1, keepdims=True) + jnp.sum(p_n, axis=-1, keepdims=True)
        sl = slice(h * V_MLA, (h + 1) * V_MLA)
        pv = (_dot_nt(p_c.astype(BF16), vtc[sl].astype(BF16))
              + _dot_nt(p_n.astype(BF16), vtn[sl].astype(BF16)))
        o_ref[0, :, sl] = (pv / l).astype(o_ref.dtype)


def _mla_sample(q, ckv_c, kpe_c, ckv_n, kpe_n, w_uk, w_uvt, g_n, g_r, k_aug, tabs, *, t_new):
    Bd = ckv_c.shape[0]
    H, T = q.shape[1], q.shape[2] // Bd
    P = ckv_c.shape[1]
    rn = ckv_n.shape[1]
    assert P % rn == 0
    old = lambda: pl.BlockSpec((P, LANES), lambda b: (0, 0), pipeline_mode=pl.Buffered(1))
    new = lambda: pl.BlockSpec((rn, LANES), lambda b: (P // rn, 0), pipeline_mode=pl.Buffered(1))
    bat = lambda b: (b, 0, 0)
    return pl.pallas_call(
        functools.partial(_mla_sample_kernel, n_heads=H, t_new=t_new),
        grid=(Bd,),
        in_specs=[pl.BlockSpec((1, H, T, QK_PAD), lambda b: (0, 0, b, 0)),
                  pl.BlockSpec((1, P, KV_LORA), bat), pl.BlockSpec((1, P, ROPE), bat),
                  pl.BlockSpec((1, rn, KV_LORA), bat), pl.BlockSpec((1, rn, LANES), bat),
                  _resident(w_uk), _resident(w_uvt), _resident(g_n), _resident(g_r), _resident(k_aug),
                  old(), old(), new(), new()],
        out_specs=pl.BlockSpec((1, T, H * V_MLA), bat),
        out_shape=jax.ShapeDtypeStruct((Bd, T, H * V_MLA), BF16),
        compiler_params=_params("arbitrary"),
        name="mla_sample",
    )(q, ckv_c, kpe_c, ckv_n, kpe_n, w_uk, w_uvt, g_n, g_r, k_aug, *tabs, *tabs)


def _band_sample_kernel(q_ref, kc_ref, vc_ref, kn_ref, vn_ref, wc_ref, wn_ref, o_ref, *, n_heads):
    T = q_ref.shape[1]
    heads = [slice(h * D_BAND, (h + 1) * D_BAND) for h in range(n_heads)]
    q = jnp.concatenate([q_ref[0, :, sl] for sl in heads], axis=0)
    tab = lambda w_ref, cols: jnp.concatenate(
        [_toeplitz(w_ref[h:h + 1, :], T, cols, stride=n_heads) for h in range(n_heads)], axis=0)
    s_c = _dot_nt(q, kc_ref[0].astype(BF16)) + tab(wc_ref, kc_ref.shape[1])
    s_n = _dot_nt(q, kn_ref[0]) + tab(wn_ref, kn_ref.shape[1])
    parts = [(s_c, vc_ref[0].astype(BF16)), (s_n, vn_ref[0])]
    o = _softmax_pv(parts, False).astype(o_ref.dtype)
    for h, sl in enumerate(heads):
        o_ref[0, :, sl] = o[h * T:(h + 1) * T]


def _band_sample(q, k_c, v_c, k_n, v_n, w_c, w_n, *, n_heads):
    Bd = q.shape[0]
    bat = lambda b: (b, 0, 0)
    blk = lambda a: pl.BlockSpec((1,) + a.shape[1:], bat)
    return pl.pallas_call(
        functools.partial(_band_sample_kernel, n_heads=n_heads),
        grid=(Bd,),
        in_specs=[blk(q), blk(k_c), blk(v_c), blk(k_n), blk(v_n), _resident(w_c), _resident(w_n)],
        out_specs=blk(q),
        out_shape=jax.ShapeDtypeStruct(q.shape, BF16),
        compiler_params=_params("arbitrary"),
        name="band_sample",
    )(q, k_c, v_c, k_n, v_n, w_c, w_n)


def _merge_kernel(x_ref, oa_ref, ob_ref, wo_ref, h_ref):
    half = oa_ref.shape[-1]
    h_ref[0] = x_ref[0] + _dot(oa_ref[0], wo_ref[:half, :]) + _dot(ob_ref[0], wo_ref[half:, :])


def _merge(x, oa, ob, w_o, *, tm):
    B, S, D = x.shape
    row = lambda b, t: (b, t, 0)
    return pl.pallas_call(
        _merge_kernel,
        grid=(B, S // tm),
        in_specs=[pl.BlockSpec((1, tm, D), row), pl.BlockSpec((1, tm, oa.shape[-1]), row),
                  pl.BlockSpec((1, tm, ob.shape[-1]), row),
                  _resident(w_o)],
        out_specs=pl.BlockSpec((1, tm, D), row),
        out_shape=jax.ShapeDtypeStruct((B, S, D), F32),
        compiler_params=_params("arbitrary", "arbitrary"),
        name="merge",
    )(x, oa, ob, w_o)


def _ffn_kernel(h_ref, g_ref, wup_ref, wdn_ref, y_ref, hn_sc):
    j = pl.program_id(2)

    @pl.when(j == 0)
    def _():
        h = h_ref[0]
        hn_sc[...] = _rms(h, g_ref[...]).astype(BF16)
        y_ref[0] = h

    u = jnp.maximum(_dot(hn_sc[...], wup_ref[...]), 0.0)
    y_ref[0] += _dot((u * u).astype(BF16), wdn_ref[...])


def _ffn(h, g, w_up, w_down, *, tm, tf):
    B, S, D = h.shape
    F = w_up.shape[1]
    row = lambda b, t, j: (b, t, 0)
    return pl.pallas_call(
        _ffn_kernel,
        grid=(B, S // tm, F // tf),
        in_specs=[pl.BlockSpec((1, tm, D), row), _resident(g),
                  pl.BlockSpec((D, tf), lambda b, t, j: (0, j)),
                  pl.BlockSpec((tf, D), lambda b, t, j: (j, 0))],
        out_specs=pl.BlockSpec((1, tm, D), row),
        out_shape=jax.ShapeDtypeStruct((B, S, D), F32),
        scratch_shapes=[pltpu.VMEM((tm, D), BF16)],
        compiler_params=_params("arbitrary", "arbitrary", "arbitrary"),
        name="ffn",
    )(h, g, w_up, w_down)


def _rope_tables(n_pos):
    inv = 1.0 / (ROPE_BASE ** (jnp.arange(0, ROPE, 2, dtype=F32) / ROPE))
    n_hi = -(-n_pos // CHUNK)
    hi = (jnp.arange(n_hi, dtype=F32) * CHUNK)[:, None, None] * inv
    lo = jnp.arange(CHUNK, dtype=F32)[None, :, None] * inv
    ch, sh, cl, sl = jnp.cos(hi), jnp.sin(hi), jnp.cos(lo), jnp.sin(lo)
    c = (ch * cl - sh * sl).reshape(n_hi * CHUNK, HALF_ROPE)[:n_pos]
    s = (sh * cl + ch * sl).reshape(n_hi * CHUNK, HALF_ROPE)[:n_pos]
    return jnp.concatenate([c, c, c, c], axis=1), jnp.concatenate([-s, s, -s, s], axis=1)


def _pad_lanes(a, width=LANES):
    return jnp.pad(a, [(0, 0)] * (a.ndim - 1) + [(0, width - a.shape[-1])])


def _pad_rows(a, rows):
    return jnp.pad(a, [(0, 0), (0, rows - a.shape[1]), (0, 0)])


def _swap_halves(a):
    return jnp.concatenate([a[..., HALF_ROPE:], a[..., :HALF_ROPE]], axis=-1)


def _prep_weights(w_in, w_uq, w_uk, w_uv):
    hb3 = (w_in.shape[1] - Q_LORA - KV_LORA - ROPE) // 3
    o = [0, Q_LORA, Q_LORA + KV_LORA, Q_LORA + KV_LORA + ROPE]
    o += [o[3] + hb3, o[3] + 2 * hb3, o[3] + 3 * hb3]
    names = ("cq", "ckv", "kpe", "qb", "kb", "vb")
    w = {n: w_in[:, o[i]:o[i + 1]].astype(BF16) for i, n in enumerate(names)}
    w["kpe"] = _pad_lanes(w["kpe"])
    ha = w_uq.shape[1] // QK_MLA
    uq = w_uq.astype(BF16).reshape(Q_LORA, ha, QK_MLA)
    nope = uq[:, :, :NOPE].reshape(Q_LORA, ha * NOPE)
    rope = uq[:, :, NOPE:]
    w["uq"] = jnp.concatenate(
        [nope, rope.reshape(Q_LORA, ha * ROPE), _swap_halves(rope).reshape(Q_LORA, ha * ROPE)], axis=1)
    w["uk"] = w_uk.astype(BF16)
    w["uvt"] = w_uv.astype(BF16).T
    return w


def _softmax_setup(g_qa, g_ka, g_qb, g_kb, rel_bias):
    c_a = QK_MLA ** -0.5 * LOG2E
    c_b = D_BAND ** -0.5 * LOG2E
    amax = lambda a: jnp.max(jnp.abs(a))
    bound_a = c_a * QK_MLA * amax(g_qa) * amax(g_ka) * BOUND_MARGIN
    bound_b = c_b * D_BAND * amax(g_qb) * amax(g_kb) * BOUND_MARGIN + LOG2E * amax(rel_bias)
    fast_a = bound_a <= FAST_LIMIT
    fast_b = bound_b <= FAST_LIMIT
    shift_a = jnp.where(fast_a, bound_a, 0.0)
    shift_b = jnp.where(fast_b, bound_b, 0.0)
    lane = jnp.arange(LANES)[None, :]
    twice = lambda a: jnp.concatenate([a, a], axis=-1)
    g = {"qa_n": g_qa[None, :NOPE] * c_a, "qa_r": twice(g_qa[None, NOPE:]) * c_a,
         "qa_rs": twice(_swap_halves(g_qa[None, NOPE:])) * c_a,
         "q_aug": (lane == AUG_LANE).astype(F32),
         "ka_n": g_ka[None, :NOPE], "ka_r": _pad_lanes(g_ka[None, NOPE:]),
         "k_aug": jnp.where(lane == AUG_LANE, -shift_a, 0.0).astype(F32),
         "qb": g_qb * c_b}
    return g, fast_a, fast_b, shift_b


def _take_static(a, idx):
    idx = np.asarray(idx)
    steps = np.diff(idx)
    pieces, i = [], 0
    while i < len(idx):
        step = int(steps[i]) if i < len(steps) and steps[i] in (-1, 1) else 0
        j = i + 1
        while j < len(idx) and idx[j] - idx[j - 1] == step:
            j += 1
        lo, hi = sorted((int(idx[i]), int(idx[j - 1])))
        run = a[:, lo:hi + 1]
        pieces.append(jnp.broadcast_to(run, (a.shape[0], j - i)) if step == 0 else run[:, ::step])
        i = j
    return jnp.concatenate(pieces, axis=1)


def _band_vectors(rel_bias, shift):
    t = BAND_PAST
    e = np.arange(2 * t)
    e = np.where(e < t, e, e - 2 * t)
    rows = [_take_static(rel_bias, np.clip(d, -MAX_REL, MAX_REL) + MAX_REL) for d in (-e, t - e)]
    return (jnp.stack(rows, axis=1) * LOG2E - shift).astype(F32)


def _band_sample_vectors(rel_bias, lb):
    H = rel_bias.shape[0]

    def interleave(dist, n_pos, n):
        e = np.arange(n)
        e = np.where(e < n_pos, e, e - n)
        u = _take_static(rel_bias, np.clip(dist - e, -MAX_REL, MAX_REL) + MAX_REL) * LOG2E
        own = np.arange(H)[:, None, None] == np.arange(H)[None, None, :]
        return jnp.where(own, u[:, :, None], NEG).reshape(H, n * H).astype(F32)

    return interleave(lb, lb, lb + LANES), interleave(0, CHUNK, 2 * CHUNK)


def _row_tile(n, pref):
    return pref if n % pref == 0 else n


def kernel(x_prompt, x_sample, cache_mla_ckv, cache_mla_kpe, cache_band_k, cache_band_v,
           norm_mix, w_in, g_cq, w_uq, g_ckv, w_uk, w_uv, g_qa, g_ka, g_qb, g_kb, rel_bias,
           w_o, norm_ffn, w_up, w_down):
    depth = w_in.shape[0]
    assert depth == 1, "single-layer step"
    B, S, D = x_prompt.shape
    Bd, T, _ = x_sample.shape
    P = cache_mla_ckv.shape[2]
    Lb = cache_band_k.shape[2]
    keep_p = min(BAND_PAST, S)
    assert S % BAND_PAST == 0 and T <= CHUNK

    w = _prep_weights(w_in[0], w_uq[0], w_uk[0], w_uv[0])
    g, fast_a, fast_b, shift_b = _softmax_setup(g_qa[0], g_ka[0], g_qb, g_kb, rel_bias[0])
    ha = w["uk"].shape[1] // NOPE
    hb = rel_bias.shape[1]
    hw = hb * D_BAND
    gains = (g_cq, g_ckv, g["qb"], g_kb)
    q_args = (w["uq"], g["qa_n"], g["qa_r"], g["qa_rs"], g["q_aug"])
    kv_args = (w["uk"], w["uvt"], g["ka_n"], g["ka_r"], g["k_aug"])

    rn = LANES
    tabs = _rope_tables(max(S, P + rn))
    cqn, ckv, kpe_pad, kpe, qb, kb, vb, kb_tail, vb_tail, k, vt = _proj(
        x_prompt, norm_mix, w, *gains, keep=keep_p, tm=_row_tile(S, ROWS_PROJ), kv=(*kv_args, *tabs))
    q = _q_up(cqn, *q_args, *tabs, tm=_row_tile(S, ROWS_Q_UP))
    tqa = _row_tile(S, ROWS_FLASH)
    flash = lambda bounded: functools.partial(_mla_flash, tq=tqa, td=min(tqa, ROWS_FLASH_DIAG), hg=2,
                                              bounded=bounded)
    late = (w_o[0], w_up[0], w_down[0])
    ride = [_can_ride(a, _flash_steps(B, ha, S, tqa, 2)) for a in late]
    oa, *cast = lax.cond(fast_a, flash(True), flash(False), q, k, vt, *(a for a, r in zip(late, ride) if r))
    cast = iter(cast)
    w["o"], w["up"], w["down"] = (next(cast) if r else a.astype(BF16) for a, r in zip(late, ride))
    u_tab = _band_vectors(rel_bias[0], shift_b)
    ob = lax.cond(fast_b, functools.partial(_band, bounded=True),
                  functools.partial(_band, bounded=False), qb, kb, vb, u_tab)
    h = _merge(x_prompt, oa, ob, w["o"], tm=_row_tile(S, ROWS_MERGE))
    y_prompt = _ffn(h, norm_ffn, w["up"], w["down"], tm=_row_tile(S, ROWS_FFN), tf=COLS_FFN)

    n_s = Bd * T
    xs = x_sample.reshape(1, n_s, D)
    cqn_s, ckv_s, kpe_pad_s, kpe_s, qb_s, kb_s, vb_s, kb_s32, vb_s32 = _proj(
        xs, norm_mix, w, *gains, keep=n_s, tm=_row_tile(n_s, ROWS_PROJ_SAMPLE))
    tabs_s = tuple(jnp.tile(t[P:P + T], (Bd, 1)) for t in tabs)
    q_s = _q_up(cqn_s, *q_args, *tabs_s, tm=n_s)
    oa_s = _mla_sample(
        q_s, cache_mla_ckv[0], cache_mla_kpe[0],
        _pad_rows(ckv_s.reshape(Bd, T, KV_LORA), rn), _pad_rows(kpe_pad_s.reshape(Bd, T, LANES), rn),
        *kv_args, tabs, t_new=T)
    w_c, w_n = _band_sample_vectors(rel_bias[0], Lb)
    ob_s = _band_sample(
        qb_s.reshape(Bd, T, hw), cache_band_k.reshape(Bd, Lb * hb, D_BAND), cache_band_v.reshape(Bd, Lb * hb, D_BAND),
        kb_s.reshape(Bd, T * hb, D_BAND), vb_s.reshape(Bd, T * hb, D_BAND), w_c, w_n, n_heads=hb)
    h_s = _merge(xs, oa_s.reshape(1, n_s, ha * V_MLA), ob_s.reshape(1, n_s, hw), w["o"], tm=n_s)
    y_sample = _ffn(h_s, norm_ffn, w["up"], w["down"], tm=n_s, tf=COLS_FFN).reshape(Bd, T, D)

    return (y_prompt, y_sample,
            ckv[None], kpe[None],
            kb_tail.reshape(B, keep_p, hb, D_BAND)[None], vb_tail.reshape(B, keep_p, hb, D_BAND)[None],
            ckv_s.reshape(Bd, T, KV_LORA)[None], kpe_s.reshape(Bd, T, ROPE)[None],
            kb_s32.reshape(Bd, T, hb, D_BAND)[None], vb_s32.reshape(Bd, T, hb, D_BAND)[None])
```

```python
import functools

import jax
import jax.numpy as jnp
import numpy as np
from jax import lax
from jax.experimental import pallas as pl
from jax.experimental.pallas import tpu as pltpu

CHUNK = 64
EPS = 1e-6
NOPE = 128
ROPE = 64
HALF_ROPE = ROPE // 2
QK_MLA = NOPE + ROPE
V_MLA = 128
D_BAND = 128
BAND_CHUNKS = 8
BAND_PAST = BAND_CHUNKS * CHUNK
MAX_REL = 128
ROPE_BASE = 10000.0
Q_LORA = 512
KV_LORA = 256
NEG = -1e30
LOG2E = 1.4426950408889634

BOUND_MARGIN = 1.02
FAST_LIMIT = 60.0

LANES = 128
BF16_ROWS = 16
QK_PAD = 2 * LANES
AUG_LANE = ROPE
VMEM_LIMIT = 56 * 1024 * 1024
ROWS_W_IN = 256
ROWS_PROJ = 512
ROWS_PROJ_SAMPLE = 256
ROWS_Q_UP = 512
ROWS_FLASH = 1024
ROWS_FLASH_DIAG = 256
ROWS_MERGE = 512
ROWS_FFN = 512
COLS_FFN = 2048

BF16 = jnp.bfloat16
F32 = jnp.float32

NT_DIMS = (((1,), (1,)), ((), ()))


def _params(*sem):
    return pltpu.CompilerParams(dimension_semantics=sem, vmem_limit_bytes=VMEM_LIMIT)


def _resident(a):
    return pl.BlockSpec(a.shape, lambda *_: (0,) * a.ndim, pipeline_mode=pl.Buffered(1))


def _rms(x, g):
    ms = jnp.mean(x * x, axis=-1, keepdims=True)
    return x * lax.rsqrt(ms + EPS) * g


def _dot(a, b):
    return jnp.dot(a, b, preferred_element_type=F32)


def _dot_nt(a, b):
    return lax.dot_general(a, b, NT_DIMS, preferred_element_type=F32)


N_PROJ_IN, N_PROJ_OUT, N_KV_IN = 12, 9, 7


def _proj_kernel(*refs, n_heads, n_mla_heads, tail_start):
    (x_ref, nm_ref, wcq_ref, wckv_ref, wkpe_ref, wqb_ref, wkb_ref, wvb_ref,
     gcq_ref, gckv_ref, gqb_ref, gkb_ref) = refs[:N_PROJ_IN]
    n_kv = N_KV_IN if n_mla_heads else 0
    outs = refs[N_PROJ_IN + n_kv:]
    cqn_ref, ckv_ref, kpe_ref, kpe_out_ref, qb_ref, kb_ref, vb_ref, kbt_ref, vbt_ref = outs[:N_PROJ_OUT]
    t = pl.program_id(1)
    xn = _rms(x_ref[0], nm_ref[...]).astype(BF16)
    cqn_ref[0] = _rms(_dot(xn, wcq_ref[...]), gcq_ref[...]).astype(BF16)
    ckv = _rms(_dot(xn, wckv_ref[...]), gckv_ref[...])
    ckv_ref[0] = ckv
    kpe = _dot(xn, wkpe_ref[...])
    kpe_ref[0] = kpe
    kpe_out_ref[0] = kpe.T[:ROPE] if n_mla_heads else kpe[:, :ROPE]
    if n_mla_heads:
        wuk_ref, wuvt_ref, gn_ref, gr_ref, aug_ref, cos_ref, sin_ref = refs[N_PROJ_IN:N_PROJ_IN + n_kv]
        k_ref, vt_ref = outs[N_PROJ_OUT:]
        keys, vt = _expand_kv(ckv, kpe, wuk_ref[...], wuvt_ref[...], gn_ref[...], gr_ref[...],
                              aug_ref[...], cos_ref[...], sin_ref[...], n_mla_heads)
        for h in range(n_mla_heads):
            k_ref[0, h] = keys[h]
            vt_ref[0, h] = vt[h * V_MLA:(h + 1) * V_MLA, :].astype(BF16)
    qb = _dot(xn, wqb_ref[...])
    kb = _dot(xn, wkb_ref[...])
    vb = _dot(xn, wvb_ref[...])
    vb_ref[0] = vb.astype(BF16)
    kbn = []
    for h in range(n_heads):
        sl = slice(h * D_BAND, (h + 1) * D_BAND)
        qb_ref[0, :, sl] = _rms(qb[:, sl], gqb_ref[...]).astype(BF16)
        kbn.append(_rms(kb[:, sl], gkb_ref[...]))
    kbn = jnp.concatenate(kbn, axis=-1)
    kb_ref[0] = kbn.astype(BF16)

    @pl.when(t >= tail_start)
    def _():
        kbt_ref[0] = kbn
        vbt_ref[0] = vb


def _proj(x, norm_mix, w, g_cq, g_ckv, g_qb, g_kb, *, keep, tm, kv=None):
    B, S, D = x.shape
    n_heads = w["qb"].shape[1] // D_BAND
    hw = n_heads * D_BAND
    nt = S // tm
    tail_start = nt - keep // tm
    row = lambda b, t: (b, t, 0)
    tail = lambda b, t: (b, jnp.maximum(t - tail_start, 0), 0)
    if kv is not None:
        kpe_shape, kpe_spec = (B, ROPE, S), pl.BlockSpec((1, ROPE, tm), lambda b, t: (b, 0, t))
    else:
        kpe_shape, kpe_spec = (B, S, ROPE), pl.BlockSpec((1, tm, ROPE), row)
    out_shapes = (
        jax.ShapeDtypeStruct((B, S, Q_LORA), BF16),
        jax.ShapeDtypeStruct((B, S, KV_LORA), F32),
        jax.ShapeDtypeStruct((B, S, LANES), F32),
        jax.ShapeDtypeStruct(kpe_shape, F32),
        jax.ShapeDtypeStruct((B, S, hw), BF16),
        jax.ShapeDtypeStruct((B, S, hw), BF16),
        jax.ShapeDtypeStruct((B, S, hw), BF16),
        jax.ShapeDtypeStruct((B, keep, hw), F32),
        jax.ShapeDtypeStruct((B, keep, hw), F32),
    )
    out_specs = (
        pl.BlockSpec((1, tm, Q_LORA), row),
        pl.BlockSpec((1, tm, KV_LORA), row),
        pl.BlockSpec((1, tm, LANES), row),
        kpe_spec,
        pl.BlockSpec((1, tm, hw), row),
        pl.BlockSpec((1, tm, hw), row),
        pl.BlockSpec((1, tm, hw), row),
        pl.BlockSpec((1, tm, hw), tail),
        pl.BlockSpec((1, tm, hw), tail),
    )
    weights = (w["cq"], w["ckv"], w["kpe"], w["qb"], w["kb"], w["vb"])
    gains = (g_cq, g_ckv, g_qb, g_kb)
    kv_specs, kv_args, ha = [], (), 0
    if kv is not None:
        ha = kv[0].shape[1] // NOPE
        pos = lambda b, t: (t, 0)
        kv_specs = [_resident(a) for a in kv[:5]] + [pl.BlockSpec((tm, LANES), pos)] * 2
        kv_args = tuple(kv)
        out_shapes += (jax.ShapeDtypeStruct((B, ha, S, QK_PAD), BF16),
                       jax.ShapeDtypeStruct((B, ha, V_MLA, S), BF16))
        out_specs += (pl.BlockSpec((1, ha, tm, QK_PAD), lambda b, t: (b, 0, t, 0)),
                      pl.BlockSpec((1, ha, V_MLA, tm), lambda b, t: (b, 0, 0, t)))
    return pl.pallas_call(
        functools.partial(_proj_kernel, n_heads=n_heads, n_mla_heads=ha, tail_start=tail_start),
        grid=(B, nt),
        in_specs=[pl.BlockSpec((1, tm, D), row), _resident(norm_mix)]
        + [_resident(a) for a in weights] + [_resident(a) for a in gains] + kv_specs,
        out_specs=out_specs,
        out_shape=out_shapes,
        compiler_params=_params("arbitrary", "arbitrary"),
        name="proj",
    )(x, norm_mix, *weights, *gains, *kv_args)


def _q_up_kernel(cqn_ref, w_ref, gn_ref, gr_ref, grs_ref, aug_ref, cos_ref, sin_ref, q_ref,
                 *, n_heads):
    hw = n_heads * LANES
    hp = hw // 2
    y = _dot(cqn_ref[0], w_ref[...])
    low = lax.broadcasted_iota(jnp.int32, (1, LANES), 1) < ROPE
    for j in range(n_heads // 2):
        r = y[:, hw + j * LANES: hw + (j + 1) * LANES]
        rs = y[:, hw + hp + j * LANES: hw + hp + (j + 1) * LANES]
        r2 = r * r
        r2_of = (jnp.where(low, r2, 0.0), jnp.where(low, 0.0, r2))
        rinv = []
        for e in range(2):
            h = 2 * j + e
            n = y[:, h * LANES:(h + 1) * LANES]
            ss = jnp.sum(n * n + r2_of[e], axis=-1, keepdims=True)
            rinv.append(lax.rsqrt(ss * (1.0 / QK_MLA) + EPS))
            q_ref[0, h, :, 0:LANES] = (n * rinv[e] * gn_ref[...]).astype(BF16)
        rot = ((r * gr_ref[...]) * cos_ref[...] + (rs * grs_ref[...]) * sin_ref[...]) \
            * jnp.where(low, rinv[0], rinv[1])
        q_ref[0, 2 * j, :, LANES:QK_PAD] = (jnp.where(low, rot, 0.0) + aug_ref[...]).astype(BF16)
        q_ref[0, 2 * j + 1, :, LANES:QK_PAD] = (
            jnp.where(low, pltpu.roll(rot, ROPE, 1), 0.0) + aug_ref[...]).astype(BF16)


def _q_up(cqn, w, g_n, g_r, g_rs, aug, cos, sin, *, tm):
    B, S, _ = cqn.shape
    n_heads = w.shape[1] // (2 * LANES)
    row = lambda b, t: (b, t, 0)
    pos = lambda b, t: (t, 0)
    return pl.pallas_call(
        functools.partial(_q_up_kernel, n_heads=n_heads),
        grid=(B, S // tm),
        in_specs=[pl.BlockSpec((1, tm, Q_LORA), row), _resident(w),
                  _resident(g_n), _resident(g_r), _resident(g_rs), _resident(aug),
                  pl.BlockSpec((tm, LANES), pos), pl.BlockSpec((tm, LANES), pos)],
        out_specs=pl.BlockSpec((1, n_heads, tm, QK_PAD), lambda b, t: (b, 0, t, 0)),
        out_shape=jax.ShapeDtypeStruct((B, n_heads, S, QK_PAD), BF16),
        compiler_params=_params("arbitrary", "arbitrary"),
        name="q_up",
    )(cqn, w, g_n, g_r, g_rs, aug, cos, sin)


def _expand_kv(c, kpe, wuk, wuvt, g_n, g_r, k_aug, cos, sin, n_heads):
    cb = c.astype(BF16)
    kn = _dot(cb, wuk)
    vt = _dot_nt(wuvt, cb)
    a = kpe * g_r
    sspe = jnp.sum(kpe * kpe, axis=-1, keepdims=True)
    a_sw = pltpu.roll(a, HALF_ROPE, 1) + pltpu.roll(a, LANES - HALF_ROPE, 1)
    low = lax.broadcasted_iota(jnp.int32, (1, LANES), 1) < ROPE
    rot = jnp.where(low, a * cos + a_sw * sin, 0.0)
    keys = []
    for h in range(n_heads):
        n = kn[:, h * NOPE:(h + 1) * NOPE]
        ss = jnp.sum(n * n, axis=-1, keepdims=True) + sspe
        rinv = lax.rsqrt(ss * (1.0 / QK_MLA) + EPS)
        keys.append(jnp.concatenate([(n * rinv * g_n).astype(BF16),
                                     (rot * rinv + k_aug).astype(BF16)], axis=-1))
    return keys, vt


N_FLASH_IN = 3


def _mla_flash_kernel(*refs, tq, td, hg, bounded, n_riders):
    q_ref, k_ref, vt_ref = refs[:N_FLASH_IN]
    o_ref = refs[N_FLASH_IN + n_riders]
    m_sc, l_sc, acc_sc = refs[N_FLASH_IN + 1 + 2 * n_riders:]
    for src, dst in zip(refs[N_FLASH_IN:N_FLASH_IN + n_riders],
                        refs[N_FLASH_IN + 1 + n_riders:N_FLASH_IN + 1 + 2 * n_riders]):
        dst[...] = src[...].astype(BF16)
    i = pl.program_id(2)
    if not bounded:
        m_sc[...] = jnp.full(m_sc.shape, -jnp.inf, F32)
    l_sc[...] = jnp.zeros(l_sc.shape, F32)
    acc_sc[...] = jnp.zeros(acc_sc.shape, F32)

    def step(tiles):
        sts = [(hh, t, _dot_nt(k_ref[0, hh, pl.ds(t[0], t[1]), :], q_ref[0, hh, t[2]:, :]))
               for t in tiles for hh in range(hg)]
        for hh, (k0, kn, q0, mask), st in sts:
            if mask is not None:
                st = jnp.where(mask, st, NEG)
            vt = vt_ref[0, hh, :, pl.ds(k0, kn)]
            if bounded:
                p = jnp.exp2(st)
                l_sc[hh, :, q0:] += jnp.sum(p, axis=0, keepdims=True)
                acc_sc[hh, :, q0:] += _dot(vt, p.astype(BF16))
            else:
                m_prev = m_sc[hh, :, q0:]
                m_new = jnp.maximum(m_prev, jnp.max(st, axis=0, keepdims=True))
                alpha = jnp.exp2(m_prev - m_new)
                p = jnp.exp2(st - m_new)
                l_sc[hh, :, q0:] = alpha * l_sc[hh, :, q0:] + jnp.sum(p, axis=0, keepdims=True)
                acc_sc[hh, :, q0:] = acc_sc[hh, :, q0:] * alpha + _dot(vt, p.astype(BF16))
                m_sc[hh, :, q0:] = m_new

    def body(j, carry):
        step([(pl.multiple_of(j * tq, tq), tq, 0, None)])
        return carry

    lax.fori_loop(0, i, body, 0)
    diag = []
    for d in range(tq // td):
        nq = tq - d * td
        q_chunk = lax.broadcasted_iota(jnp.int32, (td, nq), 1) // CHUNK
        k_chunk = lax.broadcasted_iota(jnp.int32, (td, nq), 0) // CHUNK
        diag.append((pl.multiple_of(i * tq + d * td, td), td, d * td, k_chunk <= q_chunk))
    step(diag)
    for hh in range(hg):
        o_ref[0, :, hh * V_MLA:(hh + 1) * V_MLA] = (acc_sc[hh] / l_sc[hh]).T.astype(o_ref.dtype)


def _flash_steps(B, H, S, tq, hg):
    return B * (H // hg) * (S // tq)


def _can_ride(a, n_steps):
    return a.shape[0] % n_steps == 0 and (a.shape[0] // n_steps) % BF16_ROWS == 0


def _mla_flash(q, k, vt, *riders, tq, td, hg, bounded):
    B, H, S, _ = q.shape
    nh, nq = H // hg, S // tq
    step = lambda b, h, i: ((b * nh + h) * nq + i, 0)
    rider_spec = lambda a: pl.BlockSpec((a.shape[0] // (B * nh * nq), a.shape[1]), step)
    return pl.pallas_call(
        functools.partial(_mla_flash_kernel, tq=tq, td=td, hg=hg, bounded=bounded, n_riders=len(riders)),
        grid=(B, nh, nq),
        in_specs=[pl.BlockSpec((1, hg, tq, QK_PAD), lambda b, h, i: (b, h, i, 0)),
                  pl.BlockSpec((1, hg, S, QK_PAD), lambda b, h, i: (b, h, 0, 0)),
                  pl.BlockSpec((1, hg, V_MLA, S), lambda b, h, i: (b, h, 0, 0))]
        + [rider_spec(a) for a in riders],
        out_specs=[pl.BlockSpec((1, tq, hg * V_MLA), lambda b, h, i: (b, i, h))]
        + [rider_spec(a) for a in riders],
        out_shape=[jax.ShapeDtypeStruct((B, S, H * V_MLA), BF16)]
        + [jax.ShapeDtypeStruct(a.shape, BF16) for a in riders],
        scratch_shapes=[pltpu.VMEM((hg, 1, tq), F32), pltpu.VMEM((hg, 1, tq), F32),
                        pltpu.VMEM((hg, V_MLA, tq), F32)],
        compiler_params=_params("arbitrary", "arbitrary", "arbitrary"),
        name="mla_flash_bounded" if bounded else "mla_flash",
    )(q, k, vt, *riders)


def _toeplitz(u_row, rows, cols, stride=1):
    u = jnp.broadcast_to(u_row, (rows, u_row.shape[-1]))
    return pltpu.roll(u, 0, 1, stride=stride, stride_axis=0)[:, :cols]


def _softmax_pv(parts, bounded):
    m = None
    if not bounded:
        for s, _ in parts:
            mx = jnp.max(s, axis=-1, keepdims=True)
            m = mx if m is None else jnp.maximum(m, mx)
    l = None
    acc = None
    for s, v in parts:
        p = jnp.exp2(s if bounded else s - m)
        ps = jnp.sum(p, axis=-1, keepdims=True)
        pv = _dot(p.astype(BF16), v)
        l = ps if l is None else l + ps
        acc = pv if acc is None else acc + pv
    return acc / l


def _band_kernel(q_ref, kc_ref, kp_ref, vc_ref, vp_ref, u_ref, o_ref, bc_sc, bp_sc,
                 *, n_heads, splits, bounded):
    b, g = pl.program_id(0), pl.program_id(1)
    tg = bc_sc.shape[1]

    @pl.when((b == 0) & (g == 0))
    def _():
        qc = lax.broadcasted_iota(jnp.int32, (tg, tg), 0) // CHUNK
        kc = lax.broadcasted_iota(jnp.int32, (tg, tg), 1) // CHUNK
        for h in range(n_heads):
            bc_sc[h] = jnp.where(kc <= qc, _toeplitz(u_ref[h, 0:1, :], tg, tg), NEG)
            bp_sc[h] = jnp.where(kc >= qc, _toeplitz(u_ref[h, 1:2, :], tg, tg), NEG)

    no_prev = jnp.where(g == 0, NEG, 0.0)
    tr = tg // splits
    for h0 in range(0, n_heads, 2):
        scores = []
        for h in (h0, h0 + 1):
            sl = slice(h * D_BAND, (h + 1) * D_BAND)
            for a in range(splits):
                rows, old, new = slice(a * tr, (a + 1) * tr), slice(a * tr, tg), slice(0, (a + 1) * tr)
                q = q_ref[0, rows, sl]
                scores.append((rows, old, new, sl,
                               _dot_nt(q, kp_ref[0, old, sl]) + (bp_sc[h, rows, old] + no_prev),
                               _dot_nt(q, kc_ref[0, new, sl]) + bc_sc[h, rows, new]))
        for rows, old, new, sl, s_prev, s_cur in scores:
            parts = [(s_prev, vp_ref[0, old, sl]), (s_cur, vc_ref[0, new, sl])]
            o_ref[0, rows, sl] = _softmax_pv(parts, bounded).astype(o_ref.dtype)


def _band(qb, kb, vb, u_tab, *, bounded):
    B, S, hw = qb.shape
    H = hw // D_BAND
    tg = BAND_PAST
    cur = lambda b, g: (b, g, 0)
    prev = lambda b, g: (b, jnp.maximum(g - 1, 0), 0)
    blk = (1, tg, hw)
    return pl.pallas_call(
        functools.partial(_band_kernel, n_heads=H, splits=4, bounded=bounded),
        grid=(B, S // tg),
        in_specs=[pl.BlockSpec(blk, cur), pl.BlockSpec(blk, cur), pl.BlockSpec(blk, prev),
                  pl.BlockSpec(blk, cur), pl.BlockSpec(blk, prev),
                  _resident(u_tab)],
        out_specs=pl.BlockSpec(blk, cur),
        out_shape=jax.ShapeDtypeStruct((B, S, hw), BF16),
        scratch_shapes=[pltpu.VMEM((H, tg, tg), F32), pltpu.VMEM((H, tg, tg), F32)],
        compiler_params=_params("arbitrary", "arbitrary"),
        name="band_bounded" if bounded else "band",
    )(qb, kb, kb, vb, vb, u_tab)


def _mla_sample_kernel(q_ref, cc_ref, pc_ref, cn_ref, pn_ref, wuk_ref, wuvt_ref, gn_ref, gr_ref,
                       aug_ref, cosc_ref, sinc_ref, cosn_ref, sinn_ref, o_ref, *, n_heads, t_new):
    args = (wuk_ref[...], wuvt_ref[...], gn_ref[...], gr_ref[...], aug_ref[...])
    pc = jnp.concatenate([pc_ref[0], jnp.zeros((LANES - ROPE, pc_ref.shape[2]), F32)], axis=0).T
    kc, vtc = _expand_kv(cc_ref[0], pc, *args, cosc_ref[...], sinc_ref[...], n_heads)
    kn, vtn = _expand_kv(cn_ref[0], pn_ref[0], *args, cosn_ref[...], sinn_ref[...], n_heads)
    rows_n = cn_ref.shape[1]
    new_ok = lax.broadcasted_iota(jnp.int32, (q_ref.shape[2], rows_n), 1) < t_new
    scores = [(_dot_nt(q_ref[0, h], kc[h]), jnp.where(new_ok, _dot_nt(q_ref[0, h], kn[h]), NEG))
              for h in range(n_heads)]
    for h, (s_c, s_n) in enumerate(scores):
        m = jnp.maximum(jnp.max(s_c, axis=-1, keepdims=True), jnp.max(s_n, axis=-1, keepdims=True))
        p_c = jnp.exp2(s_c - m)
        p_n = jnp.exp2(s_n - m)
        l = jnp.sum(p_c, axis=-1, keepdims=True) + jnp.sum(p_n, axis=-1, keepdims=True)
        sl = slice(h * V_MLA, (h + 1) * V_MLA)
        pv = (_dot_nt(p_c.astype(BF16), vtc[sl].astype(BF16))
              + _dot_nt(p_n.astype(BF16), vtn[sl].astype(BF16)))
        o_ref[0, :, sl] = (pv / l).astype(o_ref.dtype)


def _mla_sample(q, ckv_c, kpe_c, ckv_n, kpe_n, w_uk, w_uvt, g_n, g_r, k_aug, tabs, *, t_new):
    Bd = ckv_c.shape[0]
    H, T = q.shape[1], q.shape[2] // Bd
    P = ckv_c.shape[1]
    rn = ckv_n.shape[1]
    assert P % rn == 0
    old = lambda: pl.BlockSpec((P, LANES), lambda b: (0, 0), pipeline_mode=pl.Buffered(1))
    new = lambda: pl.BlockSpec((rn, LANES), lambda b: (P // rn, 0), pipeline_mode=pl.Buffered(1))
    bat = lambda b: (b, 0, 0)
    return pl.pallas_call(
        functools.partial(_mla_sample_kernel, n_heads=H, t_new=t_new),
        grid=(Bd,),
        in_specs=[pl.BlockSpec((1, H, T, QK_PAD), lambda b: (0, 0, b, 0)),
                  pl.BlockSpec((1, P, KV_LORA), bat), pl.BlockSpec((1, ROPE, P), bat),
                  pl.BlockSpec((1, rn, KV_LORA), bat), pl.BlockSpec((1, rn, LANES), bat),
                  _resident(w_uk), _resident(w_uvt), _resident(g_n), _resident(g_r), _resident(k_aug),
                  old(), old(), new(), new()],
        out_specs=pl.BlockSpec((1, T, H * V_MLA), bat),
        out_shape=jax.ShapeDtypeStruct((Bd, T, H * V_MLA), BF16),
        compiler_params=_params("arbitrary"),
        name="mla_sample",
    )(q, ckv_c, kpe_c, ckv_n, kpe_n, w_uk, w_uvt, g_n, g_r, k_aug, *tabs, *tabs)


def _band_sample_kernel(q_ref, kc_ref, vc_ref, kn_ref, vn_ref, wc_ref, wn_ref, o_ref, *, n_heads):
    T = q_ref.shape[1]
    heads = [slice(h * D_BAND, (h + 1) * D_BAND) for h in range(n_heads)]
    q = jnp.concatenate([q_ref[0, :, sl] for sl in heads], axis=0)
    tab = lambda w_ref, cols: jnp.concatenate(
        [_toeplitz(w_ref[h:h + 1, :], T, cols, stride=n_heads) for h in range(n_heads)], axis=0)
    s_c = _dot_nt(q, kc_ref[0].astype(BF16)) + tab(wc_ref, kc_ref.shape[1])
    s_n = _dot_nt(q, kn_ref[0]) + tab(wn_ref, kn_ref.shape[1])
    parts = [(s_c, vc_ref[0].astype(BF16)), (s_n, vn_ref[0])]
    o = _softmax_pv(parts, False).astype(o_ref.dtype)
    for h, sl in enumerate(heads):
        o_ref[0, :, sl] = o[h * T:(h + 1) * T]


def _band_sample(q, k_c, v_c, k_n, v_n, w_c, w_n, *, n_heads):
    Bd = q.shape[0]
    bat = lambda b: (b, 0, 0)
    blk = lambda a: pl.BlockSpec((1,) + a.shape[1:], bat)
    return pl.pallas_call(
        functools.partial(_band_sample_kernel, n_heads=n_heads),
        grid=(Bd,),
        in_specs=[blk(q), blk(k_c), blk(v_c), blk(k_n), blk(v_n), _resident(w_c), _resident(w_n)],
        out_specs=blk(q),
        out_shape=jax.ShapeDtypeStruct(q.shape, BF16),
        compiler_params=_params("arbitrary"),
        name="band_sample",
    )(q, k_c, v_c, k_n, v_n, w_c, w_n)


def _merge_kernel(x_ref, oa_ref, ob_ref, wo_ref, h_ref):
    half = oa_ref.shape[-1]
    h_ref[0] = x_ref[0] + _dot(oa_ref[0], wo_ref[:half, :]) + _dot(ob_ref[0], wo_ref[half:, :])


def _merge(x, oa, ob, w_o, *, tm):
    B, S, D = x.shape
    row = lambda b, t: (b, t, 0)
    return pl.pallas_call(
        _merge_kernel,
        grid=(B, S // tm),
        in_specs=[pl.BlockSpec((1, tm, D), row), pl.BlockSpec((1, tm, oa.shape[-1]), row),
                  pl.BlockSpec((1, tm, ob.shape[-1]), row),
                  _resident(w_o)],
        out_specs=pl.BlockSpec((1, tm, D), row),
        out_shape=jax.ShapeDtypeStruct((B, S, D), F32),
        compiler_params=_params("arbitrary", "arbitrary"),
        name="merge",
    )(x, oa, ob, w_o)


def _ffn_kernel(h_ref, g_ref, wup_ref, wdn_ref, y_ref, hn_sc):
    j = pl.program_id(2)

    @pl.when(j == 0)
    def _():
        h = h_ref[0]
        hn_sc[...] = _rms(h, g_ref[...]).astype(BF16)
        y_ref[0] = h

    u = jnp.maximum(_dot(hn_sc[...], wup_ref[...]), 0.0)
    y_ref[0] += _dot((u * u).astype(BF16), wdn_ref[...])


def _ffn(h, g, w_up, w_down, *, tm, tf):
    B, S, D = h.shape
    F = w_up.shape[1]
    row = lambda b, t, j: (b, t, 0)
    return pl.pallas_call(
        _ffn_kernel,
        grid=(B, S // tm, F // tf),
        in_specs=[pl.BlockSpec((1, tm, D), row), _resident(g),
                  pl.BlockSpec((D, tf), lambda b, t, j: (0, j)),
                  pl.BlockSpec((tf, D), lambda b, t, j: (j, 0))],
        out_specs=pl.BlockSpec((1, tm, D), row),
        out_shape=jax.ShapeDtypeStruct((B, S, D), F32),
        scratch_shapes=[pltpu.VMEM((tm, D), BF16)],
        compiler_params=_params("arbitrary", "arbitrary", "arbitrary"),
        name="ffn",
    )(h, g, w_up, w_down)


def _rope_tables(n_pos):
    inv = 1.0 / (ROPE_BASE ** (jnp.arange(0, ROPE, 2, dtype=F32) / ROPE))
    n_hi = -(-n_pos // CHUNK)
    hi = (jnp.arange(n_hi, dtype=F32) * CHUNK)[:, None, None] * inv
    lo = jnp.arange(CHUNK, dtype=F32)[None, :, None] * inv
    ch, sh, cl, sl = jnp.cos(hi), jnp.sin(hi), jnp.cos(lo), jnp.sin(lo)
    c = (ch * cl - sh * sl).reshape(n_hi * CHUNK, HALF_ROPE)[:n_pos]
    s = (sh * cl + ch * sl).reshape(n_hi * CHUNK, HALF_ROPE)[:n_pos]
    return jnp.concatenate([c, c, c, c], axis=1), jnp.concatenate([-s, s, -s, s], axis=1)


def _pad_lanes(a, width=LANES):
    return jnp.pad(a, [(0, 0)] * (a.ndim - 1) + [(0, width - a.shape[-1])])


def _pad_rows(a, rows):
    return jnp.pad(a, [(0, 0), (0, rows - a.shape[1]), (0, 0)])


def _swap_halves(a):
    return jnp.concatenate([a[..., HALF_ROPE:], a[..., :HALF_ROPE]], axis=-1)


def _split_w_in_kernel(wt_ref, *out_refs, bounds):
    for o_ref, (lo, hi) in zip(out_refs, bounds):
        width = o_ref.shape[1]
        piece = wt_ref[lo:lo + width, :].T
        if hi - lo < width:
            lane = lax.broadcasted_iota(jnp.int32, piece.shape, 1)
            piece = jnp.where(lane < hi - lo, piece, 0.0)
        o_ref[...] = piece.astype(BF16)


def _split_w_in(w_in_t, *, tm):
    cols, D = w_in_t.shape
    hb3 = (cols - Q_LORA - KV_LORA - ROPE) // 3
    names = ("cq", "ckv", "kpe", "qb", "kb", "vb")
    widths = (Q_LORA, KV_LORA, LANES, hb3, hb3, hb3)
    starts = [0, Q_LORA, Q_LORA + KV_LORA, Q_LORA + KV_LORA + ROPE]
    starts += [starts[3] + hb3, starts[3] + 2 * hb3, starts[3] + 3 * hb3]
    assert starts[-1] == cols and D % tm == 0
    outs = pl.pallas_call(
        functools.partial(_split_w_in_kernel, bounds=tuple(zip(starts[:-1], starts[1:]))),
        grid=(D // tm,),
        in_specs=[pl.BlockSpec((cols, tm), lambda i: (0, i))],
        out_specs=[pl.BlockSpec((tm, n), lambda i: (i, 0)) for n in widths],
        out_shape=[jax.ShapeDtypeStruct((D, n), BF16) for n in widths],
        compiler_params=_params("arbitrary"),
        name="split_w_in",
    )(w_in_t)
    return dict(zip(names, outs))


def _prep_weights(w_in, w_uq, w_uk, w_uv):
    w = _split_w_in(w_in.T, tm=ROWS_W_IN)
    ha = w_uq.shape[1] // QK_MLA
    uq = w_uq.astype(BF16).reshape(Q_LORA, ha, QK_MLA)
    nope = uq[:, :, :NOPE].reshape(Q_LORA, ha * NOPE)
    rope = uq[:, :, NOPE:]
    w["uq"] = jnp.concatenate(
        [nope, rope.reshape(Q_LORA, ha * ROPE), _swap_halves(rope).reshape(Q_LORA, ha * ROPE)], axis=1)
    w["uk"] = w_uk.astype(BF16)
    w["uvt"] = w_uv.astype(BF16).T
    return w


def _softmax_setup(g_qa, g_ka, g_qb, g_kb, rel_bias):
    c_a = QK_MLA ** -0.5 * LOG2E
    c_b = D_BAND ** -0.5 * LOG2E
    amax = lambda a: jnp.max(jnp.abs(a))
    bound_a = c_a * QK_MLA * amax(g_qa) * amax(g_ka) * BOUND_MARGIN
    bound_b = c_b * D_BAND * amax(g_qb) * amax(g_kb) * BOUND_MARGIN + LOG2E * amax(rel_bias)
    fast_a = bound_a <= FAST_LIMIT
    fast_b = bound_b <= FAST_LIMIT
    shift_a = jnp.where(fast_a, bound_a, 0.0)
    shift_b = jnp.where(fast_b, bound_b, 0.0)
    lane = jnp.arange(LANES)[None, :]
    twice = lambda a: jnp.concatenate([a, a], axis=-1)
    g = {"qa_n": g_qa[None, :NOPE] * c_a, "qa_r": twice(g_qa[None, NOPE:]) * c_a,
         "qa_rs": twice(_swap_halves(g_qa[None, NOPE:])) * c_a,
         "q_aug": (lane == AUG_LANE).astype(F32),
         "ka_n": g_ka[None, :NOPE], "ka_r": _pad_lanes(g_ka[None, NOPE:]),
         "k_aug": jnp.where(lane == AUG_LANE, -shift_a, 0.0).astype(F32),
         "qb": g_qb * c_b}
    return g, fast_a, fast_b, shift_b


def _take_static(a, idx):
    idx = np.asarray(idx)
    steps = np.diff(idx)
    pieces, i = [], 0
    while i < len(idx):
        step = int(steps[i]) if i < len(steps) and steps[i] in (-1, 1) else 0
        j = i + 1
        while j < len(idx) and idx[j] - idx[j - 1] == step:
            j += 1
        lo, hi = sorted((int(idx[i]), int(idx[j - 1])))
        run = a[:, lo:hi + 1]
        pieces.append(jnp.broadcast_to(run, (a.shape[0], j - i)) if step == 0 else run[:, ::step])
        i = j
    return jnp.concatenate(pieces, axis=1)


def _band_vectors(rel_bias, shift):
    t = BAND_PAST
    e = np.arange(2 * t)
    e = np.where(e < t, e, e - 2 * t)
    rows = [_take_static(rel_bias, np.clip(d, -MAX_REL, MAX_REL) + MAX_REL) for d in (-e, t - e)]
    return (jnp.stack(rows, axis=1) * LOG2E - shift).astype(F32)


def _band_sample_vectors(rel_bias, lb):
    H = rel_bias.shape[0]

    def interleave(dist, n_pos, n):
        e = np.arange(n)
        e = np.where(e < n_pos, e, e - n)
        u = _take_static(rel_bias, np.clip(dist - e, -MAX_REL, MAX_REL) + MAX_REL) * LOG2E
        own = np.arange(H)[:, None, None] == np.arange(H)[None, None, :]
        return jnp.where(own, u[:, :, None], NEG).reshape(H, n * H).astype(F32)

    return interleave(lb, lb, lb + LANES), interleave(0, CHUNK, 2 * CHUNK)


def _row_tile(n, pref):
    return pref if n % pref == 0 else n


def kernel(x_prompt, x_sample, cache_mla_ckv, cache_mla_kpe, cache_band_k, cache_band_v,
           norm_mix, w_in, g_cq, w_uq, g_ckv, w_uk, w_uv, g_qa, g_ka, g_qb, g_kb, rel_bias,
           w_o, norm_ffn, w_up, w_down):
    depth = w_in.shape[0]
    assert depth == 1, "single-layer step"
    B, S, D = x_prompt.shape
    Bd, T, _ = x_sample.shape
    P = cache_mla_ckv.shape[2]
    Lb = cache_band_k.shape[2]
    keep_p = min(BAND_PAST, S)
    assert S % BAND_PAST == 0 and T <= CHUNK

    w = _prep_weights(w_in[0], w_uq[0], w_uk[0], w_uv[0])
    g, fast_a, fast_b, shift_b = _softmax_setup(g_qa[0], g_ka[0], g_qb, g_kb, rel_bias[0])
    ha = w["uk"].shape[1] // NOPE
    hb = rel_bias.shape[1]
    hw = hb * D_BAND
    gains = (g_cq, g_ckv, g["qb"], g_kb)
    q_args = (w["uq"], g["qa_n"], g["qa_r"], g["qa_rs"], g["q_aug"])
    kv_args = (w["uk"], w["uvt"], g["ka_n"], g["ka_r"], g["k_aug"])

    rn = LANES
    tabs = _rope_tables(max(S, P + rn))
    cqn, ckv, kpe_pad, kpe_t, qb, kb, vb, kb_tail, vb_tail, k, vt = _proj(
        x_prompt, norm_mix, w, *gains, keep=keep_p, tm=_row_tile(S, ROWS_PROJ), kv=(*kv_args, *tabs))
    q = _q_up(cqn, *q_args, *tabs, tm=_row_tile(S, ROWS_Q_UP))
    tqa = _row_tile(S, ROWS_FLASH)
    flash = lambda bounded: functools.partial(_mla_flash, tq=tqa, td=min(tqa, ROWS_FLASH_DIAG), hg=2,
                                              bounded=bounded)
    late = (w_o[0], w_up[0], w_down[0])
    ride = [_can_ride(a, _flash_steps(B, ha, S, tqa, 2)) for a in late]
    oa, *cast = lax.cond(fast_a, flash(True), flash(False), q, k, vt, *(a for a, r in zip(late, ride) if r))
    cast = iter(cast)
    w["o"], w["up"], w["down"] = (next(cast) if r else a.astype(BF16) for a, r in zip(late, ride))
    u_tab = _band_vectors(rel_bias[0], shift_b)
    ob = lax.cond(fast_b, functools.partial(_band, bounded=True),
                  functools.partial(_band, bounded=False), qb, kb, vb, u_tab)
    h = _merge(x_prompt, oa, ob, w["o"], tm=_row_tile(S, ROWS_MERGE))
    y_prompt = _ffn(h, norm_ffn, w["up"], w["down"], tm=_row_tile(S, ROWS_FFN), tf=COLS_FFN)

    n_s = Bd * T
    xs = x_sample.reshape(1, n_s, D)
    cqn_s, ckv_s, kpe_pad_s, kpe_s, qb_s, kb_s, vb_s, kb_s32, vb_s32 = _proj(
        xs, norm_mix, w, *gains, keep=n_s, tm=_row_tile(n_s, ROWS_PROJ_SAMPLE))
    tabs_s = tuple(jnp.tile(t[P:P + T], (Bd, 1)) for t in tabs)
    q_s = _q_up(cqn_s, *q_args, *tabs_s, tm=n_s)
    oa_s = _mla_sample(
        q_s, cache_mla_ckv[0], jnp.swapaxes(cache_mla_kpe[0], 1, 2),
        _pad_rows(ckv_s.reshape(Bd, T, KV_LORA), rn), _pad_rows(kpe_pad_s.reshape(Bd, T, LANES), rn),
        *kv_args, tabs, t_new=T)
    w_c, w_n = _band_sample_vectors(rel_bias[0], Lb)
    ob_s = _band_sample(
        qb_s.reshape(Bd, T, hw), cache_band_k.reshape(Bd, Lb * hb, D_BAND), cache_band_v.reshape(Bd, Lb * hb, D_BAND),
        kb_s.reshape(Bd, T * hb, D_BAND), vb_s.reshape(Bd, T * hb, D_BAND), w_c, w_n, n_heads=hb)
    h_s = _merge(xs, oa_s.reshape(1, n_s, ha * V_MLA), ob_s.reshape(1, n_s, hw), w["o"], tm=n_s)
    y_sample = _ffn(h_s, norm_ffn, w["up"], w["down"], tm=n_s, tf=COLS_FFN).reshape(Bd, T, D)

    return (y_prompt, y_sample,
            ckv[None], jnp.swapaxes(kpe_t, 1, 2)[None],
            kb_tail.reshape(B, keep_p, hb, D_BAND)[None], vb_tail.reshape(B, keep_p, hb, D_BAND)[None],
            ckv_s.reshape(Bd, T, KV_LORA)[None], kpe_s.reshape(Bd, T, ROPE)[None],
            kb_s32.reshape(Bd, T, hb, D_BAND)[None], vb_s32.reshape(Bd, T, hb, D_BAND)[None])
```

```python
import functools

import jax
import jax.numpy as jnp
import numpy as np
from jax import lax
from jax.experimental import pallas as pl
from jax.experimental.pallas import tpu as pltpu

CHUNK = 64
EPS = 1e-6
NOPE = 128
ROPE = 64
HALF_ROPE = ROPE // 2
QK_MLA = NOPE + ROPE
V_MLA = 128
D_BAND = 128
BAND_CHUNKS = 8
BAND_PAST = BAND_CHUNKS * CHUNK
MAX_REL = 128
ROPE_BASE = 10000.0
Q_LORA = 512
KV_LORA = 256
NEG = -1e30
LOG2E = 1.4426950408889634

BOUND_MARGIN = 1.02
FAST_LIMIT = 60.0

LANES = 128
BF16_ROWS = 16
QK_PAD = 2 * LANES
AUG_LANE = ROPE
VMEM_LIMIT = 56 * 1024 * 1024
ROWS_W_IN = 256
ROWS_PROJ = 512
ROWS_PROJ_SAMPLE = 256
ROWS_Q_UP = 512
ROWS_FLASH = 1024
ROWS_FLASH_DIAG = 256
ROWS_MERGE = 512
ROWS_FFN = 512
COLS_FFN = 2048

BF16 = jnp.bfloat16
F32 = jnp.float32

NT_DIMS = (((1,), (1,)), ((), ()))


def _params(*sem):
    return pltpu.CompilerParams(dimension_semantics=sem, vmem_limit_bytes=VMEM_LIMIT)


def _resident(a):
    return pl.BlockSpec(a.shape, lambda *_: (0,) * a.ndim, pipeline_mode=pl.Buffered(1))


def _rms(x, g):
    ms = jnp.mean(x * x, axis=-1, keepdims=True)
    return x * lax.rsqrt(ms + EPS) * g


def _dot(a, b):
    return jnp.dot(a, b, preferred_element_type=F32)


def _dot_nt(a, b):
    return lax.dot_general(a, b, NT_DIMS, preferred_element_type=F32)


N_PROJ_IN, N_PROJ_OUT, N_KV_IN = 12, 9, 7


def _proj_kernel(*refs, n_heads, n_mla_heads, tail_start):
    (x_ref, nm_ref, wcq_ref, wckv_ref, wkpe_ref, wqb_ref, wkb_ref, wvb_ref,
     gcq_ref, gckv_ref, gqb_ref, gkb_ref) = refs[:N_PROJ_IN]
    n_kv = N_KV_IN if n_mla_heads else 0
    outs = refs[N_PROJ_IN + n_kv:]
    cqn_ref, ckv_ref, kpe_ref, kpe_out_ref, qb_ref, kb_ref, vb_ref, kbt_ref, vbt_ref = outs[:N_PROJ_OUT]
    t = pl.program_id(1)
    xn = _rms(x_ref[0], nm_ref[...]).astype(BF16)
    cqn_ref[0] = _rms(_dot(xn, wcq_ref[...]), gcq_ref[...]).astype(BF16)
    ckv = _rms(_dot(xn, wckv_ref[...]), gckv_ref[...])
    ckv_ref[0] = ckv
    kpe = _dot(xn, wkpe_ref[...])
    kpe_ref[0] = kpe
    kpe_out_ref[0] = kpe.T[:ROPE] if n_mla_heads else kpe[:, :ROPE]
    if n_mla_heads:
        wuk_ref, wuvt_ref, gn_ref, gr_ref, aug_ref, cos_ref, sin_ref = refs[N_PROJ_IN:N_PROJ_IN + n_kv]
        k_ref, vt_ref = outs[N_PROJ_OUT:]
        keys, vt = _expand_kv(ckv, kpe, wuk_ref[...], wuvt_ref[...], gn_ref[...], gr_ref[...],
                              aug_ref[...], cos_ref[...], sin_ref[...], n_mla_heads)
        for h in range(n_mla_heads):
            k_ref[0, h] = keys[h]
            vt_ref[0, h] = vt[h * V_MLA:(h + 1) * V_MLA, :].astype(BF16)
    qb = _dot(xn, wqb_ref[...])
    kb = _dot(xn, wkb_ref[...])
    vb = _dot(xn, wvb_ref[...])
    vb_ref[0] = vb.astype(BF16)
    kbn = []
    for h in range(n_heads):
        sl = slice(h * D_BAND, (h + 1) * D_BAND)
        qb_ref[0, :, sl] = _rms(qb[:, sl], gqb_ref[...]).astype(BF16)
        kbn.append(_rms(kb[:, sl], gkb_ref[...]))
    kbn = jnp.concatenate(kbn, axis=-1)
    kb_ref[0] = kbn.astype(BF16)

    @pl.when(t >= tail_start)
    def _():
        kbt_ref[0] = kbn
        vbt_ref[0] = vb


def _proj(x, norm_mix, w, g_cq, g_ckv, g_qb, g_kb, *, keep, tm, kv=None):
    B, S, D = x.shape
    n_heads = w["qb"].shape[1] // D_BAND
    hw = n_heads * D_BAND
    nt = S // tm
    tail_start = nt - keep // tm
    row = lambda b, t: (b, t, 0)
    tail = lambda b, t: (b, jnp.maximum(t - tail_start, 0), 0)
    if kv is not None:
        kpe_shape, kpe_spec = (B, ROPE, S), pl.BlockSpec((1, ROPE, tm), lambda b, t: (b, 0, t))
    else:
        kpe_shape, kpe_spec = (B, S, ROPE), pl.BlockSpec((1, tm, ROPE), row)
    out_shapes = (
        jax.ShapeDtypeStruct((B, S, Q_LORA), BF16),
        jax.ShapeDtypeStruct((B, S, KV_LORA), F32),
        jax.ShapeDtypeStruct((B, S, LANES), F32),
        jax.ShapeDtypeStruct(kpe_shape, F32),
        jax.ShapeDtypeStruct((B, S, hw), BF16),
        jax.ShapeDtypeStruct((B, S, hw), BF16),
        jax.ShapeDtypeStruct((B, S, hw), BF16),
        jax.ShapeDtypeStruct((B, keep, hw), F32),
        jax.ShapeDtypeStruct((B, keep, hw), F32),
    )
    out_specs = (
        pl.BlockSpec((1, tm, Q_LORA), row),
        pl.BlockSpec((1, tm, KV_LORA), row),
        pl.BlockSpec((1, tm, LANES), row),
        kpe_spec,
        pl.BlockSpec((1, tm, hw), row),
        pl.BlockSpec((1, tm, hw), row),
        pl.BlockSpec((1, tm, hw), row),
        pl.BlockSpec((1, tm, hw), tail),
        pl.BlockSpec((1, tm, hw), tail),
    )
    weights = (w["cq"], w["ckv"], w["kpe"], w["qb"], w["kb"], w["vb"])
    gains = (g_cq, g_ckv, g_qb, g_kb)
    kv_specs, kv_args, ha = [], (), 0
    if kv is not None:
        ha = kv[0].shape[1] // NOPE
        pos = lambda b, t: (t, 0)
        kv_specs = [_resident(a) for a in kv[:5]] + [pl.BlockSpec((tm, LANES), pos)] * 2
        kv_args = tuple(kv)
        out_shapes += (jax.ShapeDtypeStruct((B, ha, S, QK_PAD), BF16),
                       jax.ShapeDtypeStruct((B, ha, V_MLA, S), BF16))
        out_specs += (pl.BlockSpec((1, ha, tm, QK_PAD), lambda b, t: (b, 0, t, 0)),
                      pl.BlockSpec((1, ha, V_MLA, tm), lambda b, t: (b, 0, 0, t)))
    return pl.pallas_call(
        functools.partial(_proj_kernel, n_heads=n_heads, n_mla_heads=ha, tail_start=tail_start),
        grid=(B, nt),
        in_specs=[pl.BlockSpec((1, tm, D), row), _resident(norm_mix)]
        + [_resident(a) for a in weights] + [_resident(a) for a in gains] + kv_specs,
        out_specs=out_specs,
        out_shape=out_shapes,
        compiler_params=_params("arbitrary", "arbitrary"),
        name="proj",
    )(x, norm_mix, *weights, *gains, *kv_args)


def _q_up_kernel(cqn_ref, w_ref, gn_ref, gr_ref, grs_ref, aug_ref, cos_ref, sin_ref, q_ref,
                 *, n_heads):
    hw = n_heads * LANES
    hp = hw // 2
    y = _dot(cqn_ref[0], w_ref[...])
    low = lax.broadcasted_iota(jnp.int32, (1, LANES), 1) < ROPE
    for j in range(n_heads // 2):
        r = y[:, hw + j * LANES: hw + (j + 1) * LANES]
        rs = y[:, hw + hp + j * LANES: hw + hp + (j + 1) * LANES]
        r2 = r * r
        r2_of = (jnp.where(low, r2, 0.0), jnp.where(low, 0.0, r2))
        rinv = []
        for e in range(2):
            h = 2 * j + e
            n = y[:, h * LANES:(h + 1) * LANES]
            ss = jnp.sum(n * n + r2_of[e], axis=-1, keepdims=True)
            rinv.append(lax.rsqrt(ss * (1.0 / QK_MLA) + EPS))
            q_ref[0, h, :, 0:LANES] = (n * rinv[e] * gn_ref[...]).astype(BF16)
        rot = ((r * gr_ref[...]) * cos_ref[...] + (rs * grs_ref[...]) * sin_ref[...]) \
            * jnp.where(low, rinv[0], rinv[1])
        q_ref[0, 2 * j, :, LANES:QK_PAD] = (jnp.where(low, rot, 0.0) + aug_ref[...]).astype(BF16)
        q_ref[0, 2 * j + 1, :, LANES:QK_PAD] = (
            jnp.where(low, pltpu.roll(rot, ROPE, 1), 0.0) + aug_ref[...]).astype(BF16)


def _q_up(cqn, w, g_n, g_r, g_rs, aug, cos, sin, *, tm):
    B, S, _ = cqn.shape
    n_heads = w.shape[1] // (2 * LANES)
    row = lambda b, t: (b, t, 0)
    pos = lambda b, t: (t, 0)
    return pl.pallas_call(
        functools.partial(_q_up_kernel, n_heads=n_heads),
        grid=(B, S // tm),
        in_specs=[pl.BlockSpec((1, tm, Q_LORA), row), _resident(w),
                  _resident(g_n), _resident(g_r), _resident(g_rs), _resident(aug),
                  pl.BlockSpec((tm, LANES), pos), pl.BlockSpec((tm, LANES), pos)],
        out_specs=pl.BlockSpec((1, n_heads, tm, QK_PAD), lambda b, t: (b, 0, t, 0)),
        out_shape=jax.ShapeDtypeStruct((B, n_heads, S, QK_PAD), BF16),
        compiler_params=_params("arbitrary", "arbitrary"),
        name="q_up",
    )(cqn, w, g_n, g_r, g_rs, aug, cos, sin)


def _expand_kv(c, kpe, wuk, wuvt, g_n, g_r, k_aug, cos, sin, n_heads):
    cb = c.astype(BF16)
    kn = _dot(cb, wuk)
    vt = _dot_nt(wuvt, cb)
    a = kpe * g_r
    sspe = jnp.sum(kpe * kpe, axis=-1, keepdims=True)
    a_sw = pltpu.roll(a, HALF_ROPE, 1) + pltpu.roll(a, LANES - HALF_ROPE, 1)
    low = lax.broadcasted_iota(jnp.int32, (1, LANES), 1) < ROPE
    rot = jnp.where(low, a * cos + a_sw * sin, 0.0)
    keys = []
    for h in range(n_heads):
        n = kn[:, h * NOPE:(h + 1) * NOPE]
        ss = jnp.sum(n * n, axis=-1, keepdims=True) + sspe
        rinv = lax.rsqrt(ss * (1.0 / QK_MLA) + EPS)
        keys.append(jnp.concatenate([(n * rinv * g_n).astype(BF16),
                                     (rot * rinv + k_aug).astype(BF16)], axis=-1))
    return keys, vt


N_FLASH_IN = 3


def _mla_flash_kernel(*refs, tq, td, hg, bounded, n_riders):
    q_ref, k_ref, vt_ref = refs[:N_FLASH_IN]
    o_ref = refs[N_FLASH_IN + n_riders]
    m_sc, l_sc, acc_sc = refs[N_FLASH_IN + 1 + 2 * n_riders:]
    for src, dst in zip(refs[N_FLASH_IN:N_FLASH_IN + n_riders],
                        refs[N_FLASH_IN + 1 + n_riders:N_FLASH_IN + 1 + 2 * n_riders]):
        dst[...] = src[...].astype(BF16)
    i = pl.program_id(2)
    if not bounded:
        m_sc[...] = jnp.full(m_sc.shape, -jnp.inf, F32)
    l_sc[...] = jnp.zeros(l_sc.shape, F32)
    acc_sc[...] = jnp.zeros(acc_sc.shape, F32)

    def step(tiles):
        sts = [(hh, t, _dot_nt(k_ref[0, hh, pl.ds(t[0], t[1]), :], q_ref[0, hh, t[2]:, :]))
               for t in tiles for hh in range(hg)]
        for hh, (k0, kn, q0, mask), st in sts:
            if mask is not None:
                st = jnp.where(mask, st, NEG)
            vt = vt_ref[0, hh, :, pl.ds(k0, kn)]
            if bounded:
                p = jnp.exp2(st)
                l_sc[hh, :, q0:] += jnp.sum(p, axis=0, keepdims=True)
                acc_sc[hh, :, q0:] += _dot(vt, p.astype(BF16))
            else:
                m_prev = m_sc[hh, :, q0:]
                m_new = jnp.maximum(m_prev, jnp.max(st, axis=0, keepdims=True))
                alpha = jnp.exp2(m_prev - m_new)
                p = jnp.exp2(st - m_new)
                l_sc[hh, :, q0:] = alpha * l_sc[hh, :, q0:] + jnp.sum(p, axis=0, keepdims=True)
                acc_sc[hh, :, q0:] = acc_sc[hh, :, q0:] * alpha + _dot(vt, p.astype(BF16))
                m_sc[hh, :, q0:] = m_new

    def body(j, carry):
        step([(pl.multiple_of(j * tq, tq), tq, 0, None)])
        return carry

    lax.fori_loop(0, i, body, 0)
    diag = []
    for d in range(tq // td):
        nq = tq - d * td
        q_chunk = lax.broadcasted_iota(jnp.int32, (td, nq), 1) // CHUNK
        k_chunk = lax.broadcasted_iota(jnp.int32, (td, nq), 0) // CHUNK
        diag.append((pl.multiple_of(i * tq + d * td, td), td, d * td, k_chunk <= q_chunk))
    step(diag)
    for hh in range(hg):
        o_ref[0, :, hh * V_MLA:(hh + 1) * V_MLA] = (acc_sc[hh] / l_sc[hh]).T.astype(o_ref.dtype)


def _flash_steps(B, H, S, tq, hg):
    return B * (H // hg) * (S // tq)


def _can_ride(a, n_steps):
    return a.shape[0] % n_steps == 0 and (a.shape[0] // n_steps) % BF16_ROWS == 0


def _mla_flash(q, k, vt, *riders, tq, td, hg, bounded):
    B, H, S, _ = q.shape
    nh, nq = H // hg, S // tq
    step = lambda b, h, i: ((b * nh + h) * nq + i, 0)
    rider_spec = lambda a: pl.BlockSpec((a.shape[0] // (B * nh * nq), a.shape[1]), step)
    return pl.pallas_call(
        functools.partial(_mla_flash_kernel, tq=tq, td=td, hg=hg, bounded=bounded, n_riders=len(riders)),
        grid=(B, nh, nq),
        in_specs=[pl.BlockSpec((1, hg, tq, QK_PAD), lambda b, h, i: (b, h, i, 0)),
                  pl.BlockSpec((1, hg, S, QK_PAD), lambda b, h, i: (b, h, 0, 0)),
                  pl.BlockSpec((1, hg, V_MLA, S), lambda b, h, i: (b, h, 0, 0))]
        + [rider_spec(a) for a in riders],
        out_specs=[pl.BlockSpec((1, tq, hg * V_MLA), lambda b, h, i: (b, i, h))]
        + [rider_spec(a) for a in riders],
        out_shape=[jax.ShapeDtypeStruct((B, S, H * V_MLA), BF16)]
        + [jax.ShapeDtypeStruct(a.shape, BF16) for a in riders],
        scratch_shapes=[pltpu.VMEM((hg, 1, tq), F32), pltpu.VMEM((hg, 1, tq), F32),
                        pltpu.VMEM((hg, V_MLA, tq), F32)],
        compiler_params=_params("arbitrary", "arbitrary", "arbitrary"),
        name="mla_flash_bounded" if bounded else "mla_flash",
    )(q, k, vt, *riders)


def _toeplitz(u_row, rows, cols, stride=1):
    u = jnp.broadcast_to(u_row, (rows, u_row.shape[-1]))
    return pltpu.roll(u, 0, 1, stride=stride, stride_axis=0)[:, :cols]


def _softmax_pv(parts, bounded):
    m = None
    if not bounded:
        for s, _ in parts:
            mx = jnp.max(s, axis=-1, keepdims=True)
            m = mx if m is None else jnp.maximum(m, mx)
    l = None
    acc = None
    for s, v in parts:
        p = jnp.exp2(s if bounded else s - m)
        ps = jnp.sum(p, axis=-1, keepdims=True)
        pv = _dot(p.astype(BF16), v)
        l = ps if l is None else l + ps
        acc = pv if acc is None else acc + pv
    return acc / l


def _band_kernel(q_ref, kc_ref, kp_ref, vc_ref, vp_ref, u_ref, o_ref, bc_sc, bp_sc,
                 *, n_heads, splits, bounded):
    b, g = pl.program_id(0), pl.program_id(1)
    tg = bc_sc.shape[1]

    @pl.when((b == 0) & (g == 0))
    def _():
        qc = lax.broadcasted_iota(jnp.int32, (tg, tg), 0) // CHUNK
        kc = lax.broadcasted_iota(jnp.int32, (tg, tg), 1) // CHUNK
        for h in range(n_heads):
            bc_sc[h] = jnp.where(kc <= qc, _toeplitz(u_ref[h, 0:1, :], tg, tg), NEG)
            bp_sc[h] = jnp.where(kc >= qc, _toeplitz(u_ref[h, 1:2, :], tg, tg), NEG)

    no_prev = jnp.where(g == 0, NEG, 0.0)
    tr = tg // splits
    for h0 in range(0, n_heads, 2):
        scores = []
        for h in (h0, h0 + 1):
            sl = slice(h * D_BAND, (h + 1) * D_BAND)
            for a in range(splits):
                rows, old, new = slice(a * tr, (a + 1) * tr), slice(a * tr, tg), slice(0, (a + 1) * tr)
                q = q_ref[0, rows, sl]
                scores.append((rows, old, new, sl,
                               _dot_nt(q, kp_ref[0, old, sl]) + (bp_sc[h, rows, old] + no_prev),
                               _dot_nt(q, kc_ref[0, new, sl]) + bc_sc[h, rows, new]))
        for rows, old, new, sl, s_prev, s_cur in scores:
            parts = [(s_prev, vp_ref[0, old, sl]), (s_cur, vc_ref[0, new, sl])]
            o_ref[0, rows, sl] = _softmax_pv(parts, bounded).astype(o_ref.dtype)


def _band(qb, kb, vb, u_tab, *, bounded):
    B, S, hw = qb.shape
    H = hw // D_BAND
    tg = BAND_PAST
    cur = lambda b, g: (b, g, 0)
    prev = lambda b, g: (b, jnp.maximum(g - 1, 0), 0)
    blk = (1, tg, hw)
    return pl.pallas_call(
        functools.partial(_band_kernel, n_heads=H, splits=4, bounded=bounded),
        grid=(B, S // tg),
        in_specs=[pl.BlockSpec(blk, cur), pl.BlockSpec(blk, cur), pl.BlockSpec(blk, prev),
                  pl.BlockSpec(blk, cur), pl.BlockSpec(blk, prev),
                  _resident(u_tab)],
        out_specs=pl.BlockSpec(blk, cur),
        out_shape=jax.ShapeDtypeStruct((B, S, hw), BF16),
        scratch_shapes=[pltpu.VMEM((H, tg, tg), F32), pltpu.VMEM((H, tg, tg), F32)],
        compiler_params=_params("arbitrary", "arbitrary"),
        name="band_bounded" if bounded else "band",
    )(qb, kb, kb, vb, vb, u_tab)


def _mla_sample_kernel(q_ref, cc_ref, pc_ref, cn_ref, pn_ref, wuk_ref, wuvt_ref, gn_ref, gr_ref,
                       aug_ref, cosc_ref, sinc_ref, cosn_ref, sinn_ref, o_ref, *, n_heads, t_new):
    args = (wuk_ref[...], wuvt_ref[...], gn_ref[...], gr_ref[...], aug_ref[...])
    pc = jnp.concatenate([pc_ref[0], jnp.zeros((LANES - ROPE, pc_ref.shape[2]), F32)], axis=0).T
    kc, vtc = _expand_kv(cc_ref[0], pc, *args, cosc_ref[...], sinc_ref[...], n_heads)
    kn, vtn = _expand_kv(cn_ref[0], pn_ref[0], *args, cosn_ref[...], sinn_ref[...], n_heads)
    rows_n = cn_ref.shape[1]
    new_ok = lax.broadcasted_iota(jnp.int32, (q_ref.shape[2], rows_n), 1) < t_new
    scores = [(_dot_nt(q_ref[0, h], kc[h]), jnp.where(new_ok, _dot_nt(q_ref[0, h], kn[h]), NEG))
              for h in range(n_heads)]
    for h, (s_c, s_n) in enumerate(scores):
        m = jnp.maximum(jnp.max(s_c, axis=-1, keepdims=True), jnp.max(s_n, axis=-1, keepdims=True))
        p_c = jnp.exp2(s_c - m)
        p_n = jnp.exp2(s_n - m)
        l = jnp.sum(p_c, axis=-1, keepdims=True) + jnp.sum(p_n, axis=-1, keepdims=True)
        sl = slice(h * V_MLA, (h + 1) * V_MLA)
        pv = (_dot_nt(p_c.astype(BF16), vtc[sl].astype(BF16))
              + _dot_nt(p_n.astype(BF16), vtn[sl].astype(BF16)))
        o_ref[0, :, sl] = (pv / l).astype(o_ref.dtype)


def _mla_sample(q, ckv_c, kpe_c, ckv_n, kpe_n, w_uk, w_uvt, g_n, g_r, k_aug, tabs, *, t_new):
    Bd = ckv_c.shape[0]
    H, T = q.shape[1], q.shape[2] // Bd
    P = ckv_c.shape[1]
    rn = ckv_n.shape[1]
    assert P % rn == 0
    old = lambda: pl.BlockSpec((P, LANES), lambda b: (0, 0), pipeline_mode=pl.Buffered(1))
    new = lambda: pl.BlockSpec((rn, LANES), lambda b: (P // rn, 0), pipeline_mode=pl.Buffered(1))
    bat = lambda b: (b, 0, 0)
    return pl.pallas_call(
        functools.partial(_mla_sample_kernel, n_heads=H, t_new=t_new),
        grid=(Bd,),
        in_specs=[pl.BlockSpec((1, H, T, QK_PAD), lambda b: (0, 0, b, 0)),
                  pl.BlockSpec((1, P, KV_LORA), bat), pl.BlockSpec((1, ROPE, P), bat),
                  pl.BlockSpec((1, rn, KV_LORA), bat), pl.BlockSpec((1, rn, LANES), bat),
                  _resident(w_uk), _resident(w_uvt), _resident(g_n), _resident(g_r), _resident(k_aug),
                  old(), old(), new(), new()],
        out_specs=pl.BlockSpec((1, T, H * V_MLA), bat),
        out_shape=jax.ShapeDtypeStruct((Bd, T, H * V_MLA), BF16),
        compiler_params=_params("arbitrary"),
        name="mla_sample",
    )(q, ckv_c, kpe_c, ckv_n, kpe_n, w_uk, w_uvt, g_n, g_r, k_aug, *tabs, *tabs)


def _band_sample_kernel(q_ref, kc_ref, vc_ref, kn_ref, vn_ref, wc_ref, wn_ref, o_ref, *, n_heads):
    T = q_ref.shape[1]
    heads = [slice(h * D_BAND, (h + 1) * D_BAND) for h in range(n_heads)]
    q = jnp.concatenate([q_ref[0, :, sl] for sl in heads], axis=0)
    tab = lambda w_ref, cols: jnp.concatenate(
        [_toeplitz(w_ref[h:h + 1, :], T, cols, stride=n_heads) for h in range(n_heads)], axis=0)
    s_c = _dot_nt(q, kc_ref[0].astype(BF16)) + tab(wc_ref, kc_ref.shape[1])
    s_n = _dot_nt(q, kn_ref[0]) + tab(wn_ref, kn_ref.shape[1])
    parts = [(s_c, vc_ref[0].astype(BF16)), (s_n, vn_ref[0])]
    o = _softmax_pv(parts, False).astype(o_ref.dtype)
    for h, sl in enumerate(heads):
        o_ref[0, :, sl] = o[h * T:(h + 1) * T]


def _band_sample(q, k_c, v_c, k_n, v_n, w_c, w_n, *, n_heads):
    Bd = q.shape[0]
    bat = lambda b: (b, 0, 0)
    blk = lambda a: pl.BlockSpec((1,) + a.shape[1:], bat)
    return pl.pallas_call(
        functools.partial(_band_sample_kernel, n_heads=n_heads),
        grid=(Bd,),
        in_specs=[blk(q), blk(k_c), blk(v_c), blk(k_n), blk(v_n), _resident(w_c), _resident(w_n)],
        out_specs=blk(q),
        out_shape=jax.ShapeDtypeStruct(q.shape, BF16),
        compiler_params=_params("arbitrary"),
        name="band_sample",
    )(q, k_c, v_c, k_n, v_n, w_c, w_n)


def _merge_kernel(x_ref, oa_ref, ob_ref, wo_ref, h_ref):
    half = oa_ref.shape[-1]
    h_ref[0] = x_ref[0] + _dot(oa_ref[0], wo_ref[:half, :]) + _dot(ob_ref[0], wo_ref[half:, :])


def _merge(x, oa, ob, w_o, *, tm):
    B, S, D = x.shape
    row = lambda b, t: (b, t, 0)
    return pl.pallas_call(
        _merge_kernel,
        grid=(B, S // tm),
        in_specs=[pl.BlockSpec((1, tm, D), row), pl.BlockSpec((1, tm, oa.shape[-1]), row),
                  pl.BlockSpec((1, tm, ob.shape[-1]), row),
                  _resident(w_o)],
        out_specs=pl.BlockSpec((1, tm, D), row),
        out_shape=jax.ShapeDtypeStruct((B, S, D), F32),
        compiler_params=_params("arbitrary", "arbitrary"),
        name="merge",
    )(x, oa, ob, w_o)


def _ffn_kernel(h_ref, g_ref, wup_ref, wdn_ref, y_ref, hn_sc):
    j = pl.program_id(2)

    @pl.when(j == 0)
    def _():
        h = h_ref[0]
        hn_sc[...] = _rms(h, g_ref[...]).astype(BF16)
        y_ref[0] = h

    u = jnp.maximum(_dot(hn_sc[...], wup_ref[...]), 0.0)
    y_ref[0] += _dot((u * u).astype(BF16), wdn_ref[...])


def _ffn(h, g, w_up, w_down, *, tm, tf):
    B, S, D = h.shape
    F = w_up.shape[1]
    row = lambda b, t, j: (b, t, 0)
    return pl.pallas_call(
        _ffn_kernel,
        grid=(B, S // tm, F // tf),
        in_specs=[pl.BlockSpec((1, tm, D), row), _resident(g),
                  pl.BlockSpec((D, tf), lambda b, t, j: (0, j)),
                  pl.BlockSpec((tf, D), lambda b, t, j: (j, 0))],
        out_specs=pl.BlockSpec((1, tm, D), row),
        out_shape=jax.ShapeDtypeStruct((B, S, D), F32),
        scratch_shapes=[pltpu.VMEM((tm, D), BF16)],
        compiler_params=_params("arbitrary", "arbitrary", "arbitrary"),
        name="ffn",
    )(h, g, w_up, w_down)


def _rope_tables(n_pos):
    inv = 1.0 / (ROPE_BASE ** (np.arange(0, ROPE, 2, dtype=np.float64) / ROPE))
    ang = np.arange(n_pos, dtype=np.float64)[:, None] * inv
    c, s = np.cos(ang), np.sin(ang)
    return (np.concatenate([c, c, c, c], axis=1).astype(np.float32),
            np.concatenate([-s, s, -s, s], axis=1).astype(np.float32))


def _pad_lanes(a, width=LANES):
    return jnp.pad(a, [(0, 0)] * (a.ndim - 1) + [(0, width - a.shape[-1])])


def _pad_rows(a, rows):
    return jnp.pad(a, [(0, 0), (0, rows - a.shape[1]), (0, 0)])


def _swap_halves(a):
    return jnp.concatenate([a[..., HALF_ROPE:], a[..., :HALF_ROPE]], axis=-1)


def _split_w_in_kernel(wt_ref, *out_refs, bounds):
    for o_ref, (lo, hi) in zip(out_refs, bounds):
        width = o_ref.shape[1]
        piece = wt_ref[lo:lo + width, :].T
        if hi - lo < width:
            lane = lax.broadcasted_iota(jnp.int32, piece.shape, 1)
            piece = jnp.where(lane < hi - lo, piece, 0.0)
        o_ref[...] = piece.astype(BF16)


def _split_w_in(w_in_t, *, tm):
    cols, D = w_in_t.shape
    hb3 = (cols - Q_LORA - KV_LORA - ROPE) // 3
    names = ("cq", "ckv", "kpe", "qb", "kb", "vb")
    widths = (Q_LORA, KV_LORA, LANES, hb3, hb3, hb3)
    starts = [0, Q_LORA, Q_LORA + KV_LORA, Q_LORA + KV_LORA + ROPE]
    starts += [starts[3] + hb3, starts[3] + 2 * hb3, starts[3] + 3 * hb3]
    assert starts[-1] == cols and D % tm == 0
    outs = pl.pallas_call(
        functools.partial(_split_w_in_kernel, bounds=tuple(zip(starts[:-1], starts[1:]))),
        grid=(D // tm,),
        in_specs=[pl.BlockSpec((cols, tm), lambda i: (0, i))],
        out_specs=[pl.BlockSpec((tm, n), lambda i: (i, 0)) for n in widths],
        out_shape=[jax.ShapeDtypeStruct((D, n), BF16) for n in widths],
        compiler_params=_params("arbitrary"),
        name="split_w_in",
    )(w_in_t)
    return dict(zip(names, outs))


def _prep_weights(w_in, w_uq, w_uk, w_uv):
    w = _split_w_in(w_in.T, tm=ROWS_W_IN)
    ha = w_uq.shape[1] // QK_MLA
    uq = w_uq.astype(BF16).reshape(Q_LORA, ha, QK_MLA)
    nope = uq[:, :, :NOPE].reshape(Q_LORA, ha * NOPE)
    rope = uq[:, :, NOPE:]
    w["uq"] = jnp.concatenate(
        [nope, rope.reshape(Q_LORA, ha * ROPE), _swap_halves(rope).reshape(Q_LORA, ha * ROPE)], axis=1)
    w["uk"] = w_uk.astype(BF16)
    w["uvt"] = w_uv.astype(BF16).T
    return w


def _softmax_setup(g_qa, g_ka, g_qb, g_kb, rel_bias):
    c_a = QK_MLA ** -0.5 * LOG2E
    c_b = D_BAND ** -0.5 * LOG2E
    amax = lambda a: jnp.max(jnp.abs(a))
    bound_a = c_a * QK_MLA * amax(g_qa) * amax(g_ka) * BOUND_MARGIN
    bound_b = c_b * D_BAND * amax(g_qb) * amax(g_kb) * BOUND_MARGIN + LOG2E * amax(rel_bias)
    fast_a = bound_a <= FAST_LIMIT
    fast_b = bound_b <= FAST_LIMIT
    shift_a = jnp.where(fast_a, bound_a, 0.0)
    shift_b = jnp.where(fast_b, bound_b, 0.0)
    lane = jnp.arange(LANES)[None, :]
    twice = lambda a: jnp.concatenate([a, a], axis=-1)
    g = {"qa_n": g_qa[None, :NOPE] * c_a, "qa_r": twice(g_qa[None, NOPE:]) * c_a,
         "qa_rs": twice(_swap_halves(g_qa[None, NOPE:])) * c_a,
         "q_aug": (lane == AUG_LANE).astype(F32),
         "ka_n": g_ka[None, :NOPE], "ka_r": _pad_lanes(g_ka[None, NOPE:]),
         "k_aug": jnp.where(lane == AUG_LANE, -shift_a, 0.0).astype(F32),
         "qb": g_qb * c_b}
    return g, fast_a, fast_b, shift_b


def _take_static(a, idx):
    idx = np.asarray(idx)
    steps = np.diff(idx)
    pieces, i = [], 0
    while i < len(idx):
        step = int(steps[i]) if i < len(steps) and steps[i] in (-1, 1) else 0
        j = i + 1
        while j < len(idx) and idx[j] - idx[j - 1] == step:
            j += 1
        lo, hi = sorted((int(idx[i]), int(idx[j - 1])))
        run = a[:, lo:hi + 1]
        pieces.append(jnp.broadcast_to(run, (a.shape[0], j - i)) if step == 0 else run[:, ::step])
        i = j
    return jnp.concatenate(pieces, axis=1)


def _band_vectors(rel_bias, shift):
    t = BAND_PAST
    e = np.arange(2 * t)
    e = np.where(e < t, e, e - 2 * t)
    rows = [_take_static(rel_bias, np.clip(d, -MAX_REL, MAX_REL) + MAX_REL) for d in (-e, t - e)]
    return (jnp.stack(rows, axis=1) * LOG2E - shift).astype(F32)


def _band_sample_vectors(rel_bias, lb):
    H = rel_bias.shape[0]

    def interleave(dist, n_pos, n):
        e = np.arange(n)
        e = np.where(e < n_pos, e, e - n)
        u = _take_static(rel_bias, np.clip(dist - e, -MAX_REL, MAX_REL) + MAX_REL) * LOG2E
        own = np.arange(H)[:, None, None] == np.arange(H)[None, None, :]
        return jnp.where(own, u[:, :, None], NEG).reshape(H, n * H).astype(F32)

    return interleave(lb, lb, lb + LANES), interleave(0, CHUNK, 2 * CHUNK)


def _row_tile(n, pref):
    return pref if n % pref == 0 else n


def kernel(x_prompt, x_sample, cache_mla_ckv, cache_mla_kpe, cache_band_k, cache_band_v,
           norm_mix, w_in, g_cq, w_uq, g_ckv, w_uk, w_uv, g_qa, g_ka, g_qb, g_kb, rel_bias,
           w_o, norm_ffn, w_up, w_down):
    depth = w_in.shape[0]
    assert depth == 1, "single-layer step"
    B, S, D = x_prompt.shape
    Bd, T, _ = x_sample.shape
    P = cache_mla_ckv.shape[2]
    Lb = cache_band_k.shape[2]
    keep_p = min(BAND_PAST, S)
    assert S % BAND_PAST == 0 and T <= CHUNK

    w = _prep_weights(w_in[0], w_uq[0], w_uk[0], w_uv[0])
    g, fast_a, fast_b, shift_b = _softmax_setup(g_qa[0], g_ka[0], g_qb, g_kb, rel_bias[0])
    ha = w["uk"].shape[1] // NOPE
    hb = rel_bias.shape[1]
    hw = hb * D_BAND
    gains = (g_cq, g_ckv, g["qb"], g_kb)
    q_args = (w["uq"], g["qa_n"], g["qa_r"], g["qa_rs"], g["q_aug"])
    kv_args = (w["uk"], w["uvt"], g["ka_n"], g["ka_r"], g["k_aug"])

    rn = LANES
    tabs = _rope_tables(max(S, P + rn))
    cqn, ckv, kpe_pad, kpe_t, qb, kb, vb, kb_tail, vb_tail, k, vt = _proj(
        x_prompt, norm_mix, w, *gains, keep=keep_p, tm=_row_tile(S, ROWS_PROJ), kv=(*kv_args, *tabs))
    q = _q_up(cqn, *q_args, *tabs, tm=_row_tile(S, ROWS_Q_UP))
    tqa = _row_tile(S, ROWS_FLASH)
    flash = lambda bounded: functools.partial(_mla_flash, tq=tqa, td=min(tqa, ROWS_FLASH_DIAG), hg=2,
                                              bounded=bounded)
    late = (w_o[0], w_up[0], w_down[0])
    ride = [_can_ride(a, _flash_steps(B, ha, S, tqa, 2)) for a in late]
    oa, *cast = lax.cond(fast_a, flash(True), flash(False), q, k, vt, *(a for a, r in zip(late, ride) if r))
    cast = iter(cast)
    w["o"], w["up"], w["down"] = (next(cast) if r else a.astype(BF16) for a, r in zip(late, ride))
    u_tab = _band_vectors(rel_bias[0], shift_b)
    ob = lax.cond(fast_b, functools.partial(_band, bounded=True),
                  functools.partial(_band, bounded=False), qb, kb, vb, u_tab)
    h = _merge(x_prompt, oa, ob, w["o"], tm=_row_tile(S, ROWS_MERGE))
    y_prompt = _ffn(h, norm_ffn, w["up"], w["down"], tm=_row_tile(S, ROWS_FFN), tf=COLS_FFN)

    n_s = Bd * T
    xs = x_sample.reshape(1, n_s, D)
    cqn_s, ckv_s, kpe_pad_s, kpe_s, qb_s, kb_s, vb_s, kb_s32, vb_s32 = _proj(
        xs, norm_mix, w, *gains, keep=n_s, tm=_row_tile(n_s, ROWS_PROJ_SAMPLE))
    tabs_s = tuple(np.tile(t[P:P + T], (Bd, 1)) for t in tabs)
    q_s = _q_up(cqn_s, *q_args, *tabs_s, tm=n_s)
    oa_s = _mla_sample(
        q_s, cache_mla_ckv[0], jnp.swapaxes(cache_mla_kpe[0], 1, 2),
        _pad_rows(ckv_s.reshape(Bd, T, KV_LORA), rn), _pad_rows(kpe_pad_s.reshape(Bd, T, LANES), rn),
        *kv_args, tabs, t_new=T)
    w_c, w_n = _band_sample_vectors(rel_bias[0], Lb)
    ob_s = _band_sample(
        qb_s.reshape(Bd, T, hw), cache_band_k.reshape(Bd, Lb * hb, D_BAND), cache_band_v.reshape(Bd, Lb * hb, D_BAND),
        kb_s.reshape(Bd, T * hb, D_BAND), vb_s.reshape(Bd, T * hb, D_BAND), w_c, w_n, n_heads=hb)
    h_s = _merge(xs, oa_s.reshape(1, n_s, ha * V_MLA), ob_s.reshape(1, n_s, hw), w["o"], tm=n_s)
    y_sample = _ffn(h_s, norm_ffn, w["up"], w["down"], tm=n_s, tf=COLS_FFN).reshape(Bd, T, D)

    return (y_prompt, y_sample,
            ckv[None], jnp.swapaxes(kpe_t, 1, 2)[None],
            kb_tail.reshape(B, keep_p, hb, D_BAND)[None], vb_tail.reshape(B, keep_p, hb, D_BAND)[None],
            ckv_s.reshape(Bd, T, KV_LORA)[None], kpe_s.reshape(Bd, T, ROPE)[None],
            kb_s32.reshape(Bd, T, hb, D_BAND)[None], vb_s32.reshape(Bd, T, hb, D_BAND)[None])
```

```python
import functools

import jax
import jax.numpy as jnp
import numpy as np
from jax import lax
from jax.experimental import pallas as pl
from jax.experimental.pallas import tpu as pltpu

CHUNK = 64
EPS = 1e-6
NOPE = 128
ROPE = 64
HALF_ROPE = ROPE // 2
QK_MLA = NOPE + ROPE
V_MLA = 128
D_BAND = 128
BAND_CHUNKS = 8
BAND_PAST = BAND_CHUNKS * CHUNK
MAX_REL = 128
ROPE_BASE = 10000.0
Q_LORA = 512
KV_LORA = 256
NEG = -1e30
LOG2E = 1.4426950408889634

BOUND_MARGIN = 1.02
FAST_LIMIT = 60.0

LANES = 128
BF16_ROWS = 16
QK_PAD = 2 * LANES
AUG_LANE = ROPE
VMEM_LIMIT = 58 * 1024 * 1024
ROWS_W_IN = 256
ROWS_PROJ = 512
ROWS_PROJ_SAMPLE = 256
ROWS_Q_UP = 512
ROWS_FLASH = 1024
ROWS_FLASH_DIAG = 256
ROWS_MERGE = 512
ROWS_FFN = 512
COLS_FFN = 2048

BF16 = jnp.bfloat16
F32 = jnp.float32

NT_DIMS = (((1,), (1,)), ((), ()))


def _params(*sem):
    return pltpu.CompilerParams(dimension_semantics=sem, vmem_limit_bytes=VMEM_LIMIT)


def _resident(a):
    return pl.BlockSpec(a.shape, lambda *_: (0,) * a.ndim, pipeline_mode=pl.Buffered(1))


def _rms(x, g):
    ms = jnp.mean(x * x, axis=-1, keepdims=True)
    return x * lax.rsqrt(ms + EPS) * g


def _dot(a, b):
    return jnp.dot(a, b, preferred_element_type=F32)


def _dot_nt(a, b):
    return lax.dot_general(a, b, NT_DIMS, preferred_element_type=F32)


N_PROJ_IN, N_PROJ_OUT, N_KV_IN = 12, 9, 12


def _proj_kernel(*refs, n_heads, n_mla_heads, tail_start):
    (x_ref, nm_ref, wcq_ref, wckv_ref, wkpe_ref, wqb_ref, wkb_ref, wvb_ref,
     gcq_ref, gckv_ref, gqb_ref, gkb_ref) = refs[:N_PROJ_IN]
    n_kv = N_KV_IN if n_mla_heads else 0
    outs = refs[N_PROJ_IN + n_kv:]
    cqn_ref, ckv_ref, kpe_ref, kpe_out_ref, qb_ref, kb_ref, vb_ref, kbt_ref, vbt_ref = outs[:N_PROJ_OUT]
    t = pl.program_id(1)
    xn = _rms(x_ref[0], nm_ref[...]).astype(BF16)
    cqn = _rms(_dot(xn, wcq_ref[...]), gcq_ref[...]).astype(BF16)
    cqn_ref[0] = cqn
    ckv = _rms(_dot(xn, wckv_ref[...]), gckv_ref[...])
    ckv_ref[0] = ckv
    kpe = _dot(xn, wkpe_ref[...])
    kpe_ref[0] = kpe
    kpe_out_ref[0] = kpe.T[:ROPE] if n_mla_heads else kpe[:, :ROPE]
    if n_mla_heads:
        (wuk_ref, wuvt_ref, gn_ref, gr_ref, aug_ref, cos_ref, sin_ref,
         *q_refs) = refs[N_PROJ_IN:N_PROJ_IN + n_kv]
        k_ref, vt_ref, q_ref = outs[N_PROJ_OUT:]
        _q_heads(cqn, *q_refs, cos_ref, sin_ref, q_ref, n_mla_heads)
        keys, vt = _expand_kv(ckv, kpe, wuk_ref[...], wuvt_ref[...], gn_ref[...], gr_ref[...],
                              aug_ref[...], cos_ref[...], sin_ref[...], n_mla_heads)
        for h in range(n_mla_heads):
            k_ref[0, h] = keys[h]
            vt_ref[0, h] = vt[h * V_MLA:(h + 1) * V_MLA, :].astype(BF16)
    qb = _dot(xn, wqb_ref[...])
    kb = _dot(xn, wkb_ref[...])
    vb = _dot(xn, wvb_ref[...])
    vb_ref[0] = vb.astype(BF16)
    kbn = []
    for h in range(n_heads):
        sl = slice(h * D_BAND, (h + 1) * D_BAND)
        qb_ref[0, :, sl] = _rms(qb[:, sl], gqb_ref[...]).astype(BF16)
        kbn.append(_rms(kb[:, sl], gkb_ref[...]))
    kbn = jnp.concatenate(kbn, axis=-1)
    kb_ref[0] = kbn.astype(BF16)

    @pl.when(t >= tail_start)
    def _():
        kbt_ref[0] = kbn.reshape(kbt_ref.shape[1:])
        vbt_ref[0] = vb.reshape(vbt_ref.shape[1:])


def _proj(x, norm_mix, w, g_cq, g_ckv, g_qb, g_kb, *, keep, tm, kv=None):
    B, S, D = x.shape
    n_heads = w["qb"].shape[1] // D_BAND
    hw = n_heads * D_BAND
    nt = S // tm
    tail_start = nt - keep // tm
    row = lambda b, t: (b, t, 0)
    tail = lambda b, t: (b, jnp.maximum(t - tail_start, 0), 0, 0)
    if kv is not None:
        kpe_shape, kpe_spec = (B, ROPE, S), pl.BlockSpec((1, ROPE, tm), lambda b, t: (b, 0, t))
    else:
        kpe_shape, kpe_spec = (B, S, ROPE), pl.BlockSpec((1, tm, ROPE), row)
    out_shapes = (
        jax.ShapeDtypeStruct((B, S, Q_LORA), BF16),
        jax.ShapeDtypeStruct((B, S, KV_LORA), F32),
        jax.ShapeDtypeStruct((B, S, LANES), F32),
        jax.ShapeDtypeStruct(kpe_shape, F32),
        jax.ShapeDtypeStruct((B, S, hw), BF16),
        jax.ShapeDtypeStruct((B, S, hw), BF16),
        jax.ShapeDtypeStruct((B, S, hw), BF16),
        jax.ShapeDtypeStruct((B, keep, n_heads, D_BAND), F32),
        jax.ShapeDtypeStruct((B, keep, n_heads, D_BAND), F32),
    )
    out_specs = (
        pl.BlockSpec((1, tm, Q_LORA), row),
        pl.BlockSpec((1, tm, KV_LORA), row),
        pl.BlockSpec((1, tm, LANES), row),
        kpe_spec,
        pl.BlockSpec((1, tm, hw), row),
        pl.BlockSpec((1, tm, hw), row),
        pl.BlockSpec((1, tm, hw), row),
        pl.BlockSpec((1, tm, n_heads, D_BAND), tail, pipeline_mode=pl.Buffered(1)),
        pl.BlockSpec((1, tm, n_heads, D_BAND), tail, pipeline_mode=pl.Buffered(1)),
    )
    weights = (w["cq"], w["ckv"], w["kpe"], w["qb"], w["kb"], w["vb"])
    gains = (g_cq, g_ckv, g_qb, g_kb)
    kv_specs, kv_args, ha = [], (), 0
    if kv is not None:
        ha = kv[0].shape[1] // NOPE
        pos = lambda b, t: (t, 0)
        kv_specs = ([_resident(a) for a in kv[:5]] + [pl.BlockSpec((tm, LANES), pos)] * 2
                    + [_resident(a) for a in kv[7:]])
        kv_args = tuple(kv)
        head_rows = pl.BlockSpec((1, ha, tm, QK_PAD), lambda b, t: (b, 0, t, 0))
        out_shapes += (jax.ShapeDtypeStruct((B, ha, S, QK_PAD), BF16),
                       jax.ShapeDtypeStruct((B, ha, V_MLA, S), BF16),
                       jax.ShapeDtypeStruct((B, ha, S, QK_PAD), BF16))
        out_specs += (head_rows, pl.BlockSpec((1, ha, V_MLA, tm), lambda b, t: (b, 0, 0, t)), head_rows)
    return pl.pallas_call(
        functools.partial(_proj_kernel, n_heads=n_heads, n_mla_heads=ha, tail_start=tail_start),
        grid=(B, nt),
        in_specs=[pl.BlockSpec((1, tm, D), row), _resident(norm_mix)]
        + [_resident(a) for a in weights] + [_resident(a) for a in gains] + kv_specs,
        out_specs=out_specs,
        out_shape=out_shapes,
        compiler_params=_params("arbitrary", "arbitrary"),
        name="proj",
    )(x, norm_mix, *weights, *gains, *kv_args)


def _q_up_kernel(cqn_ref, w_ref, gn_ref, gr_ref, grs_ref, aug_ref, cos_ref, sin_ref, q_ref,
                 *, n_heads):
    _q_heads(cqn_ref[0], w_ref, gn_ref, gr_ref, grs_ref, aug_ref, cos_ref, sin_ref, q_ref, n_heads)


def _q_heads(cqn, w_ref, gn_ref, gr_ref, grs_ref, aug_ref, cos_ref, sin_ref, q_ref, n_heads):
    hw = n_heads * LANES
    hp = hw // 2
    y = _dot(cqn, w_ref[...])
    low = lax.broadcasted_iota(jnp.int32, (1, LANES), 1) < ROPE
    for j in range(n_heads // 2):
        r = y[:, hw + j * LANES: hw + (j + 1) * LANES]
        rs = y[:, hw + hp + j * LANES: hw + hp + (j + 1) * LANES]
        r2 = r * r
        r2_of = (jnp.where(low, r2, 0.0), jnp.where(low, 0.0, r2))
        rinv = []
        for e in range(2):
            h = 2 * j + e
            n = y[:, h * LANES:(h + 1) * LANES]
            ss = jnp.sum(n * n + r2_of[e], axis=-1, keepdims=True)
            rinv.append(lax.rsqrt(ss * (1.0 / QK_MLA) + EPS))
            q_ref[0, h, :, 0:LANES] = (n * rinv[e] * gn_ref[...]).astype(BF16)
        rot = ((r * gr_ref[...]) * cos_ref[...] + (rs * grs_ref[...]) * sin_ref[...]) \
            * jnp.where(low, rinv[0], rinv[1])
        q_ref[0, 2 * j, :, LANES:QK_PAD] = (jnp.where(low, rot, 0.0) + aug_ref[...]).astype(BF16)
        q_ref[0, 2 * j + 1, :, LANES:QK_PAD] = (
            jnp.where(low, pltpu.roll(rot, ROPE, 1), 0.0) + aug_ref[...]).astype(BF16)


def _q_up(cqn, w, g_n, g_r, g_rs, aug, cos, sin, *, tm):
    B, S, _ = cqn.shape
    n_heads = w.shape[1] // (2 * LANES)
    row = lambda b, t: (b, t, 0)
    pos = lambda b, t: (t, 0)
    return pl.pallas_call(
        functools.partial(_q_up_kernel, n_heads=n_heads),
        grid=(B, S // tm),
        in_specs=[pl.BlockSpec((1, tm, Q_LORA), row), _resident(w),
                  _resident(g_n), _resident(g_r), _resident(g_rs), _resident(aug),
                  pl.BlockSpec((tm, LANES), pos), pl.BlockSpec((tm, LANES), pos)],
        out_specs=pl.BlockSpec((1, n_heads, tm, QK_PAD), lambda b, t: (b, 0, t, 0)),
        out_shape=jax.ShapeDtypeStruct((B, n_heads, S, QK_PAD), BF16),
        compiler_params=_params("arbitrary", "arbitrary"),
        name="q_up",
    )(cqn, w, g_n, g_r, g_rs, aug, cos, sin)


def _expand_kv(c, kpe, wuk, wuvt, g_n, g_r, k_aug, cos, sin, n_heads):
    cb = c.astype(BF16)
    kn = _dot(cb, wuk)
    vt = _dot_nt(wuvt, cb)
    a = kpe * g_r
    sspe = jnp.sum(kpe * kpe, axis=-1, keepdims=True)
    a_sw = pltpu.roll(a, HALF_ROPE, 1) + pltpu.roll(a, LANES - HALF_ROPE, 1)
    low = lax.broadcasted_iota(jnp.int32, (1, LANES), 1) < ROPE
    rot = jnp.where(low, a * cos + a_sw * sin, 0.0)
    keys = []
    for h in range(n_heads):
        n = kn[:, h * NOPE:(h + 1) * NOPE]
        ss = jnp.sum(n * n, axis=-1, keepdims=True) + sspe
        rinv = lax.rsqrt(ss * (1.0 / QK_MLA) + EPS)
        keys.append(jnp.concatenate([(n * rinv * g_n).astype(BF16),
                                     (rot * rinv + k_aug).astype(BF16)], axis=-1))
    return keys, vt


N_FLASH_IN = 3


def _mla_flash_kernel(*refs, tq, td, hg, bounded, n_riders):
    q_ref, k_ref, vt_ref = refs[:N_FLASH_IN]
    o_ref = refs[N_FLASH_IN + n_riders]
    m_sc, l_sc, acc_sc = refs[N_FLASH_IN + 1 + 2 * n_riders:]
    for src, dst in zip(refs[N_FLASH_IN:N_FLASH_IN + n_riders],
                        refs[N_FLASH_IN + 1 + n_riders:N_FLASH_IN + 1 + 2 * n_riders]):
        dst[...] = src[...].astype(BF16)
    i = pl.program_id(2)
    if not bounded:
        m_sc[...] = jnp.full(m_sc.shape, -jnp.inf, F32)
    l_sc[...] = jnp.zeros(l_sc.shape, F32)
    acc_sc[...] = jnp.zeros(acc_sc.shape, F32)

    def step(tiles):
        sts = [(hh, t, _dot_nt(k_ref[0, hh, pl.ds(t[0], t[1]), :], q_ref[0, hh, t[2]:, :]))
               for t in tiles for hh in range(hg)]
        for hh, (k0, kn, q0, mask), st in sts:
            if mask is not None:
                st = jnp.where(mask, st, NEG)
            vt = vt_ref[0, hh, :, pl.ds(k0, kn)]
            if bounded:
                p = jnp.exp2(st)
                l_sc[hh, :, q0:] += jnp.sum(p, axis=0, keepdims=True)
                acc_sc[hh, :, q0:] += _dot(vt, p.astype(BF16))
            else:
                m_prev = m_sc[hh, :, q0:]
                m_new = jnp.maximum(m_prev, jnp.max(st, axis=0, keepdims=True))
                alpha = jnp.exp2(m_prev - m_new)
                p = jnp.exp2(st - m_new)
                l_sc[hh, :, q0:] = alpha * l_sc[hh, :, q0:] + jnp.sum(p, axis=0, keepdims=True)
                acc_sc[hh, :, q0:] = acc_sc[hh, :, q0:] * alpha + _dot(vt, p.astype(BF16))
                m_sc[hh, :, q0:] = m_new

    def body(j, carry):
        step([(pl.multiple_of(j * tq, tq), tq, 0, None)])
        return carry

    lax.fori_loop(0, i, body, 0)
    diag = []
    for d in range(tq // td):
        nq = tq - d * td
        q_chunk = lax.broadcasted_iota(jnp.int32, (td, nq), 1) // CHUNK
        k_chunk = lax.broadcasted_iota(jnp.int32, (td, nq), 0) // CHUNK
        diag.append((pl.multiple_of(i * tq + d * td, td), td, d * td, k_chunk <= q_chunk))
    step(diag)
    for hh in range(hg):
        o_ref[0, :, hh * V_MLA:(hh + 1) * V_MLA] = (acc_sc[hh] / l_sc[hh]).T.astype(o_ref.dtype)


def _flash_steps(B, H, S, tq, hg):
    return B * (H // hg) * (S // tq)


def _can_ride(a, n_steps):
    return a.shape[0] % n_steps == 0 and (a.shape[0] // n_steps) % BF16_ROWS == 0


def _mla_flash(q, k, vt, *riders, tq, td, hg, bounded):
    B, H, S, _ = q.shape
    nh, nq = H // hg, S // tq
    step = lambda b, h, i: ((b * nh + h) * nq + i, 0)
    rider_spec = lambda a: pl.BlockSpec((a.shape[0] // (B * nh * nq), a.shape[1]), step)
    return pl.pallas_call(
        functools.partial(_mla_flash_kernel, tq=tq, td=td, hg=hg, bounded=bounded, n_riders=len(riders)),
        grid=(B, nh, nq),
        in_specs=[pl.BlockSpec((1, hg, tq, QK_PAD), lambda b, h, i: (b, h, i, 0)),
                  pl.BlockSpec((1, hg, S, QK_PAD), lambda b, h, i: (b, h, 0, 0)),
                  pl.BlockSpec((1, hg, V_MLA, S), lambda b, h, i: (b, h, 0, 0))]
        + [rider_spec(a) for a in riders],
        out_specs=[pl.BlockSpec((1, tq, hg * V_MLA), lambda b, h, i: (b, i, h))]
        + [rider_spec(a) for a in riders],
        out_shape=[jax.ShapeDtypeStruct((B, S, H * V_MLA), BF16)]
        + [jax.ShapeDtypeStruct(a.shape, BF16) for a in riders],
        scratch_shapes=[pltpu.VMEM((hg, 1, tq), F32), pltpu.VMEM((hg, 1, tq), F32),
                        pltpu.VMEM((hg, V_MLA, tq), F32)],
        compiler_params=_params("arbitrary", "arbitrary", "arbitrary"),
        name="mla_flash_bounded" if bounded else "mla_flash",
    )(q, k, vt, *riders)


def _toeplitz(u_row, rows, cols, stride=1):
    u = jnp.broadcast_to(u_row, (rows, u_row.shape[-1]))
    return pltpu.roll(u, 0, 1, stride=stride, stride_axis=0)[:, :cols]


def _softmax_pv(parts, bounded):
    m = None
    if not bounded:
        for s, _ in parts:
            mx = jnp.max(s, axis=-1, keepdims=True)
            m = mx if m is None else jnp.maximum(m, mx)
    l = None
    acc = None
    for s, v in parts:
        p = jnp.exp2(s if bounded else s - m)
        ps = jnp.sum(p, axis=-1, keepdims=True)
        pv = _dot(p.astype(BF16), v)
        l = ps if l is None else l + ps
        acc = pv if acc is None else acc + pv
    return acc / l


def _band_kernel(q_ref, kc_ref, kp_ref, vc_ref, vp_ref, u_ref, o_ref, bc_sc, bp_sc,
                 *, n_heads, splits, bounded):
    b, g = pl.program_id(0), pl.program_id(1)
    tg = bc_sc.shape[1]

    @pl.when((b == 0) & (g == 0))
    def _():
        qc = lax.broadcasted_iota(jnp.int32, (tg, tg), 0) // CHUNK
        kc = lax.broadcasted_iota(jnp.int32, (tg, tg), 1) // CHUNK
        for h in range(n_heads):
            bc_sc[h] = jnp.where(kc <= qc, _toeplitz(u_ref[h, 0:1, :], tg, tg), NEG)
            bp_sc[h] = jnp.where(kc >= qc, _toeplitz(u_ref[h, 1:2, :], tg, tg), NEG)

    no_prev = jnp.where(g == 0, NEG, 0.0)
    tr = tg // splits
    for h0 in range(0, n_heads, 2):
        scores = []
        for h in (h0, h0 + 1):
            sl = slice(h * D_BAND, (h + 1) * D_BAND)
            for a in range(splits):
                rows, old, new = slice(a * tr, (a + 1) * tr), slice(a * tr, tg), slice(0, (a + 1) * tr)
                q = q_ref[0, rows, sl]
                scores.append((rows, old, new, sl,
                               _dot_nt(q, kp_ref[0, old, sl]) + (bp_sc[h, rows, old] + no_prev),
                               _dot_nt(q, kc_ref[0, new, sl]) + bc_sc[h, rows, new]))
        for rows, old, new, sl, s_prev, s_cur in scores:
            parts = [(s_prev, vp_ref[0, old, sl]), (s_cur, vc_ref[0, new, sl])]
            o_ref[0, rows, sl] = _softmax_pv(parts, bounded).astype(o_ref.dtype)


def _band(qb, kb, vb, u_tab, *, bounded):
    B, S, hw = qb.shape
    H = hw // D_BAND
    tg = BAND_PAST
    cur = lambda b, g: (b, g, 0)
    prev = lambda b, g: (b, jnp.maximum(g - 1, 0), 0)
    blk = (1, tg, hw)
    return pl.pallas_call(
        functools.partial(_band_kernel, n_heads=H, splits=4, bounded=bounded),
        grid=(B, S // tg),
        in_specs=[pl.BlockSpec(blk, cur), pl.BlockSpec(blk, cur), pl.BlockSpec(blk, prev),
                  pl.BlockSpec(blk, cur), pl.BlockSpec(blk, prev),
                  _resident(u_tab)],
        out_specs=pl.BlockSpec(blk, cur),
        out_shape=jax.ShapeDtypeStruct((B, S, hw), BF16),
        scratch_shapes=[pltpu.VMEM((H, tg, tg), F32), pltpu.VMEM((H, tg, tg), F32)],
        compiler_params=_params("arbitrary", "arbitrary"),
        name="band_bounded" if bounded else "band",
    )(qb, kb, kb, vb, vb, u_tab)


def _mla_sample_kernel(q_ref, cc_ref, pc_ref, cn_ref, pn_ref, wuk_ref, wuvt_ref, gn_ref, gr_ref,
                       aug_ref, cosc_ref, sinc_ref, cosn_ref, sinn_ref, o_ref, *, n_heads, t_new):
    args = (wuk_ref[...], wuvt_ref[...], gn_ref[...], gr_ref[...], aug_ref[...])
    pc = jnp.concatenate([pc_ref[0], jnp.zeros((LANES - ROPE, pc_ref.shape[2]), F32)], axis=0).T
    kc, vtc = _expand_kv(cc_ref[0], pc, *args, cosc_ref[...], sinc_ref[...], n_heads)
    rows_n = cosn_ref.shape[0]
    pad = lambda a: jnp.concatenate([a, jnp.zeros((rows_n - t_new, a.shape[1]), a.dtype)], axis=0)
    kn, vtn = _expand_kv(pad(cn_ref[0]), pad(pn_ref[0]), *args, cosn_ref[...], sinn_ref[...], n_heads)
    new_ok = lax.broadcasted_iota(jnp.int32, (q_ref.shape[2], rows_n), 1) < t_new
    scores = [(_dot_nt(q_ref[0, h], kc[h]), jnp.where(new_ok, _dot_nt(q_ref[0, h], kn[h]), NEG))
              for h in range(n_heads)]
    for h, (s_c, s_n) in enumerate(scores):
        m = jnp.maximum(jnp.max(s_c, axis=-1, keepdims=True), jnp.max(s_n, axis=-1, keepdims=True))
        p_c = jnp.exp2(s_c - m)
        p_n = jnp.exp2(s_n - m)
        l = jnp.sum(p_c, axis=-1, keepdims=True) + jnp.sum(p_n, axis=-1, keepdims=True)
        sl = slice(h * V_MLA, (h + 1) * V_MLA)
        pv = (_dot_nt(p_c.astype(BF16), vtc[sl].astype(BF16))
              + _dot_nt(p_n.astype(BF16), vtn[sl].astype(BF16)))
        o_ref[0, :, sl] = (pv / l).astype(o_ref.dtype)


def _mla_sample(q, ckv_c, kpe_c, ckv_n, kpe_n, w_uk, w_uvt, g_n, g_r, k_aug, tabs, *, rn):
    Bd = ckv_c.shape[0]
    H, T = q.shape[1], q.shape[2] // Bd
    P = ckv_c.shape[1]
    assert P % rn == 0 and T <= rn
    old = lambda: pl.BlockSpec((P, LANES), lambda b: (0, 0), pipeline_mode=pl.Buffered(1))
    new = lambda: pl.BlockSpec((rn, LANES), lambda b: (P // rn, 0), pipeline_mode=pl.Buffered(1))
    bat = lambda b: (b, 0, 0)
    return pl.pallas_call(
        functools.partial(_mla_sample_kernel, n_heads=H, t_new=T),
        grid=(Bd,),
        in_specs=[pl.BlockSpec((1, H, T, QK_PAD), lambda b: (0, 0, b, 0)),
                  pl.BlockSpec((1, P, KV_LORA), bat), pl.BlockSpec((1, ROPE, P), bat),
                  pl.BlockSpec((1, T, KV_LORA), bat), pl.BlockSpec((1, T, LANES), bat),
                  _resident(w_uk), _resident(w_uvt), _resident(g_n), _resident(g_r), _resident(k_aug),
                  old(), old(), new(), new()],
        out_specs=pl.BlockSpec((1, T, H * V_MLA), bat),
        out_shape=jax.ShapeDtypeStruct((Bd, T, H * V_MLA), BF16),
        compiler_params=_params("arbitrary"),
        name="mla_sample",
    )(q, ckv_c, kpe_c, ckv_n, kpe_n, w_uk, w_uvt, g_n, g_r, k_aug, *tabs, *tabs)


def _band_sample_kernel(q_ref, kc_ref, vc_ref, kn_ref, vn_ref, wc_ref, wn_ref, o_ref, *, n_heads):
    T = q_ref.shape[1]
    heads = [slice(h * D_BAND, (h + 1) * D_BAND) for h in range(n_heads)]
    q = jnp.concatenate([q_ref[0, :, sl] for sl in heads], axis=0)
    tab = lambda w_ref, cols: jnp.concatenate(
        [_toeplitz(w_ref[h:h + 1, :], T, cols, stride=n_heads) for h in range(n_heads)], axis=0)
    s_c = _dot_nt(q, kc_ref[0].astype(BF16)) + tab(wc_ref, kc_ref.shape[1])
    s_n = _dot_nt(q, kn_ref[0].astype(BF16)) + tab(wn_ref, kn_ref.shape[1])
    parts = [(s_c, vc_ref[0].astype(BF16)), (s_n, vn_ref[0].astype(BF16))]
    o = _softmax_pv(parts, False).astype(o_ref.dtype)
    for h, sl in enumerate(heads):
        o_ref[0, :, sl] = o[h * T:(h + 1) * T]


def _band_sample(q, k_c, v_c, k_n, v_n, w_c, w_n, *, n_heads):
    Bd = q.shape[0]
    bat = lambda b: (b, 0, 0)
    blk = lambda a: pl.BlockSpec((1,) + a.shape[1:], bat)
    return pl.pallas_call(
        functools.partial(_band_sample_kernel, n_heads=n_heads),
        grid=(Bd,),
        in_specs=[blk(q), blk(k_c), blk(v_c), blk(k_n), blk(v_n), _resident(w_c), _resident(w_n)],
        out_specs=blk(q),
        out_shape=jax.ShapeDtypeStruct(q.shape, BF16),
        compiler_params=_params("arbitrary"),
        name="band_sample",
    )(q, k_c, v_c, k_n, v_n, w_c, w_n)


def _merge_kernel(x_ref, oa_ref, ob_ref, wo_ref, h_ref):
    half = oa_ref.shape[-1]
    h_ref[0] = x_ref[0] + _dot(oa_ref[0], wo_ref[:half, :]) + _dot(ob_ref[0], wo_ref[half:, :])


def _merge(x, oa, ob, w_o, *, tm):
    B, S, D = x.shape
    row = lambda b, t: (b, t, 0)
    return pl.pallas_call(
        _merge_kernel,
        grid=(B, S // tm),
        in_specs=[pl.BlockSpec((1, tm, D), row), pl.BlockSpec((1, tm, oa.shape[-1]), row),
                  pl.BlockSpec((1, tm, ob.shape[-1]), row),
                  _resident(w_o)],
        out_specs=pl.BlockSpec((1, tm, D), row),
        out_shape=jax.ShapeDtypeStruct((B, S, D), F32),
        compiler_params=_params("arbitrary", "arbitrary"),
        name="merge",
    )(x, oa, ob, w_o)


def _ffn_kernel(h_ref, g_ref, wup_ref, wdn_ref, y_ref, hn_sc):
    j = pl.program_id(2)

    @pl.when(j == 0)
    def _():
        h = h_ref[0]
        hn_sc[...] = _rms(h, g_ref[...]).astype(BF16)
        y_ref[0] = h

    u = jnp.maximum(_dot(hn_sc[...], wup_ref[...]), 0.0)
    y_ref[0] += _dot((u * u).astype(BF16), wdn_ref[...])


def _ffn(h, g, w_up, w_down, *, tm, tf):
    B, S, D = h.shape
    F = w_up.shape[1]
    row = lambda b, t, j: (b, t, 0)
    return pl.pallas_call(
        _ffn_kernel,
        grid=(B, S // tm, F // tf),
        in_specs=[pl.BlockSpec((1, tm, D), row), _resident(g),
                  pl.BlockSpec((D, tf), lambda b, t, j: (0, j)),
                  pl.BlockSpec((tf, D), lambda b, t, j: (j, 0))],
        out_specs=pl.BlockSpec((1, tm, D), row),
        out_shape=jax.ShapeDtypeStruct((B, S, D), F32),
        scratch_shapes=[pltpu.VMEM((tm, D), BF16)],
        compiler_params=_params("arbitrary", "arbitrary", "arbitrary"),
        name="ffn",
    )(h, g, w_up, w_down)


def _rope_tables(n_pos):
    inv = 1.0 / (ROPE_BASE ** (np.arange(0, ROPE, 2, dtype=np.float64) / ROPE))
    ang = np.arange(n_pos, dtype=np.float64)[:, None] * inv
    c, s = np.cos(ang), np.sin(ang)
    return (np.concatenate([c, c, c, c], axis=1).astype(np.float32),
            np.concatenate([-s, s, -s, s], axis=1).astype(np.float32))


def _pad_lanes(a, width=LANES):
    return jnp.pad(a, [(0, 0)] * (a.ndim - 1) + [(0, width - a.shape[-1])])


def _swap_halves(a):
    return jnp.concatenate([a[..., HALF_ROPE:], a[..., :HALF_ROPE]], axis=-1)


def _split_w_in_kernel(wt_ref, *out_refs, bounds):
    for o_ref, (lo, hi) in zip(out_refs, bounds):
        width = o_ref.shape[1]
        piece = wt_ref[lo:lo + width, :].T
        if hi - lo < width:
            lane = lax.broadcasted_iota(jnp.int32, piece.shape, 1)
            piece = jnp.where(lane < hi - lo, piece, 0.0)
        o_ref[...] = piece.astype(BF16)


def _split_w_in(w_in_t, *, tm):
    cols, D = w_in_t.shape
    hb3 = (cols - Q_LORA - KV_LORA - ROPE) // 3
    names = ("cq", "ckv", "kpe", "qb", "kb", "vb")
    widths = (Q_LORA, KV_LORA, LANES, hb3, hb3, hb3)
    starts = [0, Q_LORA, Q_LORA + KV_LORA, Q_LORA + KV_LORA + ROPE]
    starts += [starts[3] + hb3, starts[3] + 2 * hb3, starts[3] + 3 * hb3]
    assert starts[-1] == cols and D % tm == 0
    outs = pl.pallas_call(
        functools.partial(_split_w_in_kernel, bounds=tuple(zip(starts[:-1], starts[1:]))),
        grid=(D // tm,),
        in_specs=[pl.BlockSpec((cols, tm), lambda i: (0, i))],
        out_specs=[pl.BlockSpec((tm, n), lambda i: (i, 0)) for n in widths],
        out_shape=[jax.ShapeDtypeStruct((D, n), BF16) for n in widths],
        compiler_params=_params("arbitrary"),
        name="split_w_in",
    )(w_in_t)
    return dict(zip(names, outs))


def _prep_weights(w_in, w_uq, w_uk, w_uv):
    w = _split_w_in(w_in.T, tm=ROWS_W_IN)
    ha = w_uq.shape[1] // QK_MLA
    uq = w_uq.astype(BF16).reshape(Q_LORA, ha, QK_MLA)
    nope = uq[:, :, :NOPE].reshape(Q_LORA, ha * NOPE)
    rope = uq[:, :, NOPE:]
    w["uq"] = jnp.concatenate(
        [nope, rope.reshape(Q_LORA, ha * ROPE), _swap_halves(rope).reshape(Q_LORA, ha * ROPE)], axis=1)
    w["uk"] = w_uk.astype(BF16)
    w["uvt"] = w_uv.astype(BF16).T
    return w


def _softmax_setup(g_qa, g_ka, g_qb, g_kb, rel_bias):
    c_a = QK_MLA ** -0.5 * LOG2E
    c_b = D_BAND ** -0.5 * LOG2E
    amax = lambda a: jnp.max(jnp.abs(a))
    bound_a = c_a * QK_MLA * amax(g_qa) * amax(g_ka) * BOUND_MARGIN
    bound_b = c_b * D_BAND * amax(g_qb) * amax(g_kb) * BOUND_MARGIN + LOG2E * amax(rel_bias)
    fast_a = bound_a <= FAST_LIMIT
    fast_b = bound_b <= FAST_LIMIT
    shift_a = jnp.where(fast_a, bound_a, 0.0)
    shift_b = jnp.where(fast_b, bound_b, 0.0)
    lane = jnp.arange(LANES)[None, :]
    twice = lambda a: jnp.concatenate([a, a], axis=-1)
    g = {"qa_n": g_qa[None, :NOPE] * c_a, "qa_r": twice(g_qa[None, NOPE:]) * c_a,
         "qa_rs": twice(_swap_halves(g_qa[None, NOPE:])) * c_a,
         "q_aug": (lane == AUG_LANE).astype(F32),
         "ka_n": g_ka[None, :NOPE], "ka_r": _pad_lanes(g_ka[None, NOPE:]),
         "k_aug": jnp.where(lane == AUG_LANE, -shift_a, 0.0).astype(F32),
         "qb": g_qb * c_b}
    return g, fast_a, fast_b, shift_b


def _take_static(a, idx):
    idx = np.asarray(idx)
    steps = np.diff(idx)
    pieces, i = [], 0
    while i < len(idx):
        step = int(steps[i]) if i < len(steps) and steps[i] in (-1, 1) else 0
        j = i + 1
        while j < len(idx) and idx[j] - idx[j - 1] == step:
            j += 1
        lo, hi = sorted((int(idx[i]), int(idx[j - 1])))
        run = a[:, lo:hi + 1]
        pieces.append(jnp.broadcast_to(run, (a.shape[0], j - i)) if step == 0 else run[:, ::step])
        i = j
    return jnp.concatenate(pieces, axis=1)


def _band_vectors(rel_bias, shift):
    t = BAND_PAST
    e = np.arange(2 * t)
    e = np.where(e < t, e, e - 2 * t)
    rows = [_take_static(rel_bias, np.clip(d, -MAX_REL, MAX_REL) + MAX_REL) for d in (-e, t - e)]
    return (jnp.stack(rows, axis=1) * LOG2E - shift).astype(F32)


def _band_sample_vectors(rel_bias, lb):
    H = rel_bias.shape[0]

    def interleave(dist, n_pos, n):
        e = np.arange(n)
        e = np.where(e < n_pos, e, e - n)
        u = _take_static(rel_bias, np.clip(dist - e, -MAX_REL, MAX_REL) + MAX_REL) * LOG2E
        own = np.arange(H)[:, None, None] == np.arange(H)[None, None, :]
        return jnp.where(own, u[:, :, None], NEG).reshape(H, n * H).astype(F32)

    return interleave(lb, lb, lb + LANES), interleave(0, CHUNK, 2 * CHUNK)


def _row_tile(n, pref):
    return pref if n % pref == 0 else n


def kernel(x_prompt, x_sample, cache_mla_ckv, cache_mla_kpe, cache_band_k, cache_band_v,
           norm_mix, w_in, g_cq, w_uq, g_ckv, w_uk, w_uv, g_qa, g_ka, g_qb, g_kb, rel_bias,
           w_o, norm_ffn, w_up, w_down):
    depth = w_in.shape[0]
    assert depth == 1, "single-layer step"
    B, S, D = x_prompt.shape
    Bd, T, _ = x_sample.shape
    P = cache_mla_ckv.shape[2]
    Lb = cache_band_k.shape[2]
    keep_p = min(BAND_PAST, S)
    assert S % BAND_PAST == 0 and T <= CHUNK

    w = _prep_weights(w_in[0], w_uq[0], w_uk[0], w_uv[0])
    g, fast_a, fast_b, shift_b = _softmax_setup(g_qa[0], g_ka[0], g_qb, g_kb, rel_bias[0])
    ha = w["uk"].shape[1] // NOPE
    hb = rel_bias.shape[1]
    hw = hb * D_BAND
    gains = (g_cq, g_ckv, g["qb"], g_kb)
    q_args = (w["uq"], g["qa_n"], g["qa_r"], g["qa_rs"], g["q_aug"])
    kv_args = (w["uk"], w["uvt"], g["ka_n"], g["ka_r"], g["k_aug"])

    rn = LANES
    tabs = _rope_tables(max(S, P + rn))
    _, ckv, kpe_pad, kpe_t, qb, kb, vb, kb_tail, vb_tail, k, vt, q = _proj(
        x_prompt, norm_mix, w, *gains, keep=keep_p, tm=_row_tile(S, ROWS_PROJ),
        kv=(*kv_args, *tabs, *q_args))
    tqa = _row_tile(S, ROWS_FLASH)
    flash = lambda bounded: functools.partial(_mla_flash, tq=tqa, td=min(tqa, ROWS_FLASH_DIAG), hg=2,
                                              bounded=bounded)
    late = (w_o[0], w_up[0], w_down[0])
    ride = [_can_ride(a, _flash_steps(B, ha, S, tqa, 2)) for a in late]
    oa, *cast = lax.cond(fast_a, flash(True), flash(False), q, k, vt, *(a for a, r in zip(late, ride) if r))
    cast = iter(cast)
    w["o"], w["up"], w["down"] = (next(cast) if r else a.astype(BF16) for a, r in zip(late, ride))
    u_tab = _band_vectors(rel_bias[0], shift_b)
    ob = lax.cond(fast_b, functools.partial(_band, bounded=True),
                  functools.partial(_band, bounded=False), qb, kb, vb, u_tab)
    h = _merge(x_prompt, oa, ob, w["o"], tm=_row_tile(S, ROWS_MERGE))
    y_prompt = _ffn(h, norm_ffn, w["up"], w["down"], tm=_row_tile(S, ROWS_FFN), tf=COLS_FFN)

    n_s = Bd * T
    xs = x_sample.reshape(1, n_s, D)
    cqn_s, ckv_s, kpe_pad_s, kpe_s, qb_s, _, _, kb_s32, vb_s32 = _proj(
        xs, norm_mix, w, *gains, keep=n_s, tm=_row_tile(n_s, ROWS_PROJ_SAMPLE))
    tabs_s = tuple(np.tile(t[P:P + T], (Bd, 1)) for t in tabs)
    q_s = _q_up(cqn_s, *q_args, *tabs_s, tm=n_s)
    oa_s = _mla_sample(
        q_s, cache_mla_ckv[0], jnp.swapaxes(cache_mla_kpe[0], 1, 2),
        ckv_s.reshape(Bd, T, KV_LORA), kpe_pad_s.reshape(Bd, T, LANES),
        *kv_args, tabs, rn=rn)
    w_c, w_n = _band_sample_vectors(rel_bias[0], Lb)
    ob_s = _band_sample(
        qb_s.reshape(Bd, T, hw), cache_band_k.reshape(Bd, Lb * hb, D_BAND), cache_band_v.reshape(Bd, Lb * hb, D_BAND),
        kb_s32.reshape(Bd, T * hb, D_BAND), vb_s32.reshape(Bd, T * hb, D_BAND), w_c, w_n, n_heads=hb)
    h_s = _merge(xs, oa_s.reshape(1, n_s, ha * V_MLA), ob_s.reshape(1, n_s, hw), w["o"], tm=n_s)
    y_sample = _ffn(h_s, norm_ffn, w["up"], w["down"], tm=n_s, tf=COLS_FFN).reshape(Bd, T, D)

    return (y_prompt, y_sample,
            ckv[None], jnp.swapaxes(kpe_t, 1, 2)[None],
            kb_tail.reshape(B, keep_p, hb, D_BAND)[None], vb_tail.reshape(B, keep_p, hb, D_BAND)[None],
            ckv_s.reshape(Bd, T, KV_LORA)[None], kpe_s.reshape(Bd, T, ROPE)[None],
            kb_s32.reshape(Bd, T, hb, D_BAND)[None], vb_s32.reshape(Bd, T, hb, D_BAND)[None])
```
